```python
import math
import jax
import jax.numpy as jnp
from jax import lax
import numpy as np

D_MODEL = 1024
BATCH = 4
SEQ = 4096
DEPTH = 4
DEC_BATCH = 128
DEC_SEQ = 8
PAST_LEN = 2048
PAGE_SIZE = 128

HEAD_DIM = 64
MIX_W = D_MODEL // 2
GLA_HEADS = 4
GLA_DV = MIX_W // GLA_HEADS
GLA_DK = GLA_DV // 2
GLA_GATE_RANK = 16
GLA_TAU = 16.0
GLA_CHUNK = 64
NSA_HEADS = MIX_W // HEAD_DIM
NSA_KV_HEADS = 2
CMP_STRIDE = 16
CMP_BLOCK = 32
CMP_HIDDEN = 128
SEL_BLOCK = 64
SEL_TOPN = 16
WINDOW = 512
FORCE_BONUS = 1e4
SB_HEADS = MIX_W // HEAD_DIM
REL_BUCKETS = 32
REL_MAX_DIST = 128
D_FF = -(-8 * D_MODEL // (3 * 256)) * 256
Q_BLK = 128
LN_EPS = 1e-5
DN_ALPHA = (2 * DEPTH) ** 0.25
DN_BETA = (8 * DEPTH) ** -0.25
GLA_QK_W = GLA_HEADS * GLA_DK
NSA_KV_W = NSA_KV_HEADS * HEAD_DIM
IN_WIDTHS = (GLA_QK_W, GLA_QK_W, MIX_W, MIX_W, GLA_GATE_RANK,
             MIX_W, NSA_KV_W, NSA_KV_W, NSA_KV_W, NSA_KV_W, NSA_KV_W, NSA_KV_W, 3 * NSA_HEADS,
             MIX_W, MIX_W, MIX_W,
             3 * D_MODEL)
D_IN = sum(IN_WIDTHS)

kernel_name = 'hybrid_gla_nsa_stickbreak_decoder_step'


def layer_norm(x, g, b):
    xf = x.astype(jnp.float32)
    mu = jnp.mean(xf, -1, keepdims=True)
    var = jnp.mean(jnp.square(xf - mu), -1, keepdims=True)
    return ((xf - mu) * lax.rsqrt(var + LN_EPS) * g + b).astype(x.dtype)


def head_norm(o):
    mu = jnp.mean(o, -1, keepdims=True)
    var = jnp.mean(jnp.square(o - mu), -1, keepdims=True)
    return (o - mu) * lax.rsqrt(var + LN_EPS)


def t5_bucket(dist):
    n = jnp.maximum(dist, 0)
    exact = REL_BUCKETS // 2
    scaled = jnp.log(jnp.maximum(n, 1).astype(jnp.float32) / exact) / math.log(REL_MAX_DIST / exact)
    large = jnp.minimum(exact + (scaled * (REL_BUCKETS - exact)).astype(jnp.int32), REL_BUCKETS - 1)
    return jnp.where(n < exact, n, large)


def masked_softmax(logits, mask):
    logits = jnp.where(mask, logits.astype(jnp.float32), -1e30)
    m = jnp.max(logits, -1, keepdims=True)
    e = jnp.where(mask, jnp.exp(logits - m), 0.0)
    return e / jnp.maximum(jnp.sum(e, -1, keepdims=True), 1e-30)


def over_query_blocks(fn, pos_q, *qs):
    T = pos_q.shape[0]
    if T <= Q_BLK:
        return fn(pos_q, *qs)
    n = T // Q_BLK

    def split(a):
        return jnp.moveaxis(a.reshape(a.shape[0], n, Q_BLK, *a.shape[2:]), 1, 0)

    out = lax.map(lambda args: fn(*args), (pos_q.reshape(n, Q_BLK),) + tuple(split(a) for a in qs))
    return jnp.moveaxis(out, 0, 1).reshape(out.shape[1], T, *out.shape[3:])


def gla_chunked(q, k, v, log_a, s0):
    B, T, H, DK = q.shape
    DV = v.shape[-1]
    C = math.gcd(T, GLA_CHUNK)
    n = T // C

    def chunks(a):
        return a.reshape(B, n, C, H, a.shape[-1]).transpose(1, 0, 3, 2, 4)

    causal = jnp.tril(jnp.ones((C, C), dtype=bool))

    def step(S, inp):
        qc, kc, vc, ac = inp
        cum = jnp.cumsum(ac, axis=2)
        rel = jnp.where(causal[:, :, None], cum[:, :, :, None, :] - cum[:, :, None, :, :], -jnp.inf)
        att = jnp.einsum('bhtk,bhsk,bhtsk->bhts', qc, kc, jnp.exp(rel))
        o = jnp.einsum('bhts,bhsv->bhtv', att, vc) + jnp.einsum('bhtk,bhkv->bhtv', qc * jnp.exp(cum), S)
        last = cum[:, :, -1:, :]
        S = jnp.exp(last[:, :, 0, :, None]) * S + jnp.einsum('bhsk,bhsv->bhkv', kc * jnp.exp(last - cum), vc)
        return S, o

    S, o = lax.scan(step, s0, tuple(chunks(a) for a in (q, k, v, log_a)))
    return o.transpose(1, 0, 3, 2, 4).reshape(B, T, H, DV), S


def nsa_compress(kv, w1, b1, w2, pe):
    B, L, _, G, dh = kv.shape
    R = CMP_BLOCK // CMP_STRIDE
    n_seg = L // CMP_STRIDE
    n_cmp = n_seg - R + 1
    seg = kv[:, :n_seg * CMP_STRIDE].reshape(B, n_seg, CMP_STRIDE, 2, G, dh).astype(jnp.float32)
    h = b1[:, None, :]
    for r in range(R):
        sl = slice(r * CMP_STRIDE, (r + 1) * CMP_STRIDE)
        h = h + jnp.einsum('bnjcgd,jcdh->bncgh', seg[:, r:r + n_cmp] + pe[sl][:, :, None, :], w1[sl])
    return jnp.einsum('bncgh,chd->bncgd', jax.nn.gelu(h), w2)


def nsa_cmp_attend(q, pos_q, kvc, rel_bias):
    B, T, H, dh = q.shape
    N, G = kvc.shape[1], kvc.shape[3]
    HG = H // G
    qg = q.reshape(B, T, G, HG, dh)
    logits = jnp.einsum('btgjd,bngd->bgjtn', qg, kvc[:, :, 0], preferred_element_type=jnp.float32) * dh ** -0.5
    end = jnp.arange(N) * CMP_STRIDE + CMP_BLOCK - 1
    dist = pos_q[:, None] - end[None, :]
    bias = rel_bias[t5_bucket(dist)].reshape(T, N, G, HG).transpose(2, 3, 0, 1)
    p = masked_softmax(logits + bias, dist >= 0)
    o = jnp.einsum('bgjtn,bngd->btgjd', p, kvc[:, :, 1]).reshape(B, T, H, dh)
    return o, p


def nsa_select(p_cmp, pos_q, n_slc):
    N = p_cmp.shape[-1]
    start = np.arange(N) * CMP_STRIDE
    blk = np.arange(n_slc) * SEL_BLOCK
    cover = ((start[:, None] < blk[None, :] + SEL_BLOCK) & (start[:, None] + CMP_BLOCK > blk[None, :])).astype(np.float32)
    score = jnp.einsum('bgjtn,nm->btgm', p_cmp, jnp.asarray(cover))
    cur = (pos_q // SEL_BLOCK)[:, None]
    j = jnp.arange(n_slc)[None, :]
    forced = (j == 0) | (j == cur) | (j == cur - 1)
    score = jnp.where(forced[:, None, :], score + FORCE_BONUS, score)
    score = jnp.where((j <= cur)[:, None, :], score, -jnp.inf)
    _, idx = lax.top_k(score, min(SEL_TOPN, n_slc))
    return idx


def nsa_sel_attend(q, pos_q, idx, kv_blocks, rel_bias):
    B, T, H, dh = q.shape
    G, K = idx.shape[2], idx.shape[3]
    HG = H // G
    b_i = jnp.arange(B)[:, None, None, None]
    g_i = jnp.arange(G)[None, None, :, None]
    sel = kv_blocks[b_i, g_i, idx].reshape(B, T, G, K * SEL_BLOCK, 2, dh)
    pos_k = (idx[..., None] * SEL_BLOCK + jnp.arange(SEL_BLOCK)).reshape(B, T, G, K * SEL_BLOCK)
    dist = pos_q[None, :, None, None] - pos_k
    qg = q.reshape(B, T, G, HG, dh)
    logits = jnp.einsum('btgjd,btgsd->btgjs', qg, sel[..., 0, :], preferred_element_type=jnp.float32) * dh ** -0.5
    bias = jnp.swapaxes(rel_bias.reshape(REL_BUCKETS, G, HG)[t5_bucket(dist), g_i], -1, -2)
    p = masked_softmax(logits + bias, (dist >= 0)[:, :, :, None, :])
    return jnp.einsum('btgjs,btgsd->btgjd', p, sel[..., 1, :]).reshape(B, T, H, dh)


def window_attend(q, pos_q, kv, pos_k, valid, rel_bias):
    B, T, H, dh = q.shape
    S, G = kv.shape[1], kv.shape[3]
    HG = H // G
    qg = q.reshape(B, T, G, HG, dh)
    logits = jnp.einsum('btgjd,bsgd->bgjts', qg, kv[:, :, 0], preferred_element_type=jnp.float32) * dh ** -0.5
    dist = pos_q[:, None] - pos_k[None, :]
    mask = valid[None, :] & (dist >= 0) & (dist <= WINDOW)
    bias = rel_bias[t5_bucket(dist)].reshape(T, S, G, HG).transpose(2, 3, 0, 1)
    p = masked_softmax(logits + bias, mask)
    return jnp.einsum('bgjts,bsgd->btgjd', p, kv[:, :, 1]).reshape(B, T, H, dh)


def sb_attend(q, pos_q, kv, pos_k):
    z = jnp.einsum('bthd,bshd->bhts', q, kv[:, :, 0], preferred_element_type=jnp.float32) * q.shape[-1] ** -0.5
    mask = pos_k[None, :] < pos_q[:, None]
    u = jnp.where(mask, jax.nn.log_sigmoid(-z), 0.0)
    rest = lax.cumsum(u, axis=3, reverse=True) - u
    a = jnp.where(mask, jnp.exp(jax.nn.log_sigmoid(z) + rest), 0.0)
    return jnp.einsum('bhts,bshd->bthd', a, kv[:, :, 1])


def trunk_layer(x, lw, rel_bias, past):
    B, T, _ = x.shape
    f32 = jnp.float32
    G = NSA_KV_HEADS
    past_len = past['cmp'].shape[1]
    L = past_len + T
    pos = past_len + jnp.arange(T)
    pos_all = jnp.arange(L)

    z = x @ lw['w_in'] + lw['b_in']
    (g_q, g_k, g_v, g_r, g_a, n_q, c_k, c_v, s_k, s_v, w_k, w_v, n_g,
     b_q, b_k, b_v, m_g) = jnp.split(z, np.cumsum(IN_WIDTHS)[:-1].tolist(), axis=-1)

    q = g_q.reshape(B, T, GLA_HEADS, GLA_DK).astype(f32) * GLA_DK ** -0.5
    k = g_k.reshape(B, T, GLA_HEADS, GLA_DK).astype(f32)
    v = g_v.reshape(B, T, GLA_HEADS, GLA_DV).astype(f32)
    log_a = jax.nn.log_sigmoid((g_a @ lw['w_gla_a2'] + lw['b_gla_a2']).astype(f32)).reshape(B, T, GLA_HEADS, GLA_DK) / GLA_TAU
    o, gla_state = gla_chunked(q, k, v, log_a, past['gla'].astype(f32))
    o_gla = (head_norm(o).reshape(B, T, MIX_W) * lw['gla_norm_g'] * jax.nn.silu(g_r.astype(f32))).astype(x.dtype)

    qn = n_q.reshape(B, T, NSA_HEADS, HEAD_DIM)
    kv_c_new = jnp.stack([c_k, c_v], 2).reshape(B, T, 2, G, HEAD_DIM)
    kv_s_new = jnp.stack([s_k, s_v], 2).reshape(B, T, 2, G, HEAD_DIM)
    kv_w_new = jnp.stack([w_k, w_v], 2).reshape(B, T, 2, G, HEAD_DIM)
    kvc = nsa_compress(jnp.concatenate([past['cmp'], kv_c_new], 1), lw['w_cmp1'], lw['b_cmp1'], lw['w_cmp2'], lw['cmp_pe'])
    o_cmp, p_cmp = nsa_cmp_attend(qn, pos, kvc, rel_bias)
    n_slc = -(-L // SEL_BLOCK)
    kv_s = jnp.concatenate([past['slc'], kv_s_new], 1)
    kv_s = jnp.pad(kv_s, ((0, 0), (0, n_slc * SEL_BLOCK - L), (0, 0), (0, 0), (0, 0)))
    kv_blocks = kv_s.reshape(B, n_slc, SEL_BLOCK, 2, G, HEAD_DIM).transpose(0, 4, 1, 2, 3, 5)
    idx = nsa_select(p_cmp, pos, n_slc)
    o_slc = over_query_blocks(lambda pb, qb, ib: nsa_sel_attend(qb, pb, ib, kv_blocks, rel_bias), pos, qn, idx)
    kv_w = jnp.concatenate([past['win'], kv_w_new], 1)
    Lw = kv_w.shape[1]
    p0 = past_len - past['win'].shape[1]
    kv_w_pad = jnp.pad(kv_w, ((0, 0), (WINDOW, 0), (0, 0), (0, 0), (0, 0)))
    pos_w = p0 - WINDOW + jnp.arange(WINDOW + Lw)
    valid_w = jnp.arange(WINDOW + Lw) >= WINDOW

    def win_block(pb, qb):
        start = pb[0] - p0
        n = WINDOW + pb.shape[0]
        return window_attend(qb, pb, lax.dynamic_slice_in_dim(kv_w_pad, start, n, 1),
                             lax.dynamic_slice_in_dim(pos_w, start, n),
                             lax.dynamic_slice_in_dim(valid_w, start, n), rel_bias)

    o_win = over_query_blocks(win_block, pos, qn)
    ng = jax.nn.sigmoid(n_g.astype(f32)).reshape(B, T, 3, NSA_HEADS, 1)
    o_nsa = (ng[:, :, 0] * o_cmp + ng[:, :, 1] * o_slc + ng[:, :, 2] * o_win).reshape(B, T, MIX_W).astype(x.dtype)

    qs = b_q.reshape(B, T, SB_HEADS, HEAD_DIM)
    kv_b_new = jnp.stack([b_k, b_v], 2).reshape(B, T, 2, SB_HEADS, HEAD_DIM)
    kv_b = jnp.concatenate([past['sb'], kv_b_new], 1)
    o_sb = over_query_blocks(lambda pb, qb: sb_attend(qb, pb, kv_b, pos_all), pos, qs).reshape(B, T, MIX_W).astype(x.dtype)

    proj = jnp.einsum('btmc,mcd->btmd', jnp.stack([o_gla, o_nsa, o_sb], 2), lw['w_branch'])
    gate = jax.nn.sigmoid(m_g.astype(f32)).reshape(B, T, 3, D_MODEL)
    mixed = jnp.sum(gate * proj, axis=2).astype(x.dtype) @ lw['w_o']
    x = layer_norm(DN_ALPHA * x + mixed, lw['ln1_g'], lw['ln1_b'])
    ff = (jax.nn.silu(x @ lw['w_ff_gate']) * (x @ lw['w_ff_up'])) @ lw['w_ff_down']
    x = layer_norm(DN_ALPHA * x + ff, lw['ln2_g'], lw['ln2_b'])
    return x, (kv_c_new, kv_s_new, kv_b_new, kv_w[:, Lw - min(WINDOW, Lw):], gla_state)


def gather_pages(pool, layer, page_table):
    g = pool[layer, page_table]
    return g.reshape(g.shape[0], g.shape[1] * g.shape[2], *g.shape[3:])


def setup_inputs(seed: int = 0) -> dict:
    key = jax.random.key(seed)
    keys = iter(jax.random.split(key, 40))
    f32 = jnp.float32

    def nrm(shape, scale=1.0):
        return scale * jax.random.normal(next(keys), shape, f32)

    n_pages = PAST_LEN // PAGE_SIZE
    n_phys = (DEC_BATCH * n_pages * 5) // 4
    page_table = jax.random.permutation(next(keys), n_phys)[:DEC_BATCH * n_pages].reshape(DEC_BATCH, n_pages).astype(jnp.int32)
    G = NSA_KV_HEADS
    return {
        'x_prompt': nrm((BATCH, SEQ, D_MODEL)),
        'x_sample': nrm((DEC_BATCH, DEC_SEQ, D_MODEL)),
        'cache_cmp_kv': nrm((DEPTH, n_phys, PAGE_SIZE, 2, G, HEAD_DIM)),
        'cache_slc_kv': nrm((DEPTH, n_phys, PAGE_SIZE, 2, G, HEAD_DIM)),
        'cache_sb_kv': nrm((DEPTH, n_phys, PAGE_SIZE, 2, SB_HEADS, HEAD_DIM)),
        'state_win_kv': nrm((DEPTH, DEC_BATCH, min(WINDOW, PAST_LEN), 2, G, HEAD_DIM)),
        'state_gla': nrm((DEPTH, DEC_BATCH, GLA_HEADS, GLA_DK, GLA_DV), 0.5),
        'page_table': page_table,
        'rel_bias': nrm((REL_BUCKETS, NSA_HEADS), 0.5),
        'w_in': nrm((DEPTH, D_MODEL, D_IN), D_MODEL ** -0.5),
        'b_in': nrm((DEPTH, D_IN), 0.02),
        'w_gla_a2': nrm((DEPTH, GLA_GATE_RANK, GLA_QK_W), GLA_GATE_RANK ** -0.5),
        'b_gla_a2': nrm((DEPTH, GLA_QK_W), 0.1),
        'gla_norm_g': 1.0 + nrm((DEPTH, MIX_W), 0.02),
        'w_cmp1': nrm((DEPTH, CMP_BLOCK, 2, HEAD_DIM, CMP_HIDDEN), (CMP_BLOCK * HEAD_DIM) ** -0.5),
        'b_cmp1': nrm((DEPTH, 2, CMP_HIDDEN), 0.02),
        'w_cmp2': nrm((DEPTH, 2, CMP_HIDDEN, HEAD_DIM), CMP_HIDDEN ** -0.5),
        'cmp_pe': nrm((DEPTH, CMP_BLOCK, 2, HEAD_DIM), 0.3),
        'w_branch': nrm((DEPTH, 3, MIX_W, D_MODEL), DN_BETA * MIX_W ** -0.5),
        'w_o': nrm((DEPTH, D_MODEL, D_MODEL), DN_BETA * D_MODEL ** -0.5),
        'ln1_g': 1.0 + nrm((DEPTH, D_MODEL), 0.02),
        'ln1_b': nrm((DEPTH, D_MODEL), 0.02),
        'w_ff_gate': nrm((DEPTH, D_MODEL, D_FF), D_MODEL ** -0.5),
        'w_ff_up': nrm((DEPTH, D_MODEL, D_FF), D_MODEL ** -0.5),
        'w_ff_down': nrm((DEPTH, D_FF, D_MODEL), DN_BETA * D_FF ** -0.5),
        'ln2_g': 1.0 + nrm((DEPTH, D_MODEL), 0.02),
        'ln2_b': nrm((DEPTH, D_MODEL), 0.02),
    }


def reference(x_prompt, x_sample, cache_cmp_kv, cache_slc_kv, cache_sb_kv, state_win_kv, state_gla, page_table,
              rel_bias, w_in, b_in, w_gla_a2, b_gla_a2, gla_norm_g, w_cmp1, b_cmp1, w_cmp2, cmp_pe,
              w_branch, w_o, ln1_g, ln1_b, w_ff_gate, w_ff_up, w_ff_down, ln2_g, ln2_b):
    B = x_prompt.shape[0]
    dt = x_prompt.dtype
    G = NSA_KV_HEADS
    yp, ys = x_prompt, x_sample
    st_p, st_s = [], []
    for l in range(DEPTH):
        lw = {'w_in': w_in[l], 'b_in': b_in[l], 'w_gla_a2': w_gla_a2[l], 'b_gla_a2': b_gla_a2[l],
              'gla_norm_g': gla_norm_g[l], 'w_cmp1': w_cmp1[l], 'b_cmp1': b_cmp1[l], 'w_cmp2': w_cmp2[l],
              'cmp_pe': cmp_pe[l], 'w_branch': w_branch[l], 'w_o': w_o[l], 'ln1_g': ln1_g[l], 'ln1_b': ln1_b[l],
              'w_ff_gate': w_ff_gate[l], 'w_ff_up': w_ff_up[l], 'w_ff_down': w_ff_down[l],
              'ln2_g': ln2_g[l], 'ln2_b': ln2_b[l]}
        fresh = {'cmp': jnp.zeros((B, 0, 2, G, HEAD_DIM), dt),
                 'slc': jnp.zeros((B, 0, 2, G, HEAD_DIM), dt),
                 'sb': jnp.zeros((B, 0, 2, SB_HEADS, HEAD_DIM), dt),
                 'win': jnp.zeros((B, 0, 2, G, HEAD_DIM), dt),
                 'gla': jnp.zeros((B, GLA_HEADS, GLA_DK, GLA_DV), jnp.float32)}
        yp, st = trunk_layer(yp, lw, rel_bias, fresh)
        st_p.append(st)
        past = {'cmp': gather_pages(cache_cmp_kv, l, page_table),
                'slc': gather_pages(cache_slc_kv, l, page_table),
                'sb': gather_pages(cache_sb_kv, l, page_table),
                'win': state_win_kv[l],
                'gla': state_gla[l]}
        ys, st = trunk_layer(ys, lw, rel_bias, past)
        st_s.append(st)

    def stacked(outs, i):
        return jnp.stack([o[i] for o in outs])

    new_cmp_kv_prompt = stacked(st_p, 0)
    new_cmp_kv_sample = stacked(st_s, 0)
    new_slc_kv_prompt = stacked(st_p, 1)
    new_slc_kv_sample = stacked(st_s, 1)
    new_sb_kv_prompt = stacked(st_p, 2)
    new_sb_kv_sample = stacked(st_s, 2)
    new_win_kv_prompt = stacked(st_p, 3)
    new_win_kv_sample = stacked(st_s, 3)
    new_gla_prompt = stacked(st_p, 4)
    new_gla_sample = stacked(st_s, 4)
    return (yp, ys, new_cmp_kv_prompt, new_cmp_kv_sample, new_slc_kv_prompt, new_slc_kv_sample,
            new_sb_kv_prompt, new_sb_kv_sample, new_win_kv_prompt, new_win_kv_sample,
            new_gla_prompt, new_gla_sample)
```

```python
import functools
import math

import jax
import jax.numpy as jnp
import numpy as np
from jax import lax
from jax.experimental import pallas as pl
from jax.experimental.pallas import tpu as pltpu

F32, BF16 = jnp.float32, jnp.bfloat16
HI = lax.Precision.HIGHEST

D_MODEL = 1024
HEAD_DIM = 64
MIX_W = D_MODEL // 2
GLA_HEADS = 4
GLA_DV = MIX_W // GLA_HEADS
GLA_DK = GLA_DV // 2
GLA_GATE_RANK = 16
GLA_TAU = 16.0
GLA_CHUNK = 64
GLA_SUB = 16
NSA_HEADS = MIX_W // HEAD_DIM
NSA_KV_HEADS = 2
NSA_KV_W = NSA_KV_HEADS * HEAD_DIM
CMP_STRIDE = 16
CMP_BLOCK = 32
CMP_HIDDEN = 128
SEL_BLOCK = 64
SEL_TOPN = 16
WINDOW = 512
FORCE_BONUS = 1e4
SB_HEADS = MIX_W // HEAD_DIM
REL_BUCKETS = 32
REL_MAX_DIST = 128
D_FF = -(-8 * D_MODEL // (3 * 256)) * 256
Q_BLK = 128
LN_EPS = 1e-5
TRUNK_DEPTH = 4
DN_ALPHA = (2 * TRUNK_DEPTH) ** 0.25
PAGE_SIZE = 128
GLA_QK_W = GLA_HEADS * GLA_DK
IN_WIDTHS = (GLA_QK_W, GLA_QK_W, MIX_W, MIX_W, GLA_GATE_RANK,
             MIX_W, NSA_KV_W, NSA_KV_W, NSA_KV_W, NSA_KV_W, NSA_KV_W, NSA_KV_W, 3 * NSA_HEADS,
             MIX_W, MIX_W, MIX_W,
             3 * D_MODEL)

LANES = 128
SUBLANES = 8
KEY_TILE = 128
VMEM_LIMIT = 56 * 1024 * 1024
NEG = -1e30
SEL_LANES = 64

COL = dict(mg=(0, 3072), sbkv=(3072, 1024), gv=(4096, 512), gr=(4608, 512), nq=(5120, 512),
           sbq=(5632, 512), gqk=(6144, 512), kvc=(6656, 256), kvs=(6912, 256), kvw=(7168, 256),
           ga=(7424, 128), ng=(7552, 128))
ZW = 7680


def _cb(name, width=None):
    off, w = COL[name]
    width = width or w
    assert off % width == 0
    return off // width


def _cparams(*sem):
    return pltpu.CompilerParams(dimension_semantics=sem, vmem_limit_bytes=VMEM_LIMIT)


def _dot(a, b):
    return jnp.dot(a, b, preferred_element_type=F32)


def _dot_hi(a, b):
    return jnp.dot(a, b, precision=HI, preferred_element_type=F32)


def _dot_nt(a, b):
    return lax.dot_general(a, b, (((1,), (1,)), ((), ())), preferred_element_type=F32)


def _dot_tn(a, b):
    return lax.dot_general(a, b, (((0,), (0,)), ((), ())), preferred_element_type=F32)


def _log_sigmoid(x):
    return jnp.minimum(x, 0.0) - jnp.log(1.0 + jnp.exp(-jnp.abs(x)))


def _iota(shape, dim):
    return lax.broadcasted_iota(jnp.int32, shape, dim)


def _pick_tile(n, cands=(1024, 512, 256, 128, 64, 32, 16, 8)):
    for c in cands:
        if n % c == 0:
            return c
    raise ValueError(f"no row tile divides {n}")


def _linear_kernel(x_ref, w_ref, b_ref, o_ref):
    o_ref[...] = _dot(x_ref[...].astype(BF16), w_ref[...]) + b_ref[...]


def _linear(x, w, b):
    n, k = x.shape
    dout = w.shape[1]
    tm, tn = _pick_tile(n), 512
    return pl.pallas_call(
        _linear_kernel, grid=(n // tm, dout // tn),
        in_specs=[pl.BlockSpec((tm, k), lambda i, j: (i, 0)),
                  pl.BlockSpec((k, tn), lambda i, j: (0, j)),
                  pl.BlockSpec((1, tn), lambda i, j: (0, j))],
        out_specs=pl.BlockSpec((tm, tn), lambda i, j: (i, j)),
        out_shape=jax.ShapeDtypeStruct((n, dout), F32),
        compiler_params=_cparams("parallel", "parallel"), name="in_proj")(x, w, b)


def _layer_norm(h, g, b):
    mu = jnp.mean(h, -1, keepdims=True)
    d = h - mu
    var = jnp.mean(d * d, -1, keepdims=True)
    return d * lax.rsqrt(var + LN_EPS) * g + b


def _merge_kernel(og_ref, oc_ref, os_ref, ow_ref, ob_ref, mg_ref, x_ref, wb_ref, wo_ref, g_ref, b_ref, o_ref):
    o_nsa = oc_ref[...] + os_ref[...] + ow_ref[...]
    mixed = None
    for br, o in enumerate((og_ref[...], o_nsa, ob_ref[...])):
        proj = _dot(o.astype(BF16), wb_ref[br])
        term = jax.nn.sigmoid(mg_ref[:, br * D_MODEL:(br + 1) * D_MODEL]) * proj
        mixed = term if mixed is None else mixed + term
    y = _dot(mixed.astype(BF16), wo_ref[...])
    o_ref[...] = _layer_norm(DN_ALPHA * x_ref[...] + y, g_ref[...], b_ref[...])


def _merge(o_gla, o_cmp, o_slc, o_win, o_sb, z, x, wb, wo, g, b):
    n = x.shape[0]
    tm = _pick_tile(n, (512, 256, 128, 64, 32, 16, 8))
    mix = pl.BlockSpec((tm, MIX_W), lambda i: (i, 0))
    return pl.pallas_call(
        _merge_kernel, grid=(n // tm,),
        in_specs=[mix, mix, mix, mix, mix,
                  pl.BlockSpec((tm, 3 * D_MODEL), lambda i: (i, _cb("mg"))),
                  pl.BlockSpec((tm, D_MODEL), lambda i: (i, 0)),
                  pl.BlockSpec((3, MIX_W, D_MODEL), lambda i: (0, 0, 0)),
                  pl.BlockSpec((D_MODEL, D_MODEL), lambda i: (0, 0)),
                  pl.BlockSpec((1, D_MODEL), lambda i: (0, 0)),
                  pl.BlockSpec((1, D_MODEL), lambda i: (0, 0))],
        out_specs=pl.BlockSpec((tm, D_MODEL), lambda i: (i, 0)),
        out_shape=jax.ShapeDtypeStruct((n, D_MODEL), F32),
        compiler_params=_cparams("parallel"), name="merge")(o_gla, o_cmp, o_slc, o_win, o_sb, z, x, wb, wo, g, b)


FF_TILE = 256


def _ffn_kernel(x_ref, wg_ref, wu_ref, wd_ref, g_ref, b_ref, o_ref, xb_ref, acc_ref):
    f = pl.program_id(1)

    @pl.when(f == 0)
    def _():
        xb_ref[...] = x_ref[...].astype(BF16)
        acc_ref[...] = jnp.zeros_like(acc_ref)

    xb = xb_ref[...]
    gate = _dot(xb, wg_ref[...])
    up = _dot(xb, wu_ref[...])
    h = gate * jax.nn.sigmoid(gate) * up
    acc_ref[...] += _dot(h.astype(BF16), wd_ref[...])

    @pl.when(f == pl.num_programs(1) - 1)
    def _():
        o_ref[...] = _layer_norm(DN_ALPHA * x_ref[...] + acc_ref[...], g_ref[...], b_ref[...])


def _ffn(x, wg, wu, wd, g, b):
    n = x.shape[0]
    tm = _pick_tile(n)
    return pl.pallas_call(
        _ffn_kernel, grid=(n // tm, D_FF // FF_TILE),
        in_specs=[pl.BlockSpec((tm, D_MODEL), lambda i, f: (i, 0)),
                  pl.BlockSpec((D_MODEL, FF_TILE), lambda i, f: (0, f)),
                  pl.BlockSpec((D_MODEL, FF_TILE), lambda i, f: (0, f)),
                  pl.BlockSpec((FF_TILE, D_MODEL), lambda i, f: (f, 0)),
                  pl.BlockSpec((1, D_MODEL), lambda i, f: (0, 0)),
                  pl.BlockSpec((1, D_MODEL), lambda i, f: (0, 0))],
        out_specs=pl.BlockSpec((tm, D_MODEL), lambda i, f: (i, 0)),
        out_shape=jax.ShapeDtypeStruct((n, D_MODEL), F32),
        scratch_shapes=[pltpu.VMEM((tm, D_MODEL), BF16), pltpu.VMEM((tm, D_MODEL), F32)],
        compiler_params=_cparams("parallel", "arbitrary"), name="ffn")(x, wg, wu, wd, g, b)


def _gla_kernel(qk_ref, v_ref, r_ref, ga_ref, s0_ref, wa_ref, ba_ref, ng_ref, o_ref, st_ref, *, chunk, n_chunks):
    C = chunk

    @pl.when(pl.program_id(1) == 0)
    def _():
        st_ref[...] = s0_ref[...]

    sub = min(C, GLA_SUB)
    n_sub = C // sub
    tril = (_iota((C, C), 1) <= _iota((C, C), 0)).astype(F32)
    row_sub = _iota((sub, GLA_DK), 0)

    def chunk_body(c, carry):
        r0 = pl.multiple_of(c * C, C)
        rows = pl.ds(r0, C)
        qk = qk_ref[rows, :]
        v = v_ref[rows, :]
        log_a = _log_sigmoid(_dot_hi(ga_ref[rows, :], wa_ref[...]) + ba_ref[...]) * (1.0 / GLA_TAU)
        cum = _dot_hi(tril, log_a)
        for h in range(GLA_HEADS):
            qh = qk[:, h * GLA_DK:(h + 1) * GLA_DK] * (GLA_DK ** -0.5)
            kh = qk[:, GLA_QK_W + h * GLA_DK:GLA_QK_W + (h + 1) * GLA_DK]
            vh = v[:, h * GLA_DV:(h + 1) * GLA_DV]
            ch = cum[:, h * GLA_DK:(h + 1) * GLA_DK]
            st = st_ref[h]
            o_inter = _dot_nt((qh * jnp.exp(ch)).astype(BF16), st.astype(BF16))
            outs = []
            for sb in range(n_sub):
                a0 = sb * sub
                qs, ks, vs, cs = qh[a0:a0 + sub], kh[a0:a0 + sub], vh[a0:a0 + sub], ch[a0:a0 + sub]
                oi = o_inter[a0:a0 + sub]
                if sb > 0:
                    cref = ch[a0 - 1:a0]
                    qd = qs * jnp.exp(cs - cref)
                    kd = kh[:a0] * jnp.exp(cref - ch[:a0])
                    att = _dot_nt(qd.astype(BF16), kd.astype(BF16))
                    oi = oi + _dot(att.astype(BF16), vh[:a0].astype(BF16))
                for s in range(sub):
                    e = jnp.exp(jnp.minimum(cs - cs[s:s + 1], 0.0))
                    w = jnp.sum(jnp.where(row_sub >= s, qs * ks[s:s + 1] * e, 0.0), axis=-1, keepdims=True)
                    oi = oi + w * vs[s:s + 1]
                outs.append(oi)
            o = outs[0] if n_sub == 1 else jnp.concatenate(outs, axis=0)
            mu = jnp.mean(o, -1, keepdims=True)
            d = o - mu
            var = jnp.mean(d * d, -1, keepdims=True)
            lanes = slice(h * GLA_DV, (h + 1) * GLA_DV)
            rr = r_ref[rows, lanes]
            o_ref[rows, lanes] = d * lax.rsqrt(var + LN_EPS) * ng_ref[:, lanes] * (rr * jax.nn.sigmoid(rr))
            last = ch[C - 1:C]
            kdl = kh * jnp.exp(last - ch)
            st_ref[h] = st * jnp.exp(last) + _dot_tn(vh.astype(BF16), kdl.astype(BF16))
        return carry

    lax.fori_loop(0, n_chunks, chunk_body, 0)


def _gla(z3, s0t, wa, ba, norm_g):
    B, T, _ = z3.shape
    C = math.gcd(T, GLA_CHUNK)
    tt = min(T, 512)
    assert T % tt == 0 and tt % C == 0
    kern = functools.partial(_gla_kernel, chunk=C, n_chunks=tt // C)
    st_spec = pl.BlockSpec((None, GLA_HEADS, GLA_DV, GLA_DK), lambda b, i: (b, 0, 0, 0))
    return pl.pallas_call(
        kern, grid=(B, T // tt),
        in_specs=[pl.BlockSpec((None, tt, 512), lambda b, i: (b, i, _cb("gqk"))),
                  pl.BlockSpec((None, tt, 512), lambda b, i: (b, i, _cb("gv"))),
                  pl.BlockSpec((None, tt, 512), lambda b, i: (b, i, _cb("gr"))),
                  pl.BlockSpec((None, tt, LANES), lambda b, i: (b, i, _cb("ga"))),
                  st_spec,
                  pl.BlockSpec((LANES, GLA_QK_W), lambda b, i: (0, 0)),
                  pl.BlockSpec((1, GLA_QK_W), lambda b, i: (0, 0)),
                  pl.BlockSpec((1, MIX_W), lambda b, i: (0, 0))],
        out_specs=[pl.BlockSpec((None, tt, MIX_W), lambda b, i: (b, i, 0)), st_spec],
        out_shape=[jax.ShapeDtypeStruct((B, T, MIX_W), F32),
                   jax.ShapeDtypeStruct((B, GLA_HEADS, GLA_DV, GLA_DK), F32)],
        compiler_params=_cparams("parallel", "arbitrary"), name="gla")(z3, z3, z3, z3, s0t, wa, ba, norm_g)


def _gather_kernel(pt_ref, page_ref, new_ref, o_ref, *, n_pages, n_new):
    p = pl.program_id(1)

    @pl.when(p < n_pages)
    def _():
        o_ref[...] = page_ref[...]

    @pl.when(p == n_pages)
    def _():
        o_ref[0:n_new, :] = new_ref[...]
        o_ref[n_new:, :] = jnp.zeros((PAGE_SIZE - n_new, o_ref.shape[1]), F32)


def _gather_pages(pool, layer, pt_flat, z3, colname, n_pages, with_new):
    depth, n_phys = pool.shape[:2]
    width = int(np.prod(pool.shape[3:]))
    pool4 = pool.reshape(depth, n_phys, PAGE_SIZE, width)
    B, n_new, _ = z3.shape
    n_tiles = n_pages + (1 if with_new else 0)
    kern = functools.partial(_gather_kernel, n_pages=n_pages, n_new=n_new)
    grid_spec = pltpu.PrefetchScalarGridSpec(
        num_scalar_prefetch=1, grid=(B, n_tiles),
        in_specs=[pl.BlockSpec((None, None, PAGE_SIZE, width),
                               lambda b, p, pt: (layer, pt[b * n_pages + jnp.minimum(p, n_pages - 1)], 0, 0)),
                  pl.BlockSpec((None, n_new, width), lambda b, p, pt: (b, 0, _cb(colname, width)))],
        out_specs=pl.BlockSpec((None, PAGE_SIZE, width), lambda b, p, pt: (b, p, 0)))
    return pl.pallas_call(
        kern, grid_spec=grid_spec,
        out_shape=jax.ShapeDtypeStruct((B, n_tiles * PAGE_SIZE, width), F32),
        compiler_params=_cparams("parallel", "arbitrary"), name="gather_" + colname)(pt_flat, pool4, z3)


def _compress_kernel(rows_ref, w1_ref, pe_ref, b1_ref, w2_ref, o_ref, h1_ref, *, n_seg):
    M = n_seg
    h0 = jnp.zeros((M, 2 * CMP_HIDDEN), F32)
    h1 = jnp.zeros((M, 2 * CMP_HIDDEN), F32)
    for j in range(CMP_STRIDE):
        x = rows_ref[pl.ds(j, M, stride=CMP_STRIDE), :]
        j1 = CMP_STRIDE + j
        h0 = h0 + _dot((x + pe_ref[j:j + 1, :]).astype(BF16), w1_ref[j])
        h1 = h1 + _dot((x + pe_ref[j1:j1 + 1, :]).astype(BF16), w1_ref[j1])
    h1_ref[0:M, :] = h1
    h1_ref[M:M + SUBLANES, :] = jnp.zeros((SUBLANES, 2 * CMP_HIDDEN), F32)
    h = jax.nn.gelu(b1_ref[...] + h0 + h1_ref[pl.ds(1, M), :])
    o_ref[...] = _dot(h.astype(BF16), w2_ref[...])


def _compress(rows3, colblk, n_rows, w1bd, pe2, b1t, w2bd):
    B = rows3.shape[0]
    n_seg = n_rows // CMP_STRIDE
    kern = functools.partial(_compress_kernel, n_seg=n_seg)
    return pl.pallas_call(
        kern, grid=(B, 2),
        in_specs=[pl.BlockSpec((None, n_rows, NSA_KV_W), lambda b, c: (b, 0, 2 * colblk + c)),
                  pl.BlockSpec((CMP_BLOCK, None, NSA_KV_W, 2 * CMP_HIDDEN), lambda b, c: (0, c, 0, 0)),
                  pl.BlockSpec((None, CMP_BLOCK, NSA_KV_W), lambda b, c: (c, 0, 0)),
                  pl.BlockSpec((None, 1, 2 * CMP_HIDDEN), lambda b, c: (c, 0, 0)),
                  pl.BlockSpec((None, 2 * CMP_HIDDEN, NSA_KV_W), lambda b, c: (c, 0, 0))],
        out_specs=pl.BlockSpec((None, n_seg, NSA_KV_W), lambda b, c: (b, 0, c)),
        out_shape=jax.ShapeDtypeStruct((B, n_seg, 2 * NSA_KV_W), F32),
        scratch_shapes=[pltpu.VMEM((n_seg + SUBLANES, 2 * CMP_HIDDEN), F32)],
        compiler_params=_cparams("parallel", "parallel"), name="compress")(rows3, w1bd, pe2, b1t, w2bd)


def _build_qbd(q_ref, qbd_ref, tq):
    lane = _iota((tq, LANES), 1)
    for p in range(NSA_HEADS):
        slab = q_ref[:, (p // 2) * LANES:(p // 2 + 1) * LANES] * (HEAD_DIM ** -0.5)
        keep = (lane >= HEAD_DIM) if p % 2 else (lane < HEAD_DIM)
        qbd_ref[p * tq:(p + 1) * tq, :] = jnp.where(keep, slab, 0.0)


def _assemble_heads(acc_ref, ng_ref, o_ref, tq, branch):
    lo = _iota((tq, LANES), 1) < HEAD_DIM
    for k in range(NSA_HEADS // 2):
        a0 = acc_ref[(2 * k) * tq:(2 * k + 1) * tq, :]
        a1 = acc_ref[(2 * k + 1) * tq:(2 * k + 2) * tq, :]
        c0 = branch * NSA_HEADS + 2 * k
        g0 = jax.nn.sigmoid(ng_ref[:, c0:c0 + 1])
        g1 = jax.nn.sigmoid(ng_ref[:, c0 + 1:c0 + 2])
        o_ref[:, k * LANES:(k + 1) * LANES] = jnp.where(lo, a0 * g0, a1 * g1)


def _masked_softmax(s, valid):
    s = jnp.where(valid, s, NEG)
    m = jnp.max(s, -1, keepdims=True)
    e = jnp.where(valid, jnp.exp(s - m), 0.0)
    return e / jnp.maximum(jnp.sum(e, -1, keepdims=True), 1e-30)


def _cmp_kernel(q_ref, kvc_ref, bias_ref, cover_ref, ng_ref, o_ref, sel_ref, qbd_ref, acc_ref, *, tq, rc, pos0):
    qi = pl.program_id(1)
    _build_qbd(q_ref, qbd_ref, tq)
    n = kvc_ref.shape[0]
    kc = kvc_ref[:, 0:NSA_KV_W].astype(BF16)
    vc = kvc_ref[:, NSA_KV_W:2 * NSA_KV_W].astype(BF16)
    heads_per_chunk = rc // tq
    p_sum = [jnp.zeros((tq, n), F32), jnp.zeros((tq, n), F32)]
    for c in range(NSA_HEADS * tq // rc):
        rows = slice(c * rc, (c + 1) * rc)
        bias = bias_ref[rows, :]
        s = _dot_nt(qbd_ref[rows, :].astype(BF16), kc) + bias
        pr = _masked_softmax(s, bias > 0.1 * NEG)
        acc_ref[rows, :] = _dot(pr.astype(BF16), vc)
        for pp in range(heads_per_chunk):
            g = (c * heads_per_chunk + pp) % 2
            p_sum[g] = p_sum[g] + pr[pp * tq:(pp + 1) * tq]
    _assemble_heads(acc_ref, ng_ref, o_ref, tq, 0)

    score = _dot_hi(p_sum[0], cover_ref[0]) + _dot_hi(p_sum[1], cover_ref[1])
    lane = _iota((tq, LANES), 1)
    blk = jnp.bitwise_and(lane, SEL_LANES - 1)
    pos = pos0 + qi * tq + _iota((tq, LANES), 0)
    cur = lax.shift_right_logical(pos, int(math.log2(SEL_BLOCK)))
    forced = (blk == 0) | (blk == cur) | (blk == cur - 1)
    score = jnp.where(forced, score + FORCE_BONUS, score)
    visible = blk <= cur
    score = jnp.where(visible, score, -jnp.inf)
    lo = lane < SEL_LANES
    rank = jnp.zeros((tq, LANES), F32)
    for mp in range(SEL_LANES):
        other = jnp.where(lo, score[:, mp:mp + 1], score[:, SEL_LANES + mp:SEL_LANES + mp + 1])
        beats = (other > score) | ((other == score) & (blk > mp))
        rank = rank + jnp.where(beats, 1.0, 0.0)
    sel_ref[...] = jnp.where((rank < SEL_TOPN) & visible, 1.0, 0.0)


def _nsa_common_specs(tq):
    q_spec = pl.BlockSpec((None, tq, MIX_W), lambda b, i: (b, i, _cb("nq")))
    ng_spec = pl.BlockSpec((None, tq, LANES), lambda b, i: (b, i, _cb("ng")))
    o_spec = pl.BlockSpec((None, tq, MIX_W), lambda b, i: (b, i, 0))
    return q_spec, ng_spec, o_spec


def _nsa_scratch(tq):
    return [pltpu.VMEM((NSA_HEADS * tq, LANES), F32), pltpu.VMEM((NSA_HEADS * tq, LANES), F32)]


def _cmp_attend(z3, kvc, bias_tab, cover2, tq, pos0):
    B, T, _ = z3.shape
    n = kvc.shape[1]
    R = NSA_HEADS * tq
    rc = min(R, 128)
    q_spec, ng_spec, o_spec = _nsa_common_specs(tq)
    kern = functools.partial(_cmp_kernel, tq=tq, rc=rc, pos0=pos0)
    return pl.pallas_call(
        kern, grid=(B, T // tq),
        in_specs=[q_spec,
                  pl.BlockSpec((None, n, 2 * NSA_KV_W), lambda b, i: (b, 0, 0)),
                  pl.BlockSpec((None, R, n), lambda b, i: (i, 0, 0)),
                  pl.BlockSpec((2, n, LANES), lambda b, i: (0, 0, 0)),
                  ng_spec],
        out_specs=[o_spec, pl.BlockSpec((None, tq, LANES), lambda b, i: (b, i, 0))],
        out_shape=[jax.ShapeDtypeStruct((B, T, MIX_W), F32), jax.ShapeDtypeStruct((B, T, LANES), F32)],
        scratch_shapes=_nsa_scratch(tq),
        compiler_params=_cparams("parallel", "parallel"), name="nsa_cmp")(z3, kvc, bias_tab, cover2, z3)


def _sel_kernel(q_ref, sel_ref, kv_ref, near_ref, far_ref, ng_ref, o_ref, qbd_ref, acc_ref, *, tq, rc, tile0):
    qi = pl.program_id(1)
    qt = tile0 + (qi * tq) // KEY_TILE
    _build_qbd(q_ref, qbd_ref, tq)
    sel_b = sel_ref[...].astype(BF16)
    slot = _iota((LANES, KEY_TILE), 0)
    col_blk = lax.shift_right_logical(_iota((LANES, KEY_TILE), 1), int(math.log2(SEL_BLOCK)))
    heads_per_chunk = rc // tq

    def expand_sel(kt):
        out = []
        for g in range(NSA_KV_HEADS):
            e = jnp.where(slot == 2 * kt + col_blk + g * SEL_LANES, 1.0, 0.0).astype(BF16)
            out.append(_dot(sel_b, e))
        return out

    for c in range(NSA_HEADS * tq // rc):
        rows = slice(c * rc, (c + 1) * rc)
        qc = qbd_ref[rows, :].astype(BF16)
        groups = [(c * heads_per_chunk + pp) % 2 for pp in range(heads_per_chunk)]

        def update(carry, kt, bias, groups=groups, qc=qc):
            m, l, acc = carry
            k0 = pl.multiple_of(kt * KEY_TILE, KEY_TILE)
            kt_rows = pl.ds(k0, KEY_TILE)
            kk = kv_ref[kt_rows, 0:NSA_KV_W].astype(BF16)
            vv = kv_ref[kt_rows, NSA_KV_W:2 * NSA_KV_W].astype(BF16)
            se = expand_sel(kt)
            pieces = [se[g] for g in groups]
            selm = pieces[0] if len(pieces) == 1 else jnp.concatenate(pieces, axis=0)
            s = _dot_nt(qc, kk) + bias
            valid = (selm > 0.5) & (bias > 0.1 * NEG)
            s = jnp.where(valid, s, NEG)
            m_new = jnp.maximum(m, jnp.max(s, -1, keepdims=True))
            alpha = jnp.exp(m - m_new)
            pe = jnp.where(valid, jnp.exp(s - m_new), 0.0)
            l = alpha * l + jnp.sum(pe, -1, keepdims=True)
            acc = alpha * acc + _dot(pe.astype(BF16), vv)
            return m_new, l, acc

        far = far_ref[rows, :]
        init = (jnp.full((rc, 1), NEG, F32), jnp.zeros((rc, 1), F32), jnp.zeros((rc, LANES), F32))
        carry = lax.fori_loop(0, jnp.maximum(qt - 1, 0), lambda kt, cr: update(cr, kt, far), init)
        prev_ok = (qt >= 1).astype(F32)
        carry = update(carry, jnp.maximum(qt - 1, 0), near_ref[rows, 0:KEY_TILE] + (prev_ok - 1.0) * (-NEG))
        m, l, acc = update(carry, qt, near_ref[rows, KEY_TILE:2 * KEY_TILE])
        acc_ref[rows, :] = acc / jnp.maximum(l, 1e-30)
    _assemble_heads(acc_ref, ng_ref, o_ref, tq, 1)


def _sel_attend(z3, sel, keys3, key_colblk, n_keys, near_tab, far_tab, tq, pos0):
    B, T, _ = z3.shape
    R = NSA_HEADS * tq
    rc = min(R, 128)
    assert pos0 % KEY_TILE == 0 and (tq == KEY_TILE or T == tq)
    q_spec, ng_spec, o_spec = _nsa_common_specs(tq)
    kern = functools.partial(_sel_kernel, tq=tq, rc=rc, tile0=pos0 // KEY_TILE)
    return pl.pallas_call(
        kern, grid=(B, T // tq),
        in_specs=[q_spec,
                  pl.BlockSpec((None, tq, LANES), lambda b, i: (b, i, 0)),
                  pl.BlockSpec((None, n_keys, 2 * NSA_KV_W), lambda b, i: (b, 0, key_colblk)),
                  pl.BlockSpec((R, 2 * KEY_TILE), lambda b, i: (0, 0)),
                  pl.BlockSpec((R, KEY_TILE), lambda b, i: (0, 0)),
                  ng_spec],
        out_specs=o_spec,
        out_shape=jax.ShapeDtypeStruct((B, T, MIX_W), F32),
        scratch_shapes=_nsa_scratch(tq),
        compiler_params=_cparams("parallel", "parallel"), name="nsa_slc")(z3, sel, keys3, near_tab, far_tab, z3)


WIN_KEYS = WINDOW + Q_BLK


def _win_kernel(q_ref, kv_ref, bias_ref, ng_ref, o_ref, qbd_ref, acc_ref, *, tq, rc, valid_from):
    qi = pl.program_id(1)
    _build_qbd(q_ref, qbd_ref, tq)
    k0 = pl.multiple_of(qi * tq, tq)
    kw = kv_ref[pl.ds(k0, WIN_KEYS), 0:NSA_KV_W].astype(BF16)
    vw = kv_ref[pl.ds(k0, WIN_KEYS), NSA_KV_W:2 * NSA_KV_W].astype(BF16)
    real = (k0 + _iota((rc, WIN_KEYS), 1)) >= valid_from
    for c in range(NSA_HEADS * tq // rc):
        rows = slice(c * rc, (c + 1) * rc)
        bias = bias_ref[rows, :]
        s = _dot_nt(qbd_ref[rows, :].astype(BF16), kw) + bias
        pr = _masked_softmax(s, (bias > 0.1 * NEG) & real)
        acc_ref[rows, :] = _dot(pr.astype(BF16), vw)
    _assemble_heads(acc_ref, ng_ref, o_ref, tq, 2)


def _win_attend(z3, keys3, win_tab, tq, valid_from):
    B, T, _ = z3.shape
    n_keys = keys3.shape[1]
    R = NSA_HEADS * tq
    rc = min(R, 128)
    q_spec, ng_spec, o_spec = _nsa_common_specs(tq)
    kern = functools.partial(_win_kernel, tq=tq, rc=rc, valid_from=valid_from)
    return pl.pallas_call(
        kern, grid=(B, T // tq),
        in_specs=[q_spec,
                  pl.BlockSpec((None, n_keys, 2 * NSA_KV_W), lambda b, i: (b, 0, 0)),
                  pl.BlockSpec((R, WIN_KEYS), lambda b, i: (0, 0)),
                  ng_spec],
        out_specs=o_spec,
        out_shape=jax.ShapeDtypeStruct((B, T, MIX_W), F32),
        scratch_shapes=_nsa_scratch(tq),
        compiler_params=_cparams("parallel", "parallel"), name="nsa_win")(z3, keys3, win_tab, z3)


def _sb_kernel(q_ref, k_ref, v_ref, o_ref, *, tq, heads, tile0):
    qi = pl.program_id(2)
    qt = tile0 + (qi * tq) // KEY_TILE
    sw = heads * HEAD_DIM
    R = heads * tq
    lane_head = lax.shift_right_logical(_iota((tq, sw), 1), int(math.log2(HEAD_DIM)))
    q = q_ref[...] * (HEAD_DIM ** -0.5)
    qbd = jnp.concatenate([jnp.where(lane_head == j, q, 0.0) for j in range(heads)], axis=0).astype(BF16)
    after = (_iota((KEY_TILE, KEY_TILE), 0) > _iota((KEY_TILE, KEY_TILE), 1)).astype(BF16)
    causal = _iota((R, KEY_TILE), 1) < jnp.bitwise_and(_iota((R, KEY_TILE), 0), tq - 1)

    def tile(kt, run, acc, masked):
        rows = pl.ds(pl.multiple_of(kt * KEY_TILE, KEY_TILE), KEY_TILE)
        z = _dot_nt(qbd, k_ref[rows, :].astype(BF16))
        u = _log_sigmoid(-z)
        if masked:
            u = jnp.where(causal, u, 0.0)
        u_hi = u.astype(BF16)
        u_lo = (u - u_hi.astype(F32)).astype(BF16)
        rest = _dot(u_hi, after) + _dot(u_lo, after)
        a = jnp.exp(u + z + rest + run)
        if masked:
            a = jnp.where(causal, a, 0.0)
        acc = acc + _dot(a.astype(BF16), v_ref[rows, :].astype(BF16))
        return run + jnp.sum(u, -1, keepdims=True), acc

    run, acc = tile(qt, jnp.zeros((R, 1), F32), jnp.zeros((R, sw), F32), True)
    run, acc = lax.fori_loop(0, qt, lambda i, cr: tile(qt - 1 - i, cr[0], cr[1], False), (run, acc))
    out = None
    for j in range(heads):
        part = jnp.where(lane_head == j, acc[j * tq:(j + 1) * tq], 0.0)
        out = part if out is None else out + part
    o_ref[...] = out


def _sb_attend(z3, keys3, k_colblk0, n_keys, tq, heads, pos0):
    B, T, _ = z3.shape
    sw = heads * HEAD_DIM
    n_slabs = SB_HEADS // heads
    assert pos0 % KEY_TILE == 0 and (tq == KEY_TILE or T == tq) and tq & (tq - 1) == 0
    kern = functools.partial(_sb_kernel, tq=tq, heads=heads, tile0=pos0 // KEY_TILE)
    qcb = _cb("sbq", sw)
    return pl.pallas_call(
        kern, grid=(B, n_slabs, T // tq),
        in_specs=[pl.BlockSpec((None, tq, sw), lambda b, s, i: (b, i, qcb + s)),
                  pl.BlockSpec((None, n_keys, sw), lambda b, s, i: (b, 0, k_colblk0 + s)),
                  pl.BlockSpec((None, n_keys, sw), lambda b, s, i: (b, 0, k_colblk0 + n_slabs + s))],
        out_specs=pl.BlockSpec((None, tq, sw), lambda b, s, i: (b, i, s)),
        out_shape=jax.ShapeDtypeStruct((B, T, MIX_W), F32),
        compiler_params=_cparams("parallel", "parallel", "parallel"), name="sb")(z3, keys3, keys3)


def _prep_in_proj(w):
    offs = np.concatenate([[0], np.cumsum(IN_WIDTHS)])
    lead = w.shape[:-1]

    def seg(i):
        return w[..., int(offs[i]):int(offs[i + 1])]

    def pad(x, width):
        return jnp.pad(x, [(0, 0)] * (x.ndim - 1) + [(0, width - x.shape[-1])])

    nq = seg(5).reshape(*lead, 2, 4, HEAD_DIM).swapaxes(-3, -2).reshape(*lead, MIX_W)
    ng = seg(12).reshape(*lead, 3, 2, 4).swapaxes(-2, -1).reshape(*lead, 3 * NSA_HEADS)
    out = jnp.concatenate([seg(16), seg(14), seg(15), seg(2), seg(3), nq, seg(13), seg(0), seg(1),
                           seg(6), seg(7), seg(8), seg(9), seg(10), seg(11), pad(seg(4), LANES), pad(ng, LANES)], -1)
    assert out.shape[-1] == ZW
    return out


def _t5_bucket(dist):
    n = jnp.maximum(dist, 0)
    exact = REL_BUCKETS // 2
    scaled = jnp.log(jnp.maximum(n, 1).astype(F32) / exact) / math.log(REL_MAX_DIST / exact)
    large = jnp.minimum(exact + (scaled * (REL_BUCKETS - exact)).astype(jnp.int32), REL_BUCKETS - 1)
    return jnp.where(n < exact, n, large)


def _bias_rows(rel_slots, dist, valid, tq):
    T, S = dist.shape
    tab = jnp.where(valid[:, :, None], rel_slots[_t5_bucket(dist)], NEG)
    return tab.reshape(T // tq, tq, S, NSA_HEADS).transpose(0, 3, 1, 2).reshape(T // tq, NSA_HEADS * tq, S)


def _group_tables(rel_slots, T, tq, pos0, n_cmp_pad):
    pos_q = pos0 + np.arange(T)
    end = np.arange(n_cmp_pad) * CMP_STRIDE + CMP_BLOCK - 1
    d_cmp = jnp.asarray(pos_q[:, None] - end[None, :], jnp.int32)
    cmp_tab = _bias_rows(rel_slots, d_cmp, d_cmp >= 0, tq)
    t = np.arange(tq)
    d_near = jnp.asarray(KEY_TILE + t[:, None] - np.arange(2 * KEY_TILE)[None, :], jnp.int32)
    near_tab = _bias_rows(rel_slots, d_near, d_near >= 0, tq)[0]
    far_tab = jnp.broadcast_to(jnp.repeat(rel_slots[REL_BUCKETS - 1], tq)[:, None], (NSA_HEADS * tq, KEY_TILE))
    d_win = jnp.asarray(WINDOW + t[:, None] - np.arange(WIN_KEYS)[None, :], jnp.int32)
    win_tab = _bias_rows(rel_slots, d_win, (d_win >= 0) & (d_win <= WINDOW), tq)[0]
    return cmp_tab, near_tab, far_tab + 0.0, win_tab


def _cover(n_cmp_pad):
    start = np.arange(n_cmp_pad) * CMP_STRIDE
    blk = np.arange(SEL_LANES) * SEL_BLOCK
    c = ((start[:, None] < blk[None, :] + SEL_BLOCK) & (start[:, None] + CMP_BLOCK > blk[None, :])).astype(np.float32)
    out = np.zeros((2, n_cmp_pad, LANES), np.float32)
    out[0, :, :SEL_LANES] = c
    out[1, :, SEL_LANES:] = c
    return jnp.asarray(out)


def kernel(x_prompt, x_sample, cache_cmp_kv, cache_slc_kv, cache_sb_kv, state_win_kv, state_gla, page_table,
           rel_bias, w_in, b_in, w_gla_a2, b_gla_a2, gla_norm_g, w_cmp1, b_cmp1, w_cmp2, cmp_pe,
           w_branch, w_o, ln1_g, ln1_b, w_ff_gate, w_ff_up, w_ff_down, ln2_g, ln2_b):
    depth = w_in.shape[0]
    B, T, _ = x_prompt.shape
    DB, DT, _ = x_sample.shape
    n_pages = page_table.shape[1]
    past = n_pages * PAGE_SIZE
    n_win_state = state_win_kv.shape[2]
    assert T % Q_BLK == 0 and DT % SUBLANES == 0 and DT <= Q_BLK and n_win_state == WINDOW and T >= WINDOW

    w_all = _prep_in_proj(w_in).astype(BF16)
    b_all = _prep_in_proj(b_in)[:, None, :]
    wa = jnp.pad(w_gla_a2, ((0, 0), (0, LANES - GLA_GATE_RANK), (0, 0)))
    eye2 = jnp.eye(2, dtype=F32)
    w1bd = jnp.einsum("gG,ljcdh->ljcgdGh", eye2, w_cmp1).reshape(
        depth, CMP_BLOCK, 2, NSA_KV_W, 2 * CMP_HIDDEN).astype(BF16)
    w2bd = jnp.einsum("gG,lchd->lcghGd", eye2, w_cmp2).reshape(depth, 2, 2 * CMP_HIDDEN, NSA_KV_W).astype(BF16)
    pe2 = jnp.broadcast_to(cmp_pe.swapaxes(1, 2)[:, :, :, None, :], (depth, 2, CMP_BLOCK, 2, HEAD_DIM)).reshape(
        depth, 2, CMP_BLOCK, NSA_KV_W)
    b1t = jnp.broadcast_to(b_cmp1[:, :, None, None, :], (depth, 2, 1, 2, CMP_HIDDEN)).reshape(
        depth, 2, 1, 2 * CMP_HIDDEN)
    wb_nsa = w_branch[:, 1].reshape(depth, 2, 4, HEAD_DIM, D_MODEL).swapaxes(1, 2).reshape(depth, MIX_W, D_MODEL)
    wb = jnp.stack([w_branch[:, 0], wb_nsa, w_branch[:, 2]], 1).astype(BF16)
    wo = w_o.astype(BF16)
    wg, wu, wd = w_ff_gate.astype(BF16), w_ff_up.astype(BF16), w_ff_down.astype(BF16)
    rel_slots = rel_bias.reshape(REL_BUCKETS, 2, 4).swapaxes(1, 2).reshape(REL_BUCKETS, NSA_HEADS)

    n_seg_p = T // CMP_STRIDE
    n_seg_s = past // CMP_STRIDE
    tabs_p = _group_tables(rel_slots, T, Q_BLK, 0, n_seg_p)
    tabs_s = _group_tables(rel_slots, DT, DT, past, n_seg_s)
    cover_p, cover_s = _cover(n_seg_p), _cover(n_seg_s)
    pt_flat = page_table.reshape(-1).astype(jnp.int32)
    n_keys_s = past + PAGE_SIZE

    xp = x_prompt.reshape(B * T, D_MODEL)
    xs = x_sample.reshape(DB * DT, D_MODEL)
    zero_state = jnp.zeros((B, GLA_HEADS, GLA_DV, GLA_DK), F32)
    outs_p, outs_s = [], []

    def mixers(z3, s0t, l, tq, pos0, tabs, cover, kvc, slc_keys, slc_cb, n_slc_keys, win_keys, win_valid_from,
               sb_keys, sb_cb, n_sb_keys, sb_heads):
        cmp_tab, near_tab, far_tab, win_tab = tabs
        o_gla, st = _gla(z3, s0t, wa[l], b_gla_a2[l][None], gla_norm_g[l][None])
        o_cmp, sel = _cmp_attend(z3, kvc, cmp_tab, cover, tq, pos0)
        o_slc = _sel_attend(z3, sel, slc_keys, slc_cb, n_slc_keys, near_tab, far_tab, tq, pos0)
        o_win = _win_attend(z3, win_keys, win_tab, tq, win_valid_from)
        o_sb = _sb_attend(z3, sb_keys, sb_cb, n_sb_keys, tq, sb_heads, pos0)
        return o_gla, st, o_cmp, o_slc, o_win, o_sb

    def dense_tail(o, z, x, l):
        n = x.shape[0]
        o_gla, o_cmp, o_slc, o_win, o_sb = (a.reshape(n, MIX_W) for a in o)
        x1 = _merge(o_gla, o_cmp, o_slc, o_win, o_sb, z, x, wb[l], wo[l], ln1_g[l][None], ln1_b[l][None])
        return _ffn(x1, wg[l], wu[l], wd[l], ln2_g[l][None], ln2_b[l][None])

    def col(z3, name):
        off, w = COL[name]
        return z3[..., off:off + w]

    for l in range(depth):
        cw = (w1bd[l], pe2[l], b1t[l], w2bd[l])

        z = _linear(xp, w_all[l], b_all[l])
        z3 = z.reshape(B, T, ZW)
        kvc = _compress(z3, _cb("kvc"), T, *cw)
        win_keys = jnp.pad(col(z3, "kvw"), ((0, 0), (WINDOW, 0), (0, 0)))
        o_gla, st, *o_rest = mixers(z3, zero_state, l, Q_BLK, 0, tabs_p, cover_p, kvc,
                                    z3, _cb("kvs"), T, win_keys, WINDOW, z3, _cb("sbkv", 2 * HEAD_DIM), T, 2)
        xp = dense_tail((o_gla, *o_rest), z, xp, l)
        outs_p.append((col(z3, "kvc"), col(z3, "kvs"), col(z3, "sbkv"), col(z3, "kvw")[:, T - WINDOW:],
                       st.swapaxes(-1, -2)))

        z = _linear(xs, w_all[l], b_all[l])
        z3 = z.reshape(DB, DT, ZW)
        cmp_rows = _gather_pages(cache_cmp_kv, l, pt_flat, z3, "kvc", n_pages, False)
        slc_keys = _gather_pages(cache_slc_kv, l, pt_flat, z3, "kvs", n_pages, True)
        sb_keys = _gather_pages(cache_sb_kv, l, pt_flat, z3, "sbkv", n_pages, True)
        kvc = _compress(cmp_rows, 0, past, *cw)
        win_state = state_win_kv[l].reshape(DB, n_win_state, 2 * NSA_KV_W)
        win_all = jnp.concatenate([win_state, col(z3, "kvw")], 1)
        win_keys = jnp.pad(win_all, ((0, 0), (0, WIN_KEYS - n_win_state - DT), (0, 0)))
        o_gla, st, *o_rest = mixers(z3, state_gla[l].swapaxes(-1, -2), l, DT, past, tabs_s, cover_s, kvc,
                                    slc_keys, 0, n_keys_s, win_keys, 0, sb_keys, 0, n_keys_s, SB_HEADS)
        xs = dense_tail((o_gla, *o_rest), z, xs, l)
        outs_s.append((col(z3, "kvc"), col(z3, "kvs"), col(z3, "sbkv"), win_all[:, win_all.shape[1] - WINDOW:],
                       st.swapaxes(-1, -2)))

    def stacked(outs, i, tail):
        a = jnp.stack([o[i] for o in outs])
        return a.reshape(*a.shape[:3], *tail)

    kv_tail = (2, NSA_KV_HEADS, HEAD_DIM)
    sb_tail = (2, SB_HEADS, HEAD_DIM)
    return (xp.reshape(B, T, D_MODEL), xs.reshape(DB, DT, D_MODEL),
            stacked(outs_p, 0, kv_tail), stacked(outs_s, 0, kv_tail),
            stacked(outs_p, 1, kv_tail), stacked(outs_s, 1, kv_tail),
            stacked(outs_p, 2, sb_tail), stacked(outs_s, 2, sb_tail),
            stacked(outs_p, 3, kv_tail), stacked(outs_s, 3, kv_tail),
            jnp.stack([o[4] for o in outs_p]), jnp.stack([o[4] for o in outs_s]))
```

```python
import functools
import math

import jax
import jax.numpy as jnp
import numpy as np
from jax import lax
from jax.experimental import pallas as pl
from jax.experimental.pallas import tpu as pltpu

F32, BF16 = jnp.float32, jnp.bfloat16
HI = lax.Precision.HIGHEST

D_MODEL = 1024
HEAD_DIM = 64
MIX_W = D_MODEL // 2
GLA_HEADS = 4
GLA_DV = MIX_W // GLA_HEADS
GLA_DK = GLA_DV // 2
GLA_GATE_RANK = 16
GLA_TAU = 16.0
GLA_CHUNK = 64
GLA_SUB = 16
NSA_HEADS = MIX_W // HEAD_DIM
NSA_KV_HEADS = 2
NSA_KV_W = NSA_KV_HEADS * HEAD_DIM
CMP_STRIDE = 16
CMP_BLOCK = 32
CMP_HIDDEN = 128
SEL_BLOCK = 64
SEL_TOPN = 16
WINDOW = 512
FORCE_BONUS = 1e4
SB_HEADS = MIX_W // HEAD_DIM
REL_BUCKETS = 32
REL_MAX_DIST = 128
D_FF = -(-8 * D_MODEL // (3 * 256)) * 256
Q_BLK = 128
LN_EPS = 1e-5
TRUNK_DEPTH = 4
DN_ALPHA = (2 * TRUNK_DEPTH) ** 0.25
PAGE_SIZE = 128
GLA_QK_W = GLA_HEADS * GLA_DK
IN_WIDTHS = (GLA_QK_W, GLA_QK_W, MIX_W, MIX_W, GLA_GATE_RANK,
             MIX_W, NSA_KV_W, NSA_KV_W, NSA_KV_W, NSA_KV_W, NSA_KV_W, NSA_KV_W, 3 * NSA_HEADS,
             MIX_W, MIX_W, MIX_W,
             3 * D_MODEL)

LANES = 128
SUBLANES = 8
KEY_TILE = 128
VMEM_LIMIT = 56 * 1024 * 1024
NEG = -1e30
SEL_LANES = 64
SB_UNDERFLOW = -104.0
SB_PROMPT_HEADS = 4
GLA_SAFE_DECAY = 60.0

COL = dict(mg=(0, 3072), sbkv=(3072, 1024), gv=(4096, 512), gr=(4608, 512), nq=(5120, 512),
           sbq=(5632, 512), gqk=(6144, 512), kvc=(6656, 256), kvs=(6912, 256), kvw=(7168, 256),
           ga=(7424, 128), ng=(7552, 128))
ZW = 7680


def _cb(name, width=None):
    off, w = COL[name]
    width = width or w
    assert off % width == 0
    return off // width


def _cparams(*sem):
    return pltpu.CompilerParams(dimension_semantics=sem, vmem_limit_bytes=VMEM_LIMIT)


def _dot(a, b):
    return jnp.dot(a, b, preferred_element_type=F32)


def _dot_hi(a, b):
    return jnp.dot(a, b, precision=HI, preferred_element_type=F32)


def _dot_nt(a, b):
    return lax.dot_general(a, b, (((1,), (1,)), ((), ())), preferred_element_type=F32)


def _dot_tn(a, b):
    return lax.dot_general(a, b, (((0,), (0,)), ((), ())), preferred_element_type=F32)


def _log_sigmoid(x):
    return jnp.minimum(x, 0.0) - jnp.log(1.0 + jnp.exp(-jnp.abs(x)))


def _iota(shape, dim):
    return lax.broadcasted_iota(jnp.int32, shape, dim)


def _pick_tile(n, cands=(1024, 512, 256, 128, 64, 32, 16, 8)):
    for c in cands:
        if n % c == 0:
            return c
    raise ValueError(f"no row tile divides {n}")


def _linear_kernel(x_ref, w_ref, b_ref, o_ref):
    o_ref[...] = _dot(x_ref[...].astype(BF16), w_ref[...]) + b_ref[...]


def _linear(x, w, b):
    n, k = x.shape
    dout = w.shape[1]
    tm, tn = _pick_tile(n), 512
    return pl.pallas_call(
        _linear_kernel, grid=(n // tm, dout // tn),
        in_specs=[pl.BlockSpec((tm, k), lambda i, j: (i, 0)),
                  pl.BlockSpec((k, tn), lambda i, j: (0, j)),
                  pl.BlockSpec((1, tn), lambda i, j: (0, j))],
        out_specs=pl.BlockSpec((tm, tn), lambda i, j: (i, j)),
        out_shape=jax.ShapeDtypeStruct((n, dout), F32),
        compiler_params=_cparams("parallel", "parallel"), name="in_proj")(x, w, b)


def _layer_norm(h, g, b):
    mu = jnp.mean(h, -1, keepdims=True)
    d = h - mu
    var = jnp.mean(d * d, -1, keepdims=True)
    return d * lax.rsqrt(var + LN_EPS) * g + b


def _merge_kernel(og_ref, oc_ref, os_ref, ow_ref, ob_ref, mg_ref, x_ref, wb_ref, wo_ref, g_ref, b_ref, o_ref):
    o_nsa = oc_ref[...] + os_ref[...] + ow_ref[...]
    mixed = None
    for br, o in enumerate((og_ref[...], o_nsa, ob_ref[...])):
        proj = _dot(o.astype(BF16), wb_ref[br])
        term = jax.nn.sigmoid(mg_ref[:, br * D_MODEL:(br + 1) * D_MODEL]) * proj
        mixed = term if mixed is None else mixed + term
    y = _dot(mixed.astype(BF16), wo_ref[...])
    o_ref[...] = _layer_norm(DN_ALPHA * x_ref[...] + y, g_ref[...], b_ref[...])


def _merge(o_gla, o_cmp, o_slc, o_win, o_sb, z, x, wb, wo, g, b):
    n = x.shape[0]
    tm = _pick_tile(n, (512, 256, 128, 64, 32, 16, 8))
    mix = pl.BlockSpec((tm, MIX_W), lambda i: (i, 0))
    return pl.pallas_call(
        _merge_kernel, grid=(n // tm,),
        in_specs=[mix, mix, mix, mix, mix,
                  pl.BlockSpec((tm, 3 * D_MODEL), lambda i: (i, _cb("mg"))),
                  pl.BlockSpec((tm, D_MODEL), lambda i: (i, 0)),
                  pl.BlockSpec((3, MIX_W, D_MODEL), lambda i: (0, 0, 0)),
                  pl.BlockSpec((D_MODEL, D_MODEL), lambda i: (0, 0)),
                  pl.BlockSpec((1, D_MODEL), lambda i: (0, 0)),
                  pl.BlockSpec((1, D_MODEL), lambda i: (0, 0))],
        out_specs=pl.BlockSpec((tm, D_MODEL), lambda i: (i, 0)),
        out_shape=jax.ShapeDtypeStruct((n, D_MODEL), F32),
        compiler_params=_cparams("parallel"), name="merge")(o_gla, o_cmp, o_slc, o_win, o_sb, z, x, wb, wo, g, b)


FF_TILE = 256


def _ffn_kernel(x_ref, wg_ref, wu_ref, wd_ref, g_ref, b_ref, o_ref, xb_ref, acc_ref):
    f = pl.program_id(1)

    @pl.when(f == 0)
    def _():
        xb_ref[...] = x_ref[...].astype(BF16)
        acc_ref[...] = jnp.zeros_like(acc_ref)

    xb = xb_ref[...]
    gate = _dot(xb, wg_ref[...])
    up = _dot(xb, wu_ref[...])
    h = gate * jax.nn.sigmoid(gate) * up
    acc_ref[...] += _dot(h.astype(BF16), wd_ref[...])

    @pl.when(f == pl.num_programs(1) - 1)
    def _():
        o_ref[...] = _layer_norm(DN_ALPHA * x_ref[...] + acc_ref[...], g_ref[...], b_ref[...])


def _ffn(x, wg, wu, wd, g, b):
    n = x.shape[0]
    tm = _pick_tile(n)
    return pl.pallas_call(
        _ffn_kernel, grid=(n // tm, D_FF // FF_TILE),
        in_specs=[pl.BlockSpec((tm, D_MODEL), lambda i, f: (i, 0)),
                  pl.BlockSpec((D_MODEL, FF_TILE), lambda i, f: (0, f)),
                  pl.BlockSpec((D_MODEL, FF_TILE), lambda i, f: (0, f)),
                  pl.BlockSpec((FF_TILE, D_MODEL), lambda i, f: (f, 0)),
                  pl.BlockSpec((1, D_MODEL), lambda i, f: (0, 0)),
                  pl.BlockSpec((1, D_MODEL), lambda i, f: (0, 0))],
        out_specs=pl.BlockSpec((tm, D_MODEL), lambda i, f: (i, 0)),
        out_shape=jax.ShapeDtypeStruct((n, D_MODEL), F32),
        scratch_shapes=[pltpu.VMEM((tm, D_MODEL), BF16), pltpu.VMEM((tm, D_MODEL), F32)],
        compiler_params=_cparams("parallel", "arbitrary"), name="ffn")(x, wg, wu, wd, g, b)


def _gla_kernel(qk_ref, v_ref, r_ref, ga_ref, s0_ref, wa_ref, ba_ref, ng_ref, o_ref, st_ref, *, chunk, n_chunks):
    C = chunk

    @pl.when(pl.program_id(1) == 0)
    def _():
        st_ref[...] = s0_ref[...]

    sub = min(C, GLA_SUB)
    n_sub = C // sub
    causal = _iota((C, C), 1) <= _iota((C, C), 0)
    tril = causal.astype(F32)
    row_sub = _iota((sub, GLA_DK), 0)

    def intra_factored(qh, kh, vh, ch):
        qd = qh * jnp.exp(ch)
        kd = kh * jnp.exp(-ch)
        att = jnp.where(causal, _dot_nt(qd.astype(BF16), kd.astype(BF16)), 0.0)
        return _dot(att.astype(BF16), vh.astype(BF16))

    def intra_blocked(qh, kh, vh, ch):
        outs = []
        for sb in range(n_sub):
            a0 = sb * sub
            qs, ks, vs, cs = qh[a0:a0 + sub], kh[a0:a0 + sub], vh[a0:a0 + sub], ch[a0:a0 + sub]
            oi = jnp.zeros((sub, GLA_DV), F32)
            if sb > 0:
                cref = ch[a0 - 1:a0]
                qd = qs * jnp.exp(cs - cref)
                kd = kh[:a0] * jnp.exp(cref - ch[:a0])
                att = _dot_nt(qd.astype(BF16), kd.astype(BF16))
                oi = oi + _dot(att.astype(BF16), vh[:a0].astype(BF16))
            for s in range(sub):
                e = jnp.exp(jnp.minimum(cs - cs[s:s + 1], 0.0))
                w = jnp.sum(jnp.where(row_sub >= s, qs * ks[s:s + 1] * e, 0.0), axis=-1, keepdims=True)
                oi = oi + w * vs[s:s + 1]
            outs.append(oi)
        return outs[0] if n_sub == 1 else jnp.concatenate(outs, axis=0)

    def chunk_body(c, carry):
        r0 = pl.multiple_of(c * C, C)
        rows = pl.ds(r0, C)
        qk = qk_ref[rows, :]
        v = v_ref[rows, :]
        log_a = _log_sigmoid(_dot_hi(ga_ref[rows, :], wa_ref[...]) + ba_ref[...]) * (1.0 / GLA_TAU)
        cum = _dot_hi(tril, log_a)
        parts = []
        for h in range(GLA_HEADS):
            parts.append((qk[:, h * GLA_DK:(h + 1) * GLA_DK] * (GLA_DK ** -0.5),
                          qk[:, GLA_QK_W + h * GLA_DK:GLA_QK_W + (h + 1) * GLA_DK],
                          v[:, h * GLA_DV:(h + 1) * GLA_DV],
                          cum[:, h * GLA_DK:(h + 1) * GLA_DK]))
        small_decay = jnp.max(-cum[C - 1:C, :]) <= GLA_SAFE_DECAY
        intra = lax.cond(small_decay,
                         lambda: tuple(intra_factored(*p) for p in parts),
                         lambda: tuple(intra_blocked(*p) for p in parts))
        for h in range(GLA_HEADS):
            qh, kh, vh, ch = parts[h]
            st = st_ref[h]
            o = intra[h] + _dot_nt((qh * jnp.exp(ch)).astype(BF16), st.astype(BF16))
            mu = jnp.mean(o, -1, keepdims=True)
            d = o - mu
            var = jnp.mean(d * d, -1, keepdims=True)
            lanes = slice(h * GLA_DV, (h + 1) * GLA_DV)
            rr = r_ref[rows, lanes]
            o_ref[rows, lanes] = d * lax.rsqrt(var + LN_EPS) * ng_ref[:, lanes] * (rr * jax.nn.sigmoid(rr))
            last = ch[C - 1:C]
            kdl = kh * jnp.exp(last - ch)
            st_ref[h] = st * jnp.exp(last) + _dot_tn(vh.astype(BF16), kdl.astype(BF16))
        return carry

    lax.fori_loop(0, n_chunks, chunk_body, 0)


def _gla(z3, s0t, wa, ba, norm_g):
    B, T, _ = z3.shape
    C = math.gcd(T, GLA_CHUNK)
    tt = min(T, 512)
    assert T % tt == 0 and tt % C == 0
    kern = functools.partial(_gla_kernel, chunk=C, n_chunks=tt // C)
    st_spec = pl.BlockSpec((None, GLA_HEADS, GLA_DV, GLA_DK), lambda b, i: (b, 0, 0, 0))
    return pl.pallas_call(
        kern, grid=(B, T // tt),
        in_specs=[pl.BlockSpec((None, tt, 512), lambda b, i: (b, i, _cb("gqk"))),
                  pl.BlockSpec((None, tt, 512), lambda b, i: (b, i, _cb("gv"))),
                  pl.BlockSpec((None, tt, 512), lambda b, i: (b, i, _cb("gr"))),
                  pl.BlockSpec((None, tt, LANES), lambda b, i: (b, i, _cb("ga"))),
                  st_spec,
                  pl.BlockSpec((LANES, GLA_QK_W), lambda b, i: (0, 0)),
                  pl.BlockSpec((1, GLA_QK_W), lambda b, i: (0, 0)),
                  pl.BlockSpec((1, MIX_W), lambda b, i: (0, 0))],
        out_specs=[pl.BlockSpec((None, tt, MIX_W), lambda b, i: (b, i, 0)), st_spec],
        out_shape=[jax.ShapeDtypeStruct((B, T, MIX_W), F32),
                   jax.ShapeDtypeStruct((B, GLA_HEADS, GLA_DV, GLA_DK), F32)],
        compiler_params=_cparams("parallel", "arbitrary"), name="gla")(z3, z3, z3, z3, s0t, wa, ba, norm_g)


def _compress_kernel(rows_ref, w1_ref, pe_ref, b1_ref, w2_ref, o_ref, h1_ref, *, n_seg):
    M = n_seg
    h0 = jnp.zeros((M, 2 * CMP_HIDDEN), F32)
    h1 = jnp.zeros((M, 2 * CMP_HIDDEN), F32)
    for j in range(CMP_STRIDE):
        x = rows_ref[pl.ds(j, M, stride=CMP_STRIDE), :]
        j1 = CMP_STRIDE + j
        h0 = h0 + _dot((x + pe_ref[j:j + 1, :]).astype(BF16), w1_ref[j])
        h1 = h1 + _dot((x + pe_ref[j1:j1 + 1, :]).astype(BF16), w1_ref[j1])
    h1_ref[0:M, :] = h1
    h1_ref[M:M + SUBLANES, :] = jnp.zeros((SUBLANES, 2 * CMP_HIDDEN), F32)
    h = jax.nn.gelu(b1_ref[...] + h0 + h1_ref[pl.ds(1, M), :])
    o_ref[...] = _dot(h.astype(BF16), w2_ref[...])


def _compress(rows3, colblk, n_rows, w1bd, pe2, b1t, w2bd):
    B = rows3.shape[0]
    n_seg = n_rows // CMP_STRIDE
    kern = functools.partial(_compress_kernel, n_seg=n_seg)
    return pl.pallas_call(
        kern, grid=(B, 2),
        in_specs=[pl.BlockSpec((None, n_rows, NSA_KV_W), lambda b, c: (b, 0, 2 * colblk + c)),
                  pl.BlockSpec((CMP_BLOCK, None, NSA_KV_W, 2 * CMP_HIDDEN), lambda b, c: (0, c, 0, 0)),
                  pl.BlockSpec((None, CMP_BLOCK, NSA_KV_W), lambda b, c: (c, 0, 0)),
                  pl.BlockSpec((None, 1, 2 * CMP_HIDDEN), lambda b, c: (c, 0, 0)),
                  pl.BlockSpec((None, 2 * CMP_HIDDEN, NSA_KV_W), lambda b, c: (c, 0, 0))],
        out_specs=pl.BlockSpec((None, n_seg, NSA_KV_W), lambda b, c: (b, 0, c)),
        out_shape=jax.ShapeDtypeStruct((B, n_seg, 2 * NSA_KV_W), F32),
        scratch_shapes=[pltpu.VMEM((n_seg + SUBLANES, 2 * CMP_HIDDEN), F32)],
        compiler_params=_cparams("parallel", "parallel"), name="compress")(rows3, w1bd, pe2, b1t, w2bd)


def _compress_paged_kernel(pt_ref, *refs, n_pages):
    pages = refs[:n_pages]
    w1_ref, pe_ref, b1_ref, w2_ref, o_ref, h1_ref, rows_ref = refs[n_pages:]
    for p, page in enumerate(pages):
        rows_ref[p * PAGE_SIZE:(p + 1) * PAGE_SIZE, :] = page[...].reshape(NSA_KV_W, PAGE_SIZE).T
    _compress_kernel(rows_ref, w1_ref, pe_ref, b1_ref, w2_ref, o_ref, h1_ref, n_seg=n_pages * PAGE_SIZE // CMP_STRIDE)


def _page_spec(block, layer, n_pages, page_of, *rest):
    def index_map(b, *idx_and_pt):
        *idx, pt = idx_and_pt
        return (layer, pt[b * n_pages + page_of(*idx)]) + tuple(r(*idx) if callable(r) else r for r in rest)
    return pl.BlockSpec(block, index_map)


def _compress_paged(pool_t, layer, pt_flat, n_pages, w1bd, pe2, b1t, w2bd):
    B = pt_flat.shape[0] // n_pages
    n_seg = n_pages * PAGE_SIZE // CMP_STRIDE
    kern = functools.partial(_compress_paged_kernel, n_pages=n_pages)
    page_specs = [_page_spec((None, None, None, NSA_KV_HEADS, HEAD_DIM, PAGE_SIZE), layer, n_pages,
                             (lambda c, p=p: p), (lambda c: c), 0, 0, 0) for p in range(n_pages)]
    grid_spec = pltpu.PrefetchScalarGridSpec(
        num_scalar_prefetch=1, grid=(B, 2),
        in_specs=page_specs + [
            pl.BlockSpec((CMP_BLOCK, None, NSA_KV_W, 2 * CMP_HIDDEN), lambda b, c, pt: (0, c, 0, 0)),
            pl.BlockSpec((None, CMP_BLOCK, NSA_KV_W), lambda b, c, pt: (c, 0, 0)),
            pl.BlockSpec((None, 1, 2 * CMP_HIDDEN), lambda b, c, pt: (c, 0, 0)),
            pl.BlockSpec((None, 2 * CMP_HIDDEN, NSA_KV_W), lambda b, c, pt: (c, 0, 0))],
        out_specs=pl.BlockSpec((None, n_seg, NSA_KV_W), lambda b, c, pt: (b, 0, c)),
        scratch_shapes=[pltpu.VMEM((n_seg + SUBLANES, 2 * CMP_HIDDEN), F32),
                        pltpu.VMEM((n_pages * PAGE_SIZE, NSA_KV_W), F32)])
    return pl.pallas_call(
        kern, grid_spec=grid_spec,
        out_shape=jax.ShapeDtypeStruct((B, n_seg, 2 * NSA_KV_W), F32),
        compiler_params=_cparams("parallel", "parallel"), name="compress_paged")(
            pt_flat, *([pool_t] * n_pages), w1bd, pe2, b1t, w2bd)


def _build_qbd(q_ref, qbd_ref, tq):
    lane = _iota((tq, LANES), 1)
    for p in range(NSA_HEADS):
        slab = q_ref[:, (p // 2) * LANES:(p // 2 + 1) * LANES] * (HEAD_DIM ** -0.5)
        keep = (lane >= HEAD_DIM) if p % 2 else (lane < HEAD_DIM)
        qbd_ref[p * tq:(p + 1) * tq, :] = jnp.where(keep, slab, 0.0)


def _assemble_heads(acc_ref, ng_ref, o_ref, tq, branch):
    lo = _iota((tq, LANES), 1) < HEAD_DIM
    for k in range(NSA_HEADS // 2):
        a0 = acc_ref[(2 * k) * tq:(2 * k + 1) * tq, :]
        a1 = acc_ref[(2 * k + 1) * tq:(2 * k + 2) * tq, :]
        c0 = branch * NSA_HEADS + 2 * k
        g0 = jax.nn.sigmoid(ng_ref[:, c0:c0 + 1])
        g1 = jax.nn.sigmoid(ng_ref[:, c0 + 1:c0 + 2])
        o_ref[:, k * LANES:(k + 1) * LANES] = jnp.where(lo, a0 * g0, a1 * g1)


def _masked_softmax(s, valid):
    s = jnp.where(valid, s, NEG)
    m = jnp.max(s, -1, keepdims=True)
    e = jnp.where(valid, jnp.exp(s - m), 0.0)
    return e / jnp.maximum(jnp.sum(e, -1, keepdims=True), 1e-30)


def _cmp_kernel(q_ref, kvc_ref, bias_ref, cover_ref, ng_ref, o_ref, sel_ref, qbd_ref, acc_ref, *, tq, rc, pos0):
    qi = pl.program_id(1)
    _build_qbd(q_ref, qbd_ref, tq)
    n = kvc_ref.shape[0]
    kc = kvc_ref[:, 0:NSA_KV_W].astype(BF16)
    vc = kvc_ref[:, NSA_KV_W:2 * NSA_KV_W].astype(BF16)
    heads_per_chunk = rc // tq
    p_sum = [jnp.zeros((tq, n), F32), jnp.zeros((tq, n), F32)]
    for c in range(NSA_HEADS * tq // rc):
        rows = slice(c * rc, (c + 1) * rc)
        bias = bias_ref[rows, :]
        s = _dot_nt(qbd_ref[rows, :].astype(BF16), kc) + bias
        pr = _masked_softmax(s, bias > 0.1 * NEG)
        acc_ref[rows, :] = _dot(pr.astype(BF16), vc)
        for pp in range(heads_per_chunk):
            g = (c * heads_per_chunk + pp) % 2
            p_sum[g] = p_sum[g] + pr[pp * tq:(pp + 1) * tq]
    _assemble_heads(acc_ref, ng_ref, o_ref, tq, 0)

    score = _dot_hi(p_sum[0], cover_ref[0]) + _dot_hi(p_sum[1], cover_ref[1])
    lane = _iota((tq, LANES), 1)
    blk = jnp.bitwise_and(lane, SEL_LANES - 1)
    pos = pos0 + qi * tq + _iota((tq, LANES), 0)
    cur = lax.shift_right_logical(pos, int(math.log2(SEL_BLOCK)))
    forced = (blk == 0) | (blk == cur) | (blk == cur - 1)
    score = jnp.where(forced, score + FORCE_BONUS, score)
    visible = blk <= cur
    score = jnp.where(visible, score, -jnp.inf)
    lo = lane < SEL_LANES
    rank = jnp.zeros((tq, LANES), F32)
    for mp in range(SEL_LANES):
        other = jnp.where(lo, score[:, mp:mp + 1], score[:, SEL_LANES + mp:SEL_LANES + mp + 1])
        beats = (other > score) | ((other == score) & (blk > mp))
        rank = rank + jnp.where(beats, 1.0, 0.0)
    sel_ref[...] = jnp.where((rank < SEL_TOPN) & visible, 1.0, 0.0)


def _nsa_common_specs(tq):
    q_spec = pl.BlockSpec((None, tq, MIX_W), lambda b, i: (b, i, _cb("nq")))
    ng_spec = pl.BlockSpec((None, tq, LANES), lambda b, i: (b, i, _cb("ng")))
    o_spec = pl.BlockSpec((None, tq, MIX_W), lambda b, i: (b, i, 0))
    return q_spec, ng_spec, o_spec


def _nsa_scratch(tq):
    return [pltpu.VMEM((NSA_HEADS * tq, LANES), F32), pltpu.VMEM((NSA_HEADS * tq, LANES), F32)]


def _cmp_attend(z3, kvc, bias_tab, cover2, tq, pos0):
    B, T, _ = z3.shape
    n = kvc.shape[1]
    R = NSA_HEADS * tq
    rc = min(R, 128)
    q_spec, ng_spec, o_spec = _nsa_common_specs(tq)
    kern = functools.partial(_cmp_kernel, tq=tq, rc=rc, pos0=pos0)
    return pl.pallas_call(
        kern, grid=(B, T // tq),
        in_specs=[q_spec,
                  pl.BlockSpec((None, n, 2 * NSA_KV_W), lambda b, i: (b, 0, 0)),
                  pl.BlockSpec((None, R, n), lambda b, i: (i, 0, 0)),
                  pl.BlockSpec((2, n, LANES), lambda b, i: (0, 0, 0)),
                  ng_spec],
        out_specs=[o_spec, pl.BlockSpec((None, tq, LANES), lambda b, i: (b, i, 0))],
        out_shape=[jax.ShapeDtypeStruct((B, T, MIX_W), F32), jax.ShapeDtypeStruct((B, T, LANES), F32)],
        scratch_shapes=_nsa_scratch(tq),
        compiler_params=_cparams("parallel", "parallel"), name="nsa_cmp")(z3, kvc, bias_tab, cover2, z3)


def _sel_kernel(q_ref, sel_ref, kv_ref, near_ref, ng_ref, emat_ref, o_ref,
                qbd_ref, acc_ref, l_ref, mx_ref, selx_ref, *, tq, rc, tile0, n_tiles):
    qi = pl.program_id(1)
    qt = tile0 + (qi * tq) // KEY_TILE
    R = NSA_HEADS * tq
    n_chunks = R // rc
    heads_per_chunk = rc // tq
    _build_qbd(q_ref, qbd_ref, tq)
    sel_b = sel_ref[...].astype(BF16)
    for g in range(NSA_KV_HEADS):
        for t in range(n_tiles):
            hit = _dot(sel_b, emat_ref[g, :, t * KEY_TILE:(t + 1) * KEY_TILE])
            selx_ref[g, t] = (hit - 1.0) * (-NEG)
    ones = jnp.ones((KEY_TILE, LANES), BF16)
    n_far = jnp.maximum(qt - 1, 0)
    drop_prev = ((qt >= 1).astype(F32) - 1.0) * (-NEG)
    near_tiles = ((n_far, lambda rows: near_ref[rows, 0:KEY_TILE] + drop_prev),
                  (qt, lambda rows: near_ref[rows, KEY_TILE:2 * KEY_TILE]))

    def logits(kt, kk, c, bias_of):
        rows = slice(c * rc, (c + 1) * rc)
        se = [selx_ref[(c * heads_per_chunk + pp) % 2, kt] for pp in range(heads_per_chunk)]
        s = _dot_nt(qbd_ref[rows, :].astype(BF16), kk) + (se[0] if len(se) == 1 else jnp.concatenate(se, axis=0))
        return rows, (s if bias_of is None else s + bias_of(rows))

    def key_rows(kt):
        return pl.ds(pl.multiple_of(kt * KEY_TILE, KEY_TILE), KEY_TILE)

    def sweep_max(kt, bias_of):
        kk = kv_ref[key_rows(kt), 0:NSA_KV_W].astype(BF16)
        for c in range(n_chunks):
            rows, s = logits(kt, kk, c, bias_of)
            mx_ref[rows, :] = jnp.maximum(mx_ref[rows, :], s)

    mx_ref[...] = jnp.full((R, LANES), NEG, F32)
    lax.fori_loop(0, n_far, lambda kt, cr: (sweep_max(kt, None), cr)[1], 0)
    for kt, bias_of in near_tiles:
        sweep_max(kt, bias_of)
    row_max = [jnp.max(mx_ref[c * rc:(c + 1) * rc, :], -1, keepdims=True) for c in range(n_chunks)]

    def sweep_acc(kt, bias_of):
        kk = kv_ref[key_rows(kt), 0:NSA_KV_W].astype(BF16)
        vv = kv_ref[key_rows(kt), NSA_KV_W:2 * NSA_KV_W].astype(BF16)
        for c in range(n_chunks):
            rows, s = logits(kt, kk, c, bias_of)
            pe = jnp.exp(s - row_max[c]).astype(BF16)
            acc_ref[rows, :] += _dot(pe, vv)
            l_ref[rows, :] += _dot(pe, ones)

    acc_ref[...] = jnp.zeros((R, LANES), F32)
    l_ref[...] = jnp.zeros((R, LANES), F32)
    lax.fori_loop(0, n_far, lambda kt, cr: (sweep_acc(kt, None), cr)[1], 0)
    for kt, bias_of in near_tiles:
        sweep_acc(kt, bias_of)
    acc_ref[...] = acc_ref[...] / l_ref[...]
    _assemble_heads(acc_ref, ng_ref, o_ref, tq, 1)


def _sel_attend(z3, sel, keys3, key_colblk, n_keys, near_tab, tq, pos0):
    B, T, _ = z3.shape
    R = NSA_HEADS * tq
    rc = min(R, 128)
    n_tiles = n_keys // KEY_TILE
    assert pos0 % KEY_TILE == 0 and (tq == KEY_TILE or T == tq) and 2 * n_tiles <= SEL_LANES
    slot = jnp.arange(LANES)[:, None]
    blk = jnp.arange(n_keys)[None, :] // SEL_BLOCK
    emat = jnp.stack([slot == blk + g * SEL_LANES for g in range(NSA_KV_HEADS)]).astype(BF16)
    q_spec, ng_spec, o_spec = _nsa_common_specs(tq)
    kern = functools.partial(_sel_kernel, tq=tq, rc=rc, tile0=pos0 // KEY_TILE, n_tiles=n_tiles)
    return pl.pallas_call(
        kern, grid=(B, T // tq),
        in_specs=[q_spec,
                  pl.BlockSpec((None, tq, LANES), lambda b, i: (b, i, 0)),
                  pl.BlockSpec((None, n_keys, 2 * NSA_KV_W), lambda b, i: (b, 0, key_colblk)),
                  pl.BlockSpec((R, 2 * KEY_TILE), lambda b, i: (0, 0)),
                  ng_spec,
                  pl.BlockSpec((NSA_KV_HEADS, LANES, n_keys), lambda b, i: (0, 0, 0))],
        out_specs=o_spec,
        out_shape=jax.ShapeDtypeStruct((B, T, MIX_W), F32),
        scratch_shapes=_nsa_scratch(tq) + [pltpu.VMEM((R, LANES), F32), pltpu.VMEM((R, LANES), F32),
                                           pltpu.VMEM((NSA_KV_HEADS, n_tiles, tq, KEY_TILE), F32)],
        compiler_params=_cparams("parallel", "parallel"), name="nsa_slc")(
            z3, sel, keys3, near_tab, z3, emat)


WIN_KEYS = WINDOW + Q_BLK


def _win_kernel(q_ref, kv_ref, bias_ref, ng_ref, o_ref, qbd_ref, acc_ref, *, tq, rc, valid_from):
    qi = pl.program_id(1)
    _build_qbd(q_ref, qbd_ref, tq)
    k0 = pl.multiple_of(qi * tq, tq)
    kw = kv_ref[pl.ds(k0, WIN_KEYS), 0:NSA_KV_W].astype(BF16)
    vw = kv_ref[pl.ds(k0, WIN_KEYS), NSA_KV_W:2 * NSA_KV_W].astype(BF16)
    real = (k0 + _iota((rc, WIN_KEYS), 1)) >= valid_from
    for c in range(NSA_HEADS * tq // rc):
        rows = slice(c * rc, (c + 1) * rc)
        bias = bias_ref[rows, :]
        s = _dot_nt(qbd_ref[rows, :].astype(BF16), kw) + bias
        pr = _masked_softmax(s, (bias > 0.1 * NEG) & real)
        acc_ref[rows, :] = _dot(pr.astype(BF16), vw)
    _assemble_heads(acc_ref, ng_ref, o_ref, tq, 2)


def _win_attend(z3, keys3, win_tab, tq, valid_from):
    B, T, _ = z3.shape
    n_keys = keys3.shape[1]
    R = NSA_HEADS * tq
    rc = min(R, 128)
    q_spec, ng_spec, o_spec = _nsa_common_specs(tq)
    kern = functools.partial(_win_kernel, tq=tq, rc=rc, valid_from=valid_from)
    return pl.pallas_call(
        kern, grid=(B, T // tq),
        in_specs=[q_spec,
                  pl.BlockSpec((None, n_keys, 2 * NSA_KV_W), lambda b, i: (b, 0, 0)),
                  pl.BlockSpec((R, WIN_KEYS), lambda b, i: (0, 0)),
                  ng_spec],
        out_specs=o_spec,
        out_shape=jax.ShapeDtypeStruct((B, T, MIX_W), F32),
        scratch_shapes=_nsa_scratch(tq),
        compiler_params=_cparams("parallel", "parallel"), name="nsa_win")(z3, keys3, win_tab, z3)


def _sb_kernel(q_ref, k_ref, v_ref, o_ref, *, tq, heads, tile0):
    qi = pl.program_id(2)
    qt = tile0 + (qi * tq) // KEY_TILE
    sw = heads * HEAD_DIM
    R = heads * tq
    lane_head = lax.shift_right_logical(_iota((tq, sw), 1), int(math.log2(HEAD_DIM)))
    q = q_ref[...] * (HEAD_DIM ** -0.5)
    qbd = jnp.concatenate([jnp.where(lane_head == j, q, 0.0) for j in range(heads)], axis=0).astype(BF16)
    after = (_iota((KEY_TILE, KEY_TILE), 0) > _iota((KEY_TILE, KEY_TILE), 1)).astype(BF16)
    causal = _iota((R, KEY_TILE), 1) < jnp.bitwise_and(_iota((R, KEY_TILE), 0), tq - 1)

    def tile(kt, run, acc, masked):
        rows = pl.ds(pl.multiple_of(kt * KEY_TILE, KEY_TILE), KEY_TILE)
        z = _dot_nt(qbd, k_ref[rows, :].astype(BF16))
        u = _log_sigmoid(-z)
        if masked:
            u = jnp.where(causal, u, 0.0)
        u_hi = u.astype(BF16)
        u_lo = (u - u_hi.astype(F32)).astype(BF16)
        rest = _dot(u_hi, after) + _dot(u_lo, after)
        a = jnp.exp(u + z + rest + run)
        if masked:
            a = jnp.where(causal, a, 0.0)
        acc = acc + _dot(a.astype(BF16), v_ref[rows, :].astype(BF16))
        return run + jnp.sum(u, -1, keepdims=True), acc

    run, acc = tile(qt, jnp.zeros((R, 1), F32), jnp.zeros((R, sw), F32), True)

    def more(st):
        return (st[0] >= 0) & (jnp.max(st[1]) > SB_UNDERFLOW)

    def step(st):
        run, acc = tile(st[0], st[1], st[2], False)
        return st[0] - 1, run, acc

    _, run, acc = lax.while_loop(more, step, (qt - 1, run, acc))
    out = None
    for j in range(heads):
        part = jnp.where(lane_head == j, acc[j * tq:(j + 1) * tq], 0.0)
        out = part if out is None else out + part
    o_ref[...] = out


def _sb_attend(z3, keys3, k_colblk0, n_keys, tq, heads, pos0):
    B, T, _ = z3.shape
    sw = heads * HEAD_DIM
    n_slabs = SB_HEADS // heads
    assert pos0 % KEY_TILE == 0 and (tq == KEY_TILE or T == tq) and tq & (tq - 1) == 0
    kern = functools.partial(_sb_kernel, tq=tq, heads=heads, tile0=pos0 // KEY_TILE)
    qcb = _cb("sbq", sw)
    return pl.pallas_call(
        kern, grid=(B, n_slabs, T // tq),
        in_specs=[pl.BlockSpec((None, tq, sw), lambda b, s, i: (b, i, qcb + s)),
                  pl.BlockSpec((None, n_keys, sw), lambda b, s, i: (b, 0, k_colblk0 + s)),
                  pl.BlockSpec((None, n_keys, sw), lambda b, s, i: (b, 0, k_colblk0 + n_slabs + s))],
        out_specs=pl.BlockSpec((None, tq, sw), lambda b, s, i: (b, i, s)),
        out_shape=jax.ShapeDtypeStruct((B, T, MIX_W), F32),
        compiler_params=_cparams("parallel", "parallel", "parallel"), name="sb")(z3, keys3, keys3)


def _new_tile(new_ref, lanes):
    x = new_ref[:, lanes]
    return jnp.concatenate([x, jnp.zeros((KEY_TILE - x.shape[0], x.shape[1]), F32)], axis=0).astype(BF16)


def _sb_paged_kernel(pt_ref, q_ref, new_ref, *refs, tq, pages_per_step):
    pages = refs[:pages_per_step]
    o_ref, qbd_ref, run_ref, acc_ref = refs[pages_per_step:]
    j = pl.program_id(1)
    sw = SB_HEADS * HEAD_DIM
    R = SB_HEADS * tq
    after = (_iota((KEY_TILE, KEY_TILE), 0) > _iota((KEY_TILE, KEY_TILE), 1)).astype(BF16)
    lane_head = lax.shift_right_logical(_iota((tq, sw), 1), int(math.log2(HEAD_DIM)))

    def tile(z, v_acc, mask):
        u = _log_sigmoid(-z)
        if mask is not None:
            u = jnp.where(mask, u, 0.0)
        u_hi = u.astype(BF16)
        u_lo = (u - u_hi.astype(F32)).astype(BF16)
        a = jnp.exp(u + z + _dot(u_hi, after) + _dot(u_lo, after) + run_ref[...])
        if mask is not None:
            a = jnp.where(mask, a, 0.0)
        acc_ref[...] += v_acc(a.astype(BF16))
        run_ref[...] += jnp.sum(u, -1, keepdims=True)

    @pl.when(j == 0)
    def _():
        q = q_ref[...] * (HEAD_DIM ** -0.5)
        for h in range(SB_HEADS):
            qbd_ref[h * tq:(h + 1) * tq, :] = jnp.where(lane_head == h, q, 0.0)
        run_ref[...] = jnp.zeros((R, 1), F32)
        acc_ref[...] = jnp.zeros((R, sw), F32)
        k_new = _new_tile(new_ref, slice(0, sw))
        v_new = _new_tile(new_ref, slice(sw, 2 * sw))
        causal = _iota((R, KEY_TILE), 1) < jnp.bitwise_and(_iota((R, KEY_TILE), 0), tq - 1)
        tile(_dot_nt(qbd_ref[...].astype(BF16), k_new), lambda a: _dot(a, v_new), causal)

    for page in reversed(pages):
        @pl.when(jnp.max(run_ref[...]) > SB_UNDERFLOW)
        def _(page=page):
            kt = page[0].reshape(sw, PAGE_SIZE).astype(BF16)
            vt = page[1].reshape(sw, PAGE_SIZE).astype(BF16)
            tile(_dot(qbd_ref[...].astype(BF16), kt), lambda a: _dot_nt(a, vt), None)

    @pl.when(j == pl.num_programs(1) - 1)
    def _():
        out = None
        for h in range(SB_HEADS):
            part = jnp.where(lane_head == h, acc_ref[h * tq:(h + 1) * tq, :], 0.0)
            out = part if out is None else out + part
        o_ref[...] = out


SB_PAGES_PER_STEP = 4


def _sb_paged(z3, pool_t, layer, pt_flat, n_pages):
    B, tq, _ = z3.shape
    sw = SB_HEADS * HEAD_DIM
    pps = math.gcd(n_pages, SB_PAGES_PER_STEP)
    n_steps = n_pages // pps
    assert tq & (tq - 1) == 0 and tq <= KEY_TILE
    kern = functools.partial(_sb_paged_kernel, tq=tq, pages_per_step=pps)
    page_specs = [_page_spec((None, None, 2, SB_HEADS, HEAD_DIM, PAGE_SIZE), layer, n_pages,
                             (lambda j, i=i: (n_steps - 1 - j) * pps + i), 0, 0, 0, 0) for i in range(pps)]
    grid_spec = pltpu.PrefetchScalarGridSpec(
        num_scalar_prefetch=1, grid=(B, n_steps),
        in_specs=[pl.BlockSpec((None, tq, sw), lambda b, j, pt: (b, 0, _cb("sbq"))),
                  pl.BlockSpec((None, tq, 2 * sw), lambda b, j, pt: (b, 0, _cb("sbkv")))] + page_specs,
        out_specs=pl.BlockSpec((None, tq, sw), lambda b, j, pt: (b, 0, 0)),
        scratch_shapes=[pltpu.VMEM((SB_HEADS * tq, sw), F32), pltpu.VMEM((SB_HEADS * tq, 1), F32),
                        pltpu.VMEM((SB_HEADS * tq, sw), F32)])
    return pl.pallas_call(
        kern, grid_spec=grid_spec,
        out_shape=jax.ShapeDtypeStruct((B, tq, MIX_W), F32),
        compiler_params=_cparams("parallel", "arbitrary"), name="sb_paged")(pt_flat, z3, z3, *([pool_t] * pps))


def _sel_paged_kernel(pt_ref, q_ref, sel_ref, new_ref, near_ref, ng_ref, emat_ref, *refs, tq, n_pages):
    pages = refs[:n_pages]
    o_ref, qbd_ref, acc_ref = refs[n_pages:]
    _build_qbd(q_ref, qbd_ref, tq)
    qb = qbd_ref[...].astype(BF16)
    sel_b = sel_ref[...].astype(BF16)
    k_new = _new_tile(new_ref, slice(0, NSA_KV_W))
    v_new = _new_tile(new_ref, slice(NSA_KV_W, 2 * NSA_KV_W))
    n_tiles = n_pages + 1
    logits = []
    for t in range(n_tiles):
        hit = [_dot(sel_b, emat_ref[g, :, t * KEY_TILE:(t + 1) * KEY_TILE]) for g in range(NSA_KV_HEADS)]
        open_ = jnp.concatenate([hit[p % 2] for p in range(NSA_HEADS)], axis=0)
        if t < n_pages:
            s = _dot(qb, pages[t][0].reshape(NSA_KV_W, PAGE_SIZE).astype(BF16))
        else:
            s = _dot_nt(qb, k_new)
        s = s + (open_ - 1.0) * (-NEG)
        if t >= n_pages - 1:
            s = s + near_ref[:, (t - n_pages + 1) * KEY_TILE:(t - n_pages + 2) * KEY_TILE]
        logits.append(s)
    lane_max = logits[0]
    for s in logits[1:]:
        lane_max = jnp.maximum(lane_max, s)
    row_max = jnp.max(lane_max, -1, keepdims=True)
    lane_sum = None
    acc = None
    for t, s in enumerate(logits):
        pe = jnp.exp(s - row_max)
        lane_sum = pe if lane_sum is None else lane_sum + pe
        pb = pe.astype(BF16)
        part = _dot_nt(pb, pages[t][1].reshape(NSA_KV_W, PAGE_SIZE).astype(BF16)) if t < n_pages else _dot(pb, v_new)
        acc = part if acc is None else acc + part
    acc_ref[...] = acc / jnp.sum(lane_sum, -1, keepdims=True)
    _assemble_heads(acc_ref, ng_ref, o_ref, tq, 1)


def _sel_paged(z3, sel, pool_t, layer, pt_flat, n_pages, near_tab):
    B, tq, _ = z3.shape
    R = NSA_HEADS * tq
    n_keys = (n_pages + 1) * KEY_TILE
    assert 2 * (n_pages + 1) <= SEL_LANES and tq <= KEY_TILE
    slot = jnp.arange(LANES)[:, None]
    blk = jnp.arange(n_keys)[None, :] // SEL_BLOCK
    emat = jnp.stack([slot == blk + g * SEL_LANES for g in range(NSA_KV_HEADS)]).astype(BF16)
    kern = functools.partial(_sel_paged_kernel, tq=tq, n_pages=n_pages)
    page_specs = [_page_spec((None, None, 2, NSA_KV_HEADS, HEAD_DIM, PAGE_SIZE), layer, n_pages,
                             (lambda p=p: p), 0, 0, 0, 0) for p in range(n_pages)]
    grid_spec = pltpu.PrefetchScalarGridSpec(
        num_scalar_prefetch=1, grid=(B,),
        in_specs=[pl.BlockSpec((None, tq, MIX_W), lambda b, pt: (b, 0, _cb("nq"))),
                  pl.BlockSpec((None, tq, LANES), lambda b, pt: (b, 0, 0)),
                  pl.BlockSpec((None, tq, 2 * NSA_KV_W), lambda b, pt: (b, 0, _cb("kvs"))),
                  pl.BlockSpec((R, 2 * KEY_TILE), lambda b, pt: (0, 0)),
                  pl.BlockSpec((None, tq, LANES), lambda b, pt: (b, 0, _cb("ng"))),
                  pl.BlockSpec((NSA_KV_HEADS, LANES, n_keys), lambda b, pt: (0, 0, 0))] + page_specs,
        out_specs=pl.BlockSpec((None, tq, MIX_W), lambda b, pt: (b, 0, 0)),
        scratch_shapes=_nsa_scratch(tq))
    return pl.pallas_call(
        kern, grid_spec=grid_spec,
        out_shape=jax.ShapeDtypeStruct((B, tq, MIX_W), F32),
        compiler_params=_cparams("parallel"), name="nsa_slc_paged")(
            pt_flat, z3, sel, z3, near_tab, z3, emat, *([pool_t] * n_pages))


def _win_state_kernel(q_ref, state_ref, new_ref, bias_ref, ng_ref, o_ref, qbd_ref, acc_ref, *, tq):
    _build_qbd(q_ref, qbd_ref, tq)
    qb = qbd_ref[...].astype(BF16)
    kt = state_ref[0].reshape(NSA_KV_W, WINDOW).astype(BF16)
    vt = state_ref[1].reshape(NSA_KV_W, WINDOW).astype(BF16)
    k_new = _new_tile(new_ref, slice(0, NSA_KV_W))
    v_new = _new_tile(new_ref, slice(NSA_KV_W, 2 * NSA_KV_W))
    bias = bias_ref[...]
    s = jnp.concatenate([_dot(qb, kt), _dot_nt(qb, k_new)], axis=1) + bias
    pr = _masked_softmax(s, bias > 0.1 * NEG).astype(BF16)
    acc_ref[...] = _dot_nt(pr[:, 0:WINDOW], vt) + _dot(pr[:, WINDOW:WINDOW + KEY_TILE], v_new)
    _assemble_heads(acc_ref, ng_ref, o_ref, tq, 2)


def _win_state(z3, state_t, layer, win_tab):
    B, tq, _ = z3.shape
    R = NSA_HEADS * tq
    assert WIN_KEYS == WINDOW + KEY_TILE and tq <= KEY_TILE
    kern = functools.partial(_win_state_kernel, tq=tq)
    return pl.pallas_call(
        kern, grid=(B,),
        in_specs=[pl.BlockSpec((None, tq, MIX_W), lambda b: (b, 0, _cb("nq"))),
                  pl.BlockSpec((None, None, 2, NSA_KV_HEADS, HEAD_DIM, WINDOW), lambda b: (layer, b, 0, 0, 0, 0)),
                  pl.BlockSpec((None, tq, 2 * NSA_KV_W), lambda b: (b, 0, _cb("kvw"))),
                  pl.BlockSpec((R, WIN_KEYS), lambda b: (0, 0)),
                  pl.BlockSpec((None, tq, LANES), lambda b: (b, 0, _cb("ng")))],
        out_specs=pl.BlockSpec((None, tq, MIX_W), lambda b: (b, 0, 0)),
        out_shape=jax.ShapeDtypeStruct((B, tq, MIX_W), F32),
        scratch_shapes=_nsa_scratch(tq),
        compiler_params=_cparams("parallel"), name="nsa_win_state")(z3, state_t, z3, win_tab, z3)


def _prep_in_proj(w):
    offs = np.concatenate([[0], np.cumsum(IN_WIDTHS)])
    lead = w.shape[:-1]

    def seg(i):
        return w[..., int(offs[i]):int(offs[i + 1])]

    def pad(x, width):
        return jnp.pad(x, [(0, 0)] * (x.ndim - 1) + [(0, width - x.shape[-1])])

    nq = seg(5).reshape(*lead, 2, 4, HEAD_DIM).swapaxes(-3, -2).reshape(*lead, MIX_W)
    ng = seg(12).reshape(*lead, 3, 2, 4).swapaxes(-2, -1).reshape(*lead, 3 * NSA_HEADS)
    out = jnp.concatenate([seg(16), seg(14), seg(15), seg(2), seg(3), nq, seg(13), seg(0), seg(1),
                           seg(6), seg(7), seg(8), seg(9), seg(10), seg(11), pad(seg(4), LANES), pad(ng, LANES)], -1)
    assert out.shape[-1] == ZW
    return out


def _t5_bucket(dist):
    n = jnp.maximum(dist, 0)
    exact = REL_BUCKETS // 2
    scaled = jnp.log(jnp.maximum(n, 1).astype(F32) / exact) / math.log(REL_MAX_DIST / exact)
    large = jnp.minimum(exact + (scaled * (REL_BUCKETS - exact)).astype(jnp.int32), REL_BUCKETS - 1)
    return jnp.where(n < exact, n, large)


def _bias_rows(rel_slots, dist, valid, tq):
    T, S = dist.shape
    bucket = _t5_bucket(dist).reshape(T // tq, 1, tq, S)
    tab = jnp.zeros((T // tq, NSA_HEADS, tq, S), F32)
    for k in range(REL_BUCKETS):
        tab = jnp.where(bucket == k, rel_slots[k].reshape(1, NSA_HEADS, 1, 1), tab)
    tab = jnp.where(valid.reshape(T // tq, 1, tq, S), tab, NEG)
    return tab.reshape(T // tq, NSA_HEADS * tq, S)


def _group_tables(rel_slots, T, tq, pos0, n_cmp_pad):
    pos_q = pos0 + np.arange(T)
    end = np.arange(n_cmp_pad) * CMP_STRIDE + CMP_BLOCK - 1
    d_cmp = jnp.asarray(pos_q[:, None] - end[None, :], jnp.int32)
    cmp_tab = _bias_rows(rel_slots, d_cmp, d_cmp >= 0, tq)
    t = np.arange(tq)
    d_near = jnp.asarray(KEY_TILE + t[:, None] - np.arange(2 * KEY_TILE)[None, :], jnp.int32)
    far = jnp.repeat(rel_slots[REL_BUCKETS - 1], tq)[:, None]
    near_tab = _bias_rows(rel_slots, d_near, d_near >= 0, tq)[0] - far
    d_win = jnp.asarray(WINDOW + t[:, None] - np.arange(WIN_KEYS)[None, :], jnp.int32)
    win_tab = _bias_rows(rel_slots, d_win, (d_win >= 0) & (d_win <= WINDOW), tq)[0]
    return cmp_tab, near_tab, win_tab


def _cover(n_cmp_pad):
    start = np.arange(n_cmp_pad) * CMP_STRIDE
    blk = np.arange(SEL_LANES) * SEL_BLOCK
    c = ((start[:, None] < blk[None, :] + SEL_BLOCK) & (start[:, None] + CMP_BLOCK > blk[None, :])).astype(np.float32)
    out = np.zeros((2, n_cmp_pad, LANES), np.float32)
    out[0, :, :SEL_LANES] = c
    out[1, :, SEL_LANES:] = c
    return jnp.asarray(out)


def kernel(x_prompt, x_sample, cache_cmp_kv, cache_slc_kv, cache_sb_kv, state_win_kv, state_gla, page_table,
           rel_bias, w_in, b_in, w_gla_a2, b_gla_a2, gla_norm_g, w_cmp1, b_cmp1, w_cmp2, cmp_pe,
           w_branch, w_o, ln1_g, ln1_b, w_ff_gate, w_ff_up, w_ff_down, ln2_g, ln2_b):
    depth = w_in.shape[0]
    B, T, _ = x_prompt.shape
    DB, DT, _ = x_sample.shape
    n_pages = page_table.shape[1]
    past = n_pages * PAGE_SIZE
    n_win_state = state_win_kv.shape[2]
    assert T % Q_BLK == 0 and DT % SUBLANES == 0 and DT <= Q_BLK and n_win_state == WINDOW and T >= WINDOW

    w_all = _prep_in_proj(w_in).astype(BF16)
    b_all = _prep_in_proj(b_in)[:, None, :]
    wa = jnp.pad(w_gla_a2, ((0, 0), (0, LANES - GLA_GATE_RANK), (0, 0)))
    eye2 = jnp.eye(2, dtype=F32)
    w1bd = jnp.einsum("gG,ljcdh->ljcgdGh", eye2, w_cmp1).reshape(
        depth, CMP_BLOCK, 2, NSA_KV_W, 2 * CMP_HIDDEN).astype(BF16)
    w2bd = jnp.einsum("gG,lchd->lcghGd", eye2, w_cmp2).reshape(depth, 2, 2 * CMP_HIDDEN, NSA_KV_W).astype(BF16)
    pe2 = jnp.broadcast_to(cmp_pe.swapaxes(1, 2)[:, :, :, None, :], (depth, 2, CMP_BLOCK, 2, HEAD_DIM)).reshape(
        depth, 2, CMP_BLOCK, NSA_KV_W)
    b1t = jnp.broadcast_to(b_cmp1[:, :, None, None, :], (depth, 2, 1, 2, CMP_HIDDEN)).reshape(
        depth, 2, 1, 2 * CMP_HIDDEN)
    wb_nsa = w_branch[:, 1].reshape(depth, 2, 4, HEAD_DIM, D_MODEL).swapaxes(1, 2).reshape(depth, MIX_W, D_MODEL)
    wb = jnp.stack([w_branch[:, 0], wb_nsa, w_branch[:, 2]], 1).astype(BF16)
    wo = w_o.astype(BF16)
    wg, wu, wd = w_ff_gate.astype(BF16), w_ff_up.astype(BF16), w_ff_down.astype(BF16)
    rel_slots = rel_bias.reshape(REL_BUCKETS, 2, 4).swapaxes(1, 2).reshape(REL_BUCKETS, NSA_HEADS)

    n_seg_p = T // CMP_STRIDE
    n_seg_s = past // CMP_STRIDE
    tabs_p = _group_tables(rel_slots, T, Q_BLK, 0, n_seg_p)
    tabs_s = _group_tables(rel_slots, DT, DT, past, n_seg_s)
    cover_p, cover_s = _cover(n_seg_p), _cover(n_seg_s)
    pt_flat = page_table.reshape(-1).astype(jnp.int32)
    cmp_pool, slc_pool, sb_pool, win_state_t = (a.transpose(0, 1, 3, 4, 5, 2) for a in
                                                (cache_cmp_kv, cache_slc_kv, cache_sb_kv, state_win_kv))

    xp = x_prompt.reshape(B * T, D_MODEL)
    xs = x_sample.reshape(DB * DT, D_MODEL)
    zero_state = jnp.zeros((B, GLA_HEADS, GLA_DV, GLA_DK), F32)
    outs_p, outs_s = [], []

    def dense_tail(o, z, x, l):
        n = x.shape[0]
        o_gla, o_cmp, o_slc, o_win, o_sb = (a.reshape(n, MIX_W) for a in o)
        x1 = _merge(o_gla, o_cmp, o_slc, o_win, o_sb, z, x, wb[l], wo[l], ln1_g[l][None], ln1_b[l][None])
        return _ffn(x1, wg[l], wu[l], wd[l], ln2_g[l][None], ln2_b[l][None])

    def col(z3, name):
        off, w = COL[name]
        return z3[..., off:off + w]

    for l in range(depth):
        cw = (w1bd[l], pe2[l], b1t[l], w2bd[l])

        z = _linear(xp, w_all[l], b_all[l])
        z3 = z.reshape(B, T, ZW)
        kvc = _compress(z3, _cb("kvc"), T, *cw)
        cmp_tab, near_tab, win_tab = tabs_p
        o_gla, st = _gla(z3, zero_state, wa[l], b_gla_a2[l][None], gla_norm_g[l][None])
        o_cmp, sel = _cmp_attend(z3, kvc, cmp_tab, cover_p, Q_BLK, 0)
        o_slc = _sel_attend(z3, sel, z3, _cb("kvs"), T, near_tab, Q_BLK, 0)
        win_keys = jnp.pad(col(z3, "kvw"), ((0, 0), (WINDOW, 0), (0, 0)))
        o_win = _win_attend(z3, win_keys, win_tab, Q_BLK, WINDOW)
        o_sb = _sb_attend(z3, z3, _cb("sbkv", SB_PROMPT_HEADS * HEAD_DIM), T, Q_BLK, SB_PROMPT_HEADS, 0)
        xp = dense_tail((o_gla, o_cmp, o_slc, o_win, o_sb), z, xp, l)
        outs_p.append((col(z3, "kvc"), col(z3, "kvs"), col(z3, "sbkv"), col(z3, "kvw")[:, T - WINDOW:],
                       st.swapaxes(-1, -2)))

        z = _linear(xs, w_all[l], b_all[l])
        z3 = z.reshape(DB, DT, ZW)
        cmp_tab, near_tab, win_tab = tabs_s
        o_gla, st = _gla(z3, state_gla[l].swapaxes(-1, -2), wa[l], b_gla_a2[l][None], gla_norm_g[l][None])
        kvc = _compress_paged(cmp_pool, l, pt_flat, n_pages, *cw)
        o_cmp, sel = _cmp_attend(z3, kvc, cmp_tab, cover_s, DT, past)
        o_slc = _sel_paged(z3, sel, slc_pool, l, pt_flat, n_pages, near_tab)
        o_win = _win_state(z3, win_state_t, l, win_tab)
        o_sb = _sb_paged(z3, sb_pool, l, pt_flat, n_pages)
        xs = dense_tail((o_gla, o_cmp, o_slc, o_win, o_sb), z, xs, l)
        win_all = jnp.concatenate([state_win_kv[l].reshape(DB, n_win_state, 2 * NSA_KV_W), col(z3, "kvw")], 1)
        outs_s.append((col(z3, "kvc"), col(z3, "kvs"), col(z3, "sbkv"), win_all[:, win_all.shape[1] - WINDOW:],
                       st.swapaxes(-1, -2)))

    def stacked(outs, i, tail):
        a = jnp.stack([o[i] for o in outs])
        return a.reshape(*a.shape[:3], *tail)

    kv_tail = (2, NSA_KV_HEADS, HEAD_DIM)
    sb_tail = (2, SB_HEADS, HEAD_DIM)
    return (xp.reshape(B, T, D_MODEL), xs.reshape(DB, DT, D_MODEL),
            stacked(outs_p, 0, kv_tail), stacked(outs_s, 0, kv_tail),
            stacked(outs_p, 1, kv_tail), stacked(outs_s, 1, kv_tail),
            stacked(outs_p, 2, sb_tail), stacked(outs_s, 2, sb_tail),
            stacked(outs_p, 3, kv_tail), stacked(outs_s, 3, kv_tail),
            jnp.stack([o[4] for o in outs_p]), jnp.stack([o[4] for o in outs_s]))
```

```python
import functools
import math

import jax
import jax.numpy as jnp
import numpy as np
from jax import lax
from jax.experimental import pallas as pl
from jax.experimental.pallas import tpu as pltpu

F32, BF16 = jnp.float32, jnp.bfloat16
HI = lax.Precision.HIGHEST

D_MODEL = 1024
HEAD_DIM = 64
MIX_W = D_MODEL // 2
GLA_HEADS = 4
GLA_DV = MIX_W // GLA_HEADS
GLA_DK = GLA_DV // 2
GLA_GATE_RANK = 16
GLA_TAU = 16.0
GLA_CHUNK = 64
GLA_SUB = 16
NSA_HEADS = MIX_W // HEAD_DIM
NSA_KV_HEADS = 2
NSA_KV_W = NSA_KV_HEADS * HEAD_DIM
CMP_STRIDE = 16
CMP_BLOCK = 32
CMP_HIDDEN = 128
SEL_BLOCK = 64
SEL_TOPN = 16
WINDOW = 512
FORCE_BONUS = 1e4
SB_HEADS = MIX_W // HEAD_DIM
REL_BUCKETS = 32
REL_MAX_DIST = 128
D_FF = -(-8 * D_MODEL // (3 * 256)) * 256
Q_BLK = 128
LN_EPS = 1e-5
TRUNK_DEPTH = 4
DN_ALPHA = (2 * TRUNK_DEPTH) ** 0.25
PAGE_SIZE = 128
GLA_QK_W = GLA_HEADS * GLA_DK
IN_WIDTHS = (GLA_QK_W, GLA_QK_W, MIX_W, MIX_W, GLA_GATE_RANK,
             MIX_W, NSA_KV_W, NSA_KV_W, NSA_KV_W, NSA_KV_W, NSA_KV_W, NSA_KV_W, 3 * NSA_HEADS,
             MIX_W, MIX_W, MIX_W,
             3 * D_MODEL)

LANES = 128
SUBLANES = 8
KEY_TILE = 128
VMEM_LIMIT = 56 * 1024 * 1024
NEG = -1e30
SEL_LANES = 64
SB_UNDERFLOW = -104.0
SB_PROMPT_HEADS = 4
GLA_SAFE_DECAY = 60.0

COL = dict(mg=(0, 3072), sbkv=(3072, 1024), gv=(4096, 512), gr=(4608, 512), nq=(5120, 512),
           sbq=(5632, 512), gqk=(6144, 512), kvc=(6656, 256), kvs=(6912, 256), kvw=(7168, 256),
           ga=(7424, 128), ng=(7552, 128))
ZW = 7680


def _cb(name, width=None):
    off, w = COL[name]
    width = width or w
    assert off % width == 0
    return off // width


def _cparams(*sem):
    return pltpu.CompilerParams(dimension_semantics=sem, vmem_limit_bytes=VMEM_LIMIT)


def _dot(a, b):
    return jnp.dot(a, b, preferred_element_type=F32)


def _dot_hi(a, b):
    return jnp.dot(a, b, precision=HI, preferred_element_type=F32)


def _dot_nt(a, b):
    return lax.dot_general(a, b, (((1,), (1,)), ((), ())), preferred_element_type=F32)


def _dot_tn(a, b):
    return lax.dot_general(a, b, (((0,), (0,)), ((), ())), preferred_element_type=F32)


def _log_sigmoid(x):
    return jnp.minimum(x, 0.0) - jnp.log(1.0 + jnp.exp(-jnp.abs(x)))


def _iota(shape, dim):
    return lax.broadcasted_iota(jnp.int32, shape, dim)


def _pick_tile(n, cands=(1024, 512, 256, 128, 64, 32, 16, 8)):
    for c in cands:
        if n % c == 0:
            return c
    raise ValueError(f"no row tile divides {n}")


def _linear_kernel(x_ref, w_ref, b_ref, o_ref, xb_ref):
    @pl.when(pl.program_id(1) == 0)
    def _():
        xb_ref[...] = x_ref[...].astype(BF16)

    o_ref[...] = _dot(xb_ref[...], w_ref[...]) + b_ref[...]


def _linear(x, w, b):
    n, k = x.shape
    dout = w.shape[1]
    tm, tn = _pick_tile(n), 512
    return pl.pallas_call(
        _linear_kernel, grid=(n // tm, dout // tn),
        in_specs=[pl.BlockSpec((tm, k), lambda i, j: (i, 0)),
                  pl.BlockSpec((k, tn), lambda i, j: (0, j)),
                  pl.BlockSpec((1, tn), lambda i, j: (0, j))],
        out_specs=pl.BlockSpec((tm, tn), lambda i, j: (i, j)),
        out_shape=jax.ShapeDtypeStruct((n, dout), F32),
        scratch_shapes=[pltpu.VMEM((tm, k), BF16)],
        compiler_params=_cparams("parallel", "arbitrary"), name="in_proj")(x, w, b)


def _layer_norm(h, g, b):
    mu = jnp.mean(h, -1, keepdims=True)
    d = h - mu
    var = jnp.mean(d * d, -1, keepdims=True)
    return d * lax.rsqrt(var + LN_EPS) * g + b


def _merge_kernel(og_ref, oc_ref, os_ref, ow_ref, ob_ref, mg_ref, x_ref, wb_ref, wo_ref, g_ref, b_ref, o_ref):
    o_nsa = oc_ref[...] + os_ref[...] + ow_ref[...]
    mixed = None
    for br, o in enumerate((og_ref[...], o_nsa, ob_ref[...])):
        proj = _dot(o.astype(BF16), wb_ref[br])
        term = jax.nn.sigmoid(mg_ref[:, br * D_MODEL:(br + 1) * D_MODEL]) * proj
        mixed = term if mixed is None else mixed + term
    y = _dot(mixed.astype(BF16), wo_ref[...])
    o_ref[...] = _layer_norm(DN_ALPHA * x_ref[...] + y, g_ref[...], b_ref[...])


def _merge(o_gla, o_cmp, o_slc, o_win, o_sb, z, x, wb, wo, g, b):
    n = x.shape[0]
    tm = _pick_tile(n, (512, 256, 128, 64, 32, 16, 8))
    mix = pl.BlockSpec((tm, MIX_W), lambda i: (i, 0))
    return pl.pallas_call(
        _merge_kernel, grid=(n // tm,),
        in_specs=[mix, mix, mix, mix, mix,
                  pl.BlockSpec((tm, 3 * D_MODEL), lambda i: (i, _cb("mg"))),
                  pl.BlockSpec((tm, D_MODEL), lambda i: (i, 0)),
                  pl.BlockSpec((3, MIX_W, D_MODEL), lambda i: (0, 0, 0)),
                  pl.BlockSpec((D_MODEL, D_MODEL), lambda i: (0, 0)),
                  pl.BlockSpec((1, D_MODEL), lambda i: (0, 0)),
                  pl.BlockSpec((1, D_MODEL), lambda i: (0, 0))],
        out_specs=pl.BlockSpec((tm, D_MODEL), lambda i: (i, 0)),
        out_shape=jax.ShapeDtypeStruct((n, D_MODEL), F32),
        compiler_params=_cparams("parallel"), name="merge")(o_gla, o_cmp, o_slc, o_win, o_sb, z, x, wb, wo, g, b)


FF_TILE = 256


def _ffn_kernel(x_ref, wg_ref, wu_ref, wd_ref, g_ref, b_ref, o_ref, xb_ref, acc_ref):
    f = pl.program_id(1)

    @pl.when(f == 0)
    def _():
        xb_ref[...] = x_ref[...].astype(BF16)
        acc_ref[...] = jnp.zeros_like(acc_ref)

    xb = xb_ref[...]
    gate = _dot(xb, wg_ref[...])
    up = _dot(xb, wu_ref[...])
    h = gate * jax.nn.sigmoid(gate) * up
    acc_ref[...] += _dot(h.astype(BF16), wd_ref[...])

    @pl.when(f == pl.num_programs(1) - 1)
    def _():
        o_ref[...] = _layer_norm(DN_ALPHA * x_ref[...] + acc_ref[...], g_ref[...], b_ref[...])


def _ffn(x, wg, wu, wd, g, b):
    n = x.shape[0]
    tm = _pick_tile(n)
    return pl.pallas_call(
        _ffn_kernel, grid=(n // tm, D_FF // FF_TILE),
        in_specs=[pl.BlockSpec((tm, D_MODEL), lambda i, f: (i, 0)),
                  pl.BlockSpec((D_MODEL, FF_TILE), lambda i, f: (0, f)),
                  pl.BlockSpec((D_MODEL, FF_TILE), lambda i, f: (0, f)),
                  pl.BlockSpec((FF_TILE, D_MODEL), lambda i, f: (f, 0)),
                  pl.BlockSpec((1, D_MODEL), lambda i, f: (0, 0)),
                  pl.BlockSpec((1, D_MODEL), lambda i, f: (0, 0))],
        out_specs=pl.BlockSpec((tm, D_MODEL), lambda i, f: (i, 0)),
        out_shape=jax.ShapeDtypeStruct((n, D_MODEL), F32),
        scratch_shapes=[pltpu.VMEM((tm, D_MODEL), BF16), pltpu.VMEM((tm, D_MODEL), F32)],
        compiler_params=_cparams("parallel", "arbitrary"), name="ffn")(x, wg, wu, wd, g, b)


def _gla_kernel(qk_ref, v_ref, r_ref, ga_ref, s0_ref, wa_ref, ba_ref, ng_ref, o_ref, st_ref, *, chunk, n_chunks):
    C = chunk

    @pl.when(pl.program_id(1) == 0)
    def _():
        st_ref[...] = s0_ref[...]

    sub = min(C, GLA_SUB)
    n_sub = C // sub
    causal = _iota((C, C), 1) <= _iota((C, C), 0)
    tril = causal.astype(F32)
    row_sub = _iota((sub, GLA_DK), 0)

    def intra_factored(qh, kh, vh, ch):
        qd = qh * jnp.exp(ch)
        kd = kh * jnp.exp(-ch)
        att = jnp.where(causal, _dot_nt(qd.astype(BF16), kd.astype(BF16)), 0.0)
        return _dot(att.astype(BF16), vh.astype(BF16))

    def intra_blocked(qh, kh, vh, ch):
        outs = []
        for sb in range(n_sub):
            a0 = sb * sub
            qs, ks, vs, cs = qh[a0:a0 + sub], kh[a0:a0 + sub], vh[a0:a0 + sub], ch[a0:a0 + sub]
            oi = jnp.zeros((sub, GLA_DV), F32)
            if sb > 0:
                cref = ch[a0 - 1:a0]
                qd = qs * jnp.exp(cs - cref)
                kd = kh[:a0] * jnp.exp(cref - ch[:a0])
                att = _dot_nt(qd.astype(BF16), kd.astype(BF16))
                oi = oi + _dot(att.astype(BF16), vh[:a0].astype(BF16))
            for s in range(sub):
                e = jnp.exp(jnp.minimum(cs - cs[s:s + 1], 0.0))
                w = jnp.sum(jnp.where(row_sub >= s, qs * ks[s:s + 1] * e, 0.0), axis=-1, keepdims=True)
                oi = oi + w * vs[s:s + 1]
            outs.append(oi)
        return outs[0] if n_sub == 1 else jnp.concatenate(outs, axis=0)

    def chunk_body(c, carry):
        r0 = pl.multiple_of(c * C, C)
        rows = pl.ds(r0, C)
        qk = qk_ref[rows, :]
        v = v_ref[rows, :]
        log_a = _log_sigmoid(_dot_hi(ga_ref[rows, :], wa_ref[...]) + ba_ref[...]) * (1.0 / GLA_TAU)
        cum = _dot_hi(tril, log_a)
        parts = []
        for h in range(GLA_HEADS):
            parts.append((qk[:, h * GLA_DK:(h + 1) * GLA_DK] * (GLA_DK ** -0.5),
                          qk[:, GLA_QK_W + h * GLA_DK:GLA_QK_W + (h + 1) * GLA_DK],
                          v[:, h * GLA_DV:(h + 1) * GLA_DV],
                          cum[:, h * GLA_DK:(h + 1) * GLA_DK]))
        small_decay = jnp.max(-cum[C - 1:C, :]) <= GLA_SAFE_DECAY
        intra = lax.cond(small_decay,
                         lambda: tuple(intra_factored(*p) for p in parts),
                         lambda: tuple(intra_blocked(*p) for p in parts))
        for h in range(GLA_HEADS):
            qh, kh, vh, ch = parts[h]
            st = st_ref[h]
            o = intra[h] + _dot_nt((qh * jnp.exp(ch)).astype(BF16), st.astype(BF16))
            mu = jnp.mean(o, -1, keepdims=True)
            d = o - mu
            var = jnp.mean(d * d, -1, keepdims=True)
            lanes = slice(h * GLA_DV, (h + 1) * GLA_DV)
            rr = r_ref[rows, lanes]
            o_ref[rows, lanes] = d * lax.rsqrt(var + LN_EPS) * ng_ref[:, lanes] * (rr * jax.nn.sigmoid(rr))
            last = ch[C - 1:C]
            kdl = kh * jnp.exp(last - ch)
            st_ref[h] = st * jnp.exp(last) + _dot_tn(vh.astype(BF16), kdl.astype(BF16))
        return carry

    lax.fori_loop(0, n_chunks, chunk_body, 0, unroll=2 if n_chunks % 2 == 0 else 1)


def _gla(z3, s0t, wa, ba, norm_g):
    B, T, _ = z3.shape
    C = math.gcd(T, GLA_CHUNK)
    tt = min(T, 512)
    assert T % tt == 0 and tt % C == 0
    kern = functools.partial(_gla_kernel, chunk=C, n_chunks=tt // C)
    st_spec = pl.BlockSpec((None, GLA_HEADS, GLA_DV, GLA_DK), lambda b, i: (b, 0, 0, 0))
    return pl.pallas_call(
        kern, grid=(B, T // tt),
        in_specs=[pl.BlockSpec((None, tt, 512), lambda b, i: (b, i, _cb("gqk"))),
                  pl.BlockSpec((None, tt, 512), lambda b, i: (b, i, _cb("gv"))),
                  pl.BlockSpec((None, tt, 512), lambda b, i: (b, i, _cb("gr"))),
                  pl.BlockSpec((None, tt, LANES), lambda b, i: (b, i, _cb("ga"))),
                  st_spec,
                  pl.BlockSpec((LANES, GLA_QK_W), lambda b, i: (0, 0)),
                  pl.BlockSpec((1, GLA_QK_W), lambda b, i: (0, 0)),
                  pl.BlockSpec((1, MIX_W), lambda b, i: (0, 0))],
        out_specs=[pl.BlockSpec((None, tt, MIX_W), lambda b, i: (b, i, 0)), st_spec],
        out_shape=[jax.ShapeDtypeStruct((B, T, MIX_W), F32),
                   jax.ShapeDtypeStruct((B, GLA_HEADS, GLA_DV, GLA_DK), F32)],
        compiler_params=_cparams("parallel", "arbitrary"), name="gla")(z3, z3, z3, z3, s0t, wa, ba, norm_g)


def _compress_kernel(rows_ref, w1_ref, pe_ref, b1_ref, w2_ref, o_ref, h1_ref, *, n_seg):
    M = n_seg
    xs = [rows_ref[pl.ds(j, M, stride=CMP_STRIDE), :] for j in range(CMP_STRIDE)]
    halves = []
    for r in range(CMP_BLOCK // CMP_STRIDE):
        xr = jnp.concatenate([(xs[j] + pe_ref[r * CMP_STRIDE + j:r * CMP_STRIDE + j + 1, :]).astype(BF16)
                              for j in range(CMP_STRIDE)], axis=1)
        halves.append(_dot(xr, w1_ref[r]))
    h0, h1 = halves
    h1_ref[0:M, :] = h1
    h1_ref[M:M + SUBLANES, :] = jnp.zeros((SUBLANES, 2 * CMP_HIDDEN), F32)
    h = jax.nn.gelu(b1_ref[...] + h0 + h1_ref[pl.ds(1, M), :])
    o_ref[...] = _dot(h.astype(BF16), w2_ref[...])


def _compress(rows3, colblk, n_rows, w1bd, pe2, b1t, w2bd):
    B = rows3.shape[0]
    n_seg = n_rows // CMP_STRIDE
    kern = functools.partial(_compress_kernel, n_seg=n_seg)
    return pl.pallas_call(
        kern, grid=(B, 2),
        in_specs=[pl.BlockSpec((None, n_rows, NSA_KV_W), lambda b, c: (b, 0, 2 * colblk + c)),
                  pl.BlockSpec((None, 2, CMP_STRIDE * NSA_KV_W, 2 * CMP_HIDDEN), lambda b, c: (c, 0, 0, 0)),
                  pl.BlockSpec((None, CMP_BLOCK, NSA_KV_W), lambda b, c: (c, 0, 0)),
                  pl.BlockSpec((None, 1, 2 * CMP_HIDDEN), lambda b, c: (c, 0, 0)),
                  pl.BlockSpec((None, 2 * CMP_HIDDEN, NSA_KV_W), lambda b, c: (c, 0, 0))],
        out_specs=pl.BlockSpec((None, n_seg, NSA_KV_W), lambda b, c: (b, 0, c)),
        out_shape=jax.ShapeDtypeStruct((B, n_seg, 2 * NSA_KV_W), F32),
        scratch_shapes=[pltpu.VMEM((n_seg + SUBLANES, 2 * CMP_HIDDEN), F32)],
        compiler_params=_cparams("parallel", "parallel"), name="compress")(rows3, w1bd, pe2, b1t, w2bd)


def _compress_paged_kernel(pt_ref, *refs, n_pages):
    pages = refs[:n_pages]
    w1_ref, pe_ref, b1_ref, w2_ref, o_ref, h1_ref, rows_ref = refs[n_pages:]
    for p, page in enumerate(pages):
        rows_ref[p * PAGE_SIZE:(p + 1) * PAGE_SIZE, :] = page[...].reshape(NSA_KV_W, PAGE_SIZE).T
    _compress_kernel(rows_ref, w1_ref, pe_ref, b1_ref, w2_ref, o_ref, h1_ref, n_seg=n_pages * PAGE_SIZE // CMP_STRIDE)


def _page_spec(block, layer, n_pages, page_of, *rest):
    def index_map(b, *idx_and_pt):
        *idx, pt = idx_and_pt
        return (layer, pt[b * n_pages + page_of(*idx)]) + tuple(r(*idx) if callable(r) else r for r in rest)
    return pl.BlockSpec(block, index_map)


def _compress_paged(pool_t, layer, pt_flat, n_pages, w1bd, pe2, b1t, w2bd):
    B = pt_flat.shape[0] // n_pages
    n_seg = n_pages * PAGE_SIZE // CMP_STRIDE
    kern = functools.partial(_compress_paged_kernel, n_pages=n_pages)
    page_specs = [_page_spec((None, None, None, NSA_KV_HEADS, HEAD_DIM, PAGE_SIZE), layer, n_pages,
                             (lambda c, p=p: p), (lambda c: c), 0, 0, 0) for p in range(n_pages)]
    grid_spec = pltpu.PrefetchScalarGridSpec(
        num_scalar_prefetch=1, grid=(B, 2),
        in_specs=page_specs + [
            pl.BlockSpec((None, 2, CMP_STRIDE * NSA_KV_W, 2 * CMP_HIDDEN), lambda b, c, pt: (c, 0, 0, 0)),
            pl.BlockSpec((None, CMP_BLOCK, NSA_KV_W), lambda b, c, pt: (c, 0, 0)),
            pl.BlockSpec((None, 1, 2 * CMP_HIDDEN), lambda b, c, pt: (c, 0, 0)),
            pl.BlockSpec((None, 2 * CMP_HIDDEN, NSA_KV_W), lambda b, c, pt: (c, 0, 0))],
        out_specs=pl.BlockSpec((None, n_seg, NSA_KV_W), lambda b, c, pt: (b, 0, c)),
        scratch_shapes=[pltpu.VMEM((n_seg + SUBLANES, 2 * CMP_HIDDEN), F32),
                        pltpu.VMEM((n_pages * PAGE_SIZE, NSA_KV_W), F32)])
    return pl.pallas_call(
        kern, grid_spec=grid_spec,
        out_shape=jax.ShapeDtypeStruct((B, n_seg, 2 * NSA_KV_W), F32),
        compiler_params=_cparams("parallel", "parallel"), name="compress_paged")(
            pt_flat, *([pool_t] * n_pages), w1bd, pe2, b1t, w2bd)


def _build_qbd(q_ref, qbd_ref, tq):
    lane = _iota((tq, LANES), 1)
    for p in range(NSA_HEADS):
        slab = q_ref[:, (p // 2) * LANES:(p // 2 + 1) * LANES] * (HEAD_DIM ** -0.5)
        keep = (lane >= HEAD_DIM) if p % 2 else (lane < HEAD_DIM)
        qbd_ref[p * tq:(p + 1) * tq, :] = jnp.where(keep, slab, 0.0)


def _assemble_heads(acc_ref, ng_ref, o_ref, tq, branch):
    lo = _iota((tq, LANES), 1) < HEAD_DIM
    for k in range(NSA_HEADS // 2):
        a0 = acc_ref[(2 * k) * tq:(2 * k + 1) * tq, :]
        a1 = acc_ref[(2 * k + 1) * tq:(2 * k + 2) * tq, :]
        c0 = branch * NSA_HEADS + 2 * k
        g0 = jax.nn.sigmoid(ng_ref[:, c0:c0 + 1])
        g1 = jax.nn.sigmoid(ng_ref[:, c0 + 1:c0 + 2])
        o_ref[:, k * LANES:(k + 1) * LANES] = jnp.where(lo, a0 * g0, a1 * g1)


def _masked_softmax(s, valid):
    s = jnp.where(valid, s, NEG)
    m = jnp.max(s, -1, keepdims=True)
    e = jnp.where(valid, jnp.exp(s - m), 0.0)
    return e / jnp.maximum(jnp.sum(e, -1, keepdims=True), 1e-30)


def _dot_nt_hi(a, b):
    return lax.dot_general(a, b, (((1,), (1,)), ((), ())), precision=HI, preferred_element_type=F32)


def _select_blocks(score, blk, cur, slot_axis):
    forced = (blk == 0) | (blk == cur) | (blk == cur - 1)
    score = jnp.where(forced, score + FORCE_BONUS, score)
    visible = blk <= cur
    score = jnp.where(visible, score, -jnp.inf)
    first_group = _iota(score.shape, slot_axis) < SEL_LANES
    rank = jnp.zeros(score.shape, F32)
    for mp in range(SEL_LANES):
        if slot_axis == 0:
            other = jnp.where(first_group, score[mp:mp + 1, :], score[SEL_LANES + mp:SEL_LANES + mp + 1, :])
        else:
            other = jnp.where(first_group, score[:, mp:mp + 1], score[:, SEL_LANES + mp:SEL_LANES + mp + 1])
        tie = jnp.where(blk > mp, 1.0, 0.0)
        rank = rank + jnp.where(other > score, 1.0, jnp.where(other == score, tie, 0.0))
    return jnp.where(visible, jnp.where(rank < SEL_TOPN, 1.0, 0.0), 0.0)


def _cmp_kernel(q_ref, kvc_ref, bias_ref, covt_ref, ng_ref, o_ref, sel_ref, qbd_ref, acc_ref, *, tq, rc, pos0):
    qi = pl.program_id(1)
    _build_qbd(q_ref, qbd_ref, tq)
    n = kvc_ref.shape[0]
    kc = kvc_ref[:, 0:NSA_KV_W].astype(BF16)
    vc = kvc_ref[:, NSA_KV_W:2 * NSA_KV_W].astype(BF16)
    ones = jnp.ones((n, LANES), BF16)
    heads_per_chunk = rc // tq
    p_sum = [jnp.zeros((tq, n), F32), jnp.zeros((tq, n), F32)]
    for c in range(NSA_HEADS * tq // rc):
        rows = slice(c * rc, (c + 1) * rc)
        s = _dot_nt(qbd_ref[rows, :].astype(BF16), kc) + bias_ref[rows, :]
        lane_max = s[:, 0:LANES]
        for t in range(1, n // LANES):
            lane_max = jnp.maximum(lane_max, s[:, t * LANES:(t + 1) * LANES])
        row_max = jnp.max(lane_max, -1, keepdims=True)
        e = jnp.exp(s - row_max)
        e_hi = e.astype(BF16)
        e_lo = (e - e_hi.astype(F32)).astype(BF16)
        row_sum = _dot(e_hi, ones) + _dot(e_lo, ones)
        scale = jnp.where(row_max > 0.5 * NEG, 1.0 / row_sum, 0.0)
        pr = jnp.concatenate([e[:, t * LANES:(t + 1) * LANES] * scale for t in range(n // LANES)], axis=1)
        acc_ref[rows, :] = _dot(pr.astype(BF16), vc)
        for pp in range(heads_per_chunk):
            g = (c * heads_per_chunk + pp) % 2
            p_sum[g] = p_sum[g] + pr[pp * tq:(pp + 1) * tq]
    _assemble_heads(acc_ref, ng_ref, o_ref, tq, 0)

    shift = int(math.log2(SEL_BLOCK))
    if tq == LANES:
        score_t = _dot_nt_hi(covt_ref[0], p_sum[0]) + _dot_nt_hi(covt_ref[1], p_sum[1])
        blk = jnp.bitwise_and(_iota((LANES, tq), 0), SEL_LANES - 1)
        cur = lax.shift_right_logical(pos0 + qi * tq + _iota((LANES, tq), 1), shift)
        sel_t = _select_blocks(score_t, blk, cur, 0)
        eye = (_iota((LANES, LANES), 0) == _iota((LANES, LANES), 1)).astype(BF16)
        sel_ref[...] = _dot_tn(sel_t.astype(BF16), eye)
    else:
        score = _dot_nt_hi(p_sum[0], covt_ref[0]) + _dot_nt_hi(p_sum[1], covt_ref[1])
        blk = jnp.bitwise_and(_iota((tq, LANES), 1), SEL_LANES - 1)
        cur = lax.shift_right_logical(pos0 + qi * tq + _iota((tq, LANES), 0), shift)
        sel_ref[...] = _select_blocks(score, blk, cur, 1)


def _nsa_common_specs(tq):
    q_spec = pl.BlockSpec((None, tq, MIX_W), lambda b, i: (b, i, _cb("nq")))
    ng_spec = pl.BlockSpec((None, tq, LANES), lambda b, i: (b, i, _cb("ng")))
    o_spec = pl.BlockSpec((None, tq, MIX_W), lambda b, i: (b, i, 0))
    return q_spec, ng_spec, o_spec


def _nsa_scratch(tq):
    return [pltpu.VMEM((NSA_HEADS * tq, LANES), F32), pltpu.VMEM((NSA_HEADS * tq, LANES), F32)]


def _cmp_attend(z3, kvc, bias_tab, cover2, tq, pos0):
    B, T, _ = z3.shape
    n = kvc.shape[1]
    R = NSA_HEADS * tq
    rc = min(R, 128)
    assert n % LANES == 0
    q_spec, ng_spec, o_spec = _nsa_common_specs(tq)
    kern = functools.partial(_cmp_kernel, tq=tq, rc=rc, pos0=pos0)
    return pl.pallas_call(
        kern, grid=(B, T // tq),
        in_specs=[q_spec,
                  pl.BlockSpec((None, n, 2 * NSA_KV_W), lambda b, i: (b, 0, 0)),
                  pl.BlockSpec((None, R, n), lambda b, i: (i, 0, 0)),
                  pl.BlockSpec((2, LANES, n), lambda b, i: (0, 0, 0)),
                  ng_spec],
        out_specs=[o_spec, pl.BlockSpec((None, tq, LANES), lambda b, i: (b, i, 0))],
        out_shape=[jax.ShapeDtypeStruct((B, T, MIX_W), F32), jax.ShapeDtypeStruct((B, T, LANES), F32)],
        scratch_shapes=_nsa_scratch(tq),
        compiler_params=_cparams("parallel", "parallel"), name="nsa_cmp")(z3, kvc, bias_tab, cover2, z3)


SLC_PAIR = 2 * KEY_TILE


def _sel_kernel(q_ref, sel_ref, kv_ref, near_ref, ng_ref, emat_ref, o_ref,
                qbd_ref, acc_ref, mx_ref, selx_ref, selp_ref, *, tq, tile0, n_tiles):
    qi = pl.program_id(1)
    qt = tile0 + (qi * tq) // KEY_TILE
    R = NSA_HEADS * tq
    _build_qbd(q_ref, qbd_ref, tq)
    sel_b = sel_ref[...].astype(BF16)
    for t in range(n_tiles):
        @pl.when(t <= qt)
        def _(t=t):
            for g in range(NSA_KV_HEADS):
                hit = _dot(sel_b, emat_ref[g, :, t * KEY_TILE:(t + 1) * KEY_TILE])
                selx_ref[g, t] = (hit - 1.0) * (-NEG)
                if t // 2 < n_tiles // 2:
                    selp_ref[g, t // 2, :, (t % 2) * KEY_TILE:(t % 2 + 1) * KEY_TILE] = (hit - 1.0) * (-NEG)
    n_far = jnp.maximum(qt - 1, 0)
    n_pair = lax.shift_right_logical(n_far, 1)
    odd_far = jnp.bitwise_and(n_far, 1) == 1
    drop_prev = ((qt >= 1).astype(F32) - 1.0) * (-NEG)
    near_tiles = ((n_far, lambda rows: near_ref[rows, 0:KEY_TILE] + drop_prev),
                  (qt, lambda rows: near_ref[rows, KEY_TILE:2 * KEY_TILE]))

    def head_rows(p):
        return slice(p * tq, (p + 1) * tq)

    def logits(p, kk, sel_add, bias_of):
        s = _dot_nt(qbd_ref[head_rows(p), :].astype(BF16), kk) + sel_add[p % 2]
        return s if bias_of is None else s + bias_of(head_rows(p))

    def load_keys(start, size, with_values):
        rows = pl.ds(pl.multiple_of(start, KEY_TILE), size)
        kk = kv_ref[rows, 0:NSA_KV_W].astype(BF16)
        if not with_values:
            return kk, None
        v = kv_ref[rows, NSA_KV_W:2 * NSA_KV_W]
        lo = _iota(v.shape, 1) < HEAD_DIM
        return kk, (jnp.where(lo, v, 1.0).astype(BF16), jnp.where(lo, 1.0, v).astype(BF16))

    def sweep_max(kk, sel_add, bias_of):
        for p in range(NSA_HEADS):
            s = logits(p, kk, sel_add, bias_of)
            m = mx_ref[head_rows(p), :]
            for t in range(s.shape[1] // KEY_TILE):
                m = jnp.maximum(m, s[:, t * KEY_TILE:(t + 1) * KEY_TILE])
            mx_ref[head_rows(p), :] = m

    def sweep_acc(kk, vv, sel_add, bias_of, row_max):
        for p in range(NSA_HEADS):
            pe = jnp.exp(logits(p, kk, sel_add, bias_of) - row_max[p]).astype(BF16)
            acc_ref[head_rows(p), :] += _dot(pe, vv[p % 2])

    def sweep(kt, pair, bias_of, row_max):
        if pair:
            kk, vv = load_keys(kt * SLC_PAIR, SLC_PAIR, row_max is not None)
            sel_add = [selp_ref[g, kt] for g in range(NSA_KV_HEADS)]
        else:
            kk, vv = load_keys(kt * KEY_TILE, KEY_TILE, row_max is not None)
            sel_add = [selx_ref[g, kt] for g in range(NSA_KV_HEADS)]
        if row_max is None:
            sweep_max(kk, sel_add, bias_of)
        else:
            sweep_acc(kk, vv, sel_add, bias_of, row_max)

    def all_tiles(row_max):
        lax.fori_loop(0, n_pair, lambda kt, cr: (sweep(kt, True, None, row_max), cr)[1], 0)

        @pl.when(odd_far)
        def _():
            sweep(n_far - 1, False, None, row_max)

        for kt, bias_of in near_tiles:
            sweep(kt, False, bias_of, row_max)

    mx_ref[...] = jnp.full((R, LANES), NEG, F32)
    all_tiles(None)
    row_max = [jnp.max(mx_ref[head_rows(p), :], -1, keepdims=True) for p in range(NSA_HEADS)]
    acc_ref[...] = jnp.zeros((R, LANES), F32)
    all_tiles(row_max)
    acc = acc_ref[...]
    acc_ref[...] = acc / pltpu.roll(acc, HEAD_DIM, 1)
    _assemble_heads(acc_ref, ng_ref, o_ref, tq, 1)


def _sel_attend(z3, sel, keys3, key_colblk, n_keys, near_tab, tq, pos0):
    B, T, _ = z3.shape
    R = NSA_HEADS * tq
    n_tiles = n_keys // KEY_TILE
    assert pos0 % KEY_TILE == 0 and tq == KEY_TILE and 2 * n_tiles <= SEL_LANES
    slot = jnp.arange(LANES)[:, None]
    blk = jnp.arange(n_keys)[None, :] // SEL_BLOCK
    emat = jnp.stack([slot == blk + g * SEL_LANES for g in range(NSA_KV_HEADS)]).astype(BF16)
    q_spec, ng_spec, o_spec = _nsa_common_specs(tq)
    kern = functools.partial(_sel_kernel, tq=tq, tile0=pos0 // KEY_TILE, n_tiles=n_tiles)
    return pl.pallas_call(
        kern, grid=(B, T // tq),
        in_specs=[q_spec,
                  pl.BlockSpec((None, tq, LANES), lambda b, i: (b, i, 0)),
                  pl.BlockSpec((None, n_keys, 2 * NSA_KV_W), lambda b, i: (b, 0, key_colblk)),
                  pl.BlockSpec((R, 2 * KEY_TILE), lambda b, i: (0, 0)),
                  ng_spec,
                  pl.BlockSpec((NSA_KV_HEADS, LANES, n_keys), lambda b, i: (0, 0, 0))],
        out_specs=o_spec,
        out_shape=jax.ShapeDtypeStruct((B, T, MIX_W), F32),
        scratch_shapes=_nsa_scratch(tq) + [pltpu.VMEM((R, LANES), F32),
                                           pltpu.VMEM((NSA_KV_HEADS, n_tiles, tq, KEY_TILE), F32),
                                           pltpu.VMEM((NSA_KV_HEADS, max(n_tiles // 2, 1), tq, SLC_PAIR), F32)],
        compiler_params=_cparams("parallel", "parallel"), name="nsa_slc")(
            z3, sel, keys3, near_tab, z3, emat)


WIN_KEYS = WINDOW + Q_BLK


def _win_kernel(q_ref, kv_ref, bias_ref, ng_ref, o_ref, qbd_ref, acc_ref, *, tq):
    qi = pl.program_id(1)
    _build_qbd(q_ref, qbd_ref, tq)
    n_t = WIN_KEYS // KEY_TILE
    ks, vs, drop = [], [], []
    for j in range(n_t):
        kt = qi - (n_t - 1) + j
        rows = pl.ds(pl.multiple_of(jnp.maximum(kt, 0) * KEY_TILE, KEY_TILE), KEY_TILE)
        ks.append(kv_ref[rows, 0:NSA_KV_W].astype(BF16))
        vs.append(kv_ref[rows, NSA_KV_W:2 * NSA_KV_W].astype(BF16))
        drop.append(((kt >= 0).astype(F32) - 1.0) * (-NEG))
    for p in range(NSA_HEADS):
        rows = slice(p * tq, (p + 1) * tq)
        qb = qbd_ref[rows, :].astype(BF16)
        s = [_dot_nt(qb, ks[j]) + (bias_ref[rows, j * KEY_TILE:(j + 1) * KEY_TILE] + drop[j]) for j in range(n_t)]
        lane_max = s[0]
        for sj in s[1:]:
            lane_max = jnp.maximum(lane_max, sj)
        row_max = jnp.max(lane_max, -1, keepdims=True)
        lane_sum, acc = None, None
        for j in range(n_t):
            e = jnp.exp(s[j] - row_max)
            lane_sum = e if lane_sum is None else lane_sum + e
            part = _dot(e.astype(BF16), vs[j])
            acc = part if acc is None else acc + part
        acc_ref[rows, :] = acc / jnp.sum(lane_sum, -1, keepdims=True)
    _assemble_heads(acc_ref, ng_ref, o_ref, tq, 2)


def _win_attend(z3, win_tab, tq):
    B, T, _ = z3.shape
    R = NSA_HEADS * tq
    assert tq == KEY_TILE
    q_spec, ng_spec, o_spec = _nsa_common_specs(tq)
    kern = functools.partial(_win_kernel, tq=tq)
    return pl.pallas_call(
        kern, grid=(B, T // tq),
        in_specs=[q_spec,
                  pl.BlockSpec((None, T, 2 * NSA_KV_W), lambda b, i: (b, 0, _cb("kvw"))),
                  pl.BlockSpec((R, WIN_KEYS), lambda b, i: (0, 0)),
                  ng_spec],
        out_specs=o_spec,
        out_shape=jax.ShapeDtypeStruct((B, T, MIX_W), F32),
        scratch_shapes=_nsa_scratch(tq),
        compiler_params=_cparams("parallel", "parallel"), name="nsa_win")(z3, z3, win_tab, z3)


def _sb_kernel(q_ref, k_ref, v_ref, o_ref, *, tq, heads, tile0):
    qi = pl.program_id(2)
    qt = tile0 + (qi * tq) // KEY_TILE
    sw = heads * HEAD_DIM
    R = heads * tq
    lane_head = lax.shift_right_logical(_iota((tq, sw), 1), int(math.log2(HEAD_DIM)))
    q = q_ref[...] * (HEAD_DIM ** -0.5)
    qbd = jnp.concatenate([jnp.where(lane_head == j, q, 0.0) for j in range(heads)], axis=0).astype(BF16)
    after = (_iota((KEY_TILE, KEY_TILE), 0) > _iota((KEY_TILE, KEY_TILE), 1)).astype(BF16)
    causal = _iota((R, KEY_TILE), 1) < jnp.bitwise_and(_iota((R, KEY_TILE), 0), tq - 1)

    def tile(kt, run, acc, masked):
        rows = pl.ds(pl.multiple_of(kt * KEY_TILE, KEY_TILE), KEY_TILE)
        z = _dot_nt(qbd, k_ref[rows, :].astype(BF16))
        u = _log_sigmoid(-z)
        if masked:
            u = jnp.where(causal, u, 0.0)
        u_hi = u.astype(BF16)
        u_lo = (u - u_hi.astype(F32)).astype(BF16)
        rest = _dot(u_hi, after) + _dot(u_lo, after)
        a = jnp.exp(u + z + rest + run)
        if masked:
            a = jnp.where(causal, a, 0.0)
        acc = acc + _dot(a.astype(BF16), v_ref[rows, :].astype(BF16))
        return run + jnp.sum(u, -1, keepdims=True), acc

    run, acc = tile(qt, jnp.zeros((R, 1), F32), jnp.zeros((R, sw), F32), True)

    def more(st):
        return (st[0] >= 0) & (jnp.max(st[1]) > SB_UNDERFLOW)

    def step(st):
        run, acc = tile(st[0], st[1], st[2], False)
        return st[0] - 1, run, acc

    _, run, acc = lax.while_loop(more, step, (qt - 1, run, acc))
    out = None
    for j in range(heads):
        part = jnp.where(lane_head == j, acc[j * tq:(j + 1) * tq], 0.0)
        out = part if out is None else out + part
    o_ref[...] = out


def _sb_attend(z3, keys3, k_colblk0, n_keys, tq, heads, pos0):
    B, T, _ = z3.shape
    sw = heads * HEAD_DIM
    n_slabs = SB_HEADS // heads
    assert pos0 % KEY_TILE == 0 and (tq == KEY_TILE or T == tq) and tq & (tq - 1) == 0
    kern = functools.partial(_sb_kernel, tq=tq, heads=heads, tile0=pos0 // KEY_TILE)
    qcb = _cb("sbq", sw)
    return pl.pallas_call(
        kern, grid=(B, n_slabs, T // tq),
        in_specs=[pl.BlockSpec((None, tq, sw), lambda b, s, i: (b, i, qcb + s)),
                  pl.BlockSpec((None, n_keys, sw), lambda b, s, i: (b, 0, k_colblk0 + s)),
                  pl.BlockSpec((None, n_keys, sw), lambda b, s, i: (b, 0, k_colblk0 + n_slabs + s))],
        out_specs=pl.BlockSpec((None, tq, sw), lambda b, s, i: (b, i, s)),
        out_shape=jax.ShapeDtypeStruct((B, T, MIX_W), F32),
        compiler_params=_cparams("parallel", "parallel", "parallel"), name="sb")(z3, keys3, keys3)


def _new_tile(new_ref, lanes):
    x = new_ref[:, lanes]
    return jnp.concatenate([x, jnp.zeros((KEY_TILE - x.shape[0], x.shape[1]), F32)], axis=0).astype(BF16)


def _sb_tile(z, weigh_values, mask, run_ref, acc_ref):
    after = (_iota((KEY_TILE, KEY_TILE), 0) > _iota((KEY_TILE, KEY_TILE), 1)).astype(BF16)
    u = _log_sigmoid(-z)
    if mask is not None:
        u = jnp.where(mask, u, 0.0)
    u_hi = u.astype(BF16)
    u_lo = (u - u_hi.astype(F32)).astype(BF16)
    a = jnp.exp(u + z + _dot(u_hi, after) + _dot(u_lo, after) + run_ref[...])
    if mask is not None:
        a = jnp.where(mask, a, 0.0)
    acc_ref[...] += weigh_values(a.astype(BF16))
    run_ref[...] += jnp.sum(u, -1, keepdims=True)


def _sb_build_qbd(q_ref, qbd_ref, tq):
    sw = SB_HEADS * HEAD_DIM
    lane_head = lax.shift_right_logical(_iota((tq, sw), 1), int(math.log2(HEAD_DIM)))
    q = q_ref[...] * (HEAD_DIM ** -0.5)
    for h in range(SB_HEADS):
        qbd_ref[h * tq:(h + 1) * tq, :] = jnp.where(lane_head == h, q, 0.0)


def _sb_pages(pages, qbd_ref, run_ref, acc_ref):
    sw = SB_HEADS * HEAD_DIM
    for page in reversed(pages):
        @pl.when(jnp.max(run_ref[...]) > SB_UNDERFLOW)
        def _(page=page):
            kt = page[0].reshape(sw, PAGE_SIZE).astype(BF16)
            vt = page[1].reshape(sw, PAGE_SIZE).astype(BF16)
            _sb_tile(_dot(qbd_ref[...].astype(BF16), kt), lambda a: _dot_nt(a, vt), None, run_ref, acc_ref)


def _sb_lead_kernel(pt_ref, q_ref, new_ref, *refs, tq, n_lead):
    pages = refs[:n_lead]
    run_ref, acc_ref, qbd_ref = refs[n_lead:]
    sw = SB_HEADS * HEAD_DIM
    R = SB_HEADS * tq
    _sb_build_qbd(q_ref, qbd_ref, tq)
    run_ref[...] = jnp.zeros((R, 1), F32)
    acc_ref[...] = jnp.zeros((R, sw), F32)
    k_new = _new_tile(new_ref, slice(0, sw))
    v_new = _new_tile(new_ref, slice(sw, 2 * sw))
    causal = _iota((R, KEY_TILE), 1) < jnp.bitwise_and(_iota((R, KEY_TILE), 0), tq - 1)
    _sb_tile(_dot_nt(qbd_ref[...].astype(BF16), k_new), lambda a: _dot(a, v_new), causal, run_ref, acc_ref)
    _sb_pages(pages, qbd_ref, run_ref, acc_ref)


def _sb_rest_kernel(pt_ref, need_ref, q_ref, run_in_ref, acc_in_ref, *refs, tq, n_rest):
    pages = refs[:n_rest]
    o_ref, qbd_ref, run_ref, acc_ref = refs[n_rest:]
    sw = SB_HEADS * HEAD_DIM
    run_ref[...] = run_in_ref[...]
    acc_ref[...] = acc_in_ref[...]

    @pl.when(need_ref[pl.program_id(0)] > 0)
    def _():
        _sb_build_qbd(q_ref, qbd_ref, tq)
        _sb_pages(pages, qbd_ref, run_ref, acc_ref)

    lane_head = lax.shift_right_logical(_iota((tq, sw), 1), int(math.log2(HEAD_DIM)))
    out = None
    for h in range(SB_HEADS):
        part = jnp.where(lane_head == h, acc_ref[h * tq:(h + 1) * tq, :], 0.0)
        out = part if out is None else out + part
    o_ref[...] = out


SB_LEAD_PAGES = 2


def _sb_paged(z3, pool_t, layer, pt_flat, n_pages):
    B, tq, _ = z3.shape
    sw = SB_HEADS * HEAD_DIM
    R = SB_HEADS * tq
    n_lead = min(SB_LEAD_PAGES, n_pages - 1)
    n_rest = n_pages - n_lead
    assert tq & (tq - 1) == 0 and tq <= KEY_TILE
    page_block = (None, None, 2, SB_HEADS, HEAD_DIM, PAGE_SIZE)
    q_spec = pl.BlockSpec((None, tq, sw), lambda b, *_: (b, 0, _cb("sbq")))
    run_spec = pl.BlockSpec((None, R, 1), lambda b, *_: (b, 0, 0))
    acc_spec = pl.BlockSpec((None, R, sw), lambda b, *_: (b, 0, 0))

    lead_pages = [_page_spec(page_block, layer, n_pages, (lambda i=i: n_rest + i), 0, 0, 0, 0) for i in range(n_lead)]
    run, acc = pl.pallas_call(
        functools.partial(_sb_lead_kernel, tq=tq, n_lead=n_lead),
        grid_spec=pltpu.PrefetchScalarGridSpec(
            num_scalar_prefetch=1, grid=(B,),
            in_specs=[q_spec, pl.BlockSpec((None, tq, 2 * sw), lambda b, pt: (b, 0, _cb("sbkv")))] + lead_pages,
            out_specs=[run_spec, acc_spec],
            scratch_shapes=[pltpu.VMEM((R, sw), F32)]),
        out_shape=[jax.ShapeDtypeStruct((B, R, 1), F32), jax.ShapeDtypeStruct((B, R, sw), F32)],
        compiler_params=_cparams("parallel"), name="sb_lead")(pt_flat, z3, z3, *([pool_t] * n_lead))

    need = (jnp.max(run, axis=(1, 2)) > SB_UNDERFLOW).astype(jnp.int32)

    def rest_page(i):
        def index_map(b, pt, nd):
            return (layer, pt[jnp.where(nd[b] > 0, b * n_pages + i, i)], 0, 0, 0, 0)
        return pl.BlockSpec(page_block, index_map)

    return pl.pallas_call(
        functools.partial(_sb_rest_kernel, tq=tq, n_rest=n_rest),
        grid_spec=pltpu.PrefetchScalarGridSpec(
            num_scalar_prefetch=2, grid=(B,),
            in_specs=[q_spec, run_spec, acc_spec] + [rest_page(i) for i in range(n_rest)],
            out_specs=pl.BlockSpec((None, tq, sw), lambda b, pt, nd: (b, 0, 0)),
            scratch_shapes=[pltpu.VMEM((R, sw), F32), pltpu.VMEM((R, 1), F32), pltpu.VMEM((R, sw), F32)]),
        out_shape=jax.ShapeDtypeStruct((B, tq, MIX_W), F32),
        compiler_params=_cparams("arbitrary"), name="sb_rest")(
            pt_flat, need, z3, run, acc, *([pool_t] * n_rest))


def _sel_paged_kernel(pt_ref, q_ref, sel_ref, new_ref, near_ref, ng_ref, emat_ref, *refs, tq, n_pages):
    pages = refs[:n_pages]
    o_ref, qbd_ref, acc_ref = refs[n_pages:]
    _build_qbd(q_ref, qbd_ref, tq)
    qb = qbd_ref[...].astype(BF16)
    sel_b = sel_ref[...].astype(BF16)
    k_new = _new_tile(new_ref, slice(0, NSA_KV_W))
    v_new = _new_tile(new_ref, slice(NSA_KV_W, 2 * NSA_KV_W))
    n_tiles = n_pages + 1
    logits = []
    for t in range(n_tiles):
        hit = [_dot(sel_b, emat_ref[g, :, t * KEY_TILE:(t + 1) * KEY_TILE]) for g in range(NSA_KV_HEADS)]
        open_ = jnp.concatenate([hit[p % 2] for p in range(NSA_HEADS)], axis=0)
        if t < n_pages:
            s = _dot(qb, pages[t][0].reshape(NSA_KV_W, PAGE_SIZE).astype(BF16))
        else:
            s = _dot_nt(qb, k_new)
        s = s + (open_ - 1.0) * (-NEG)
        if t >= n_pages - 1:
            s = s + near_ref[:, (t - n_pages + 1) * KEY_TILE:(t - n_pages + 2) * KEY_TILE]
        logits.append(s)
    lane_max = logits[0]
    for s in logits[1:]:
        lane_max = jnp.maximum(lane_max, s)
    row_max = jnp.max(lane_max, -1, keepdims=True)
    lane_sum = None
    acc = None
    for t, s in enumerate(logits):
        pe = jnp.exp(s - row_max)
        lane_sum = pe if lane_sum is None else lane_sum + pe
        pb = pe.astype(BF16)
        part = _dot_nt(pb, pages[t][1].reshape(NSA_KV_W, PAGE_SIZE).astype(BF16)) if t < n_pages else _dot(pb, v_new)
        acc = part if acc is None else acc + part
    acc_ref[...] = acc / jnp.sum(lane_sum, -1, keepdims=True)
    _assemble_heads(acc_ref, ng_ref, o_ref, tq, 1)


def _sel_paged(z3, sel, pool_t, layer, pt_flat, n_pages, near_tab):
    B, tq, _ = z3.shape
    R = NSA_HEADS * tq
    n_keys = (n_pages + 1) * KEY_TILE
    assert 2 * (n_pages + 1) <= SEL_LANES and tq <= KEY_TILE
    slot = jnp.arange(LANES)[:, None]
    blk = jnp.arange(n_keys)[None, :] // SEL_BLOCK
    emat = jnp.stack([slot == blk + g * SEL_LANES for g in range(NSA_KV_HEADS)]).astype(BF16)
    kern = functools.partial(_sel_paged_kernel, tq=tq, n_pages=n_pages)
    page_specs = [_page_spec((None, None, 2, NSA_KV_HEADS, HEAD_DIM, PAGE_SIZE), layer, n_pages,
                             (lambda p=p: p), 0, 0, 0, 0) for p in range(n_pages)]
    grid_spec = pltpu.PrefetchScalarGridSpec(
        num_scalar_prefetch=1, grid=(B,),
        in_specs=[pl.BlockSpec((None, tq, MIX_W), lambda b, pt: (b, 0, _cb("nq"))),
                  pl.BlockSpec((None, tq, LANES), lambda b, pt: (b, 0, 0)),
                  pl.BlockSpec((None, tq, 2 * NSA_KV_W), lambda b, pt: (b, 0, _cb("kvs"))),
                  pl.BlockSpec((R, 2 * KEY_TILE), lambda b, pt: (0, 0)),
                  pl.BlockSpec((None, tq, LANES), lambda b, pt: (b, 0, _cb("ng"))),
                  pl.BlockSpec((NSA_KV_HEADS, LANES, n_keys), lambda b, pt: (0, 0, 0))] + page_specs,
        out_specs=pl.BlockSpec((None, tq, MIX_W), lambda b, pt: (b, 0, 0)),
        scratch_shapes=_nsa_scratch(tq))
    return pl.pallas_call(
        kern, grid_spec=grid_spec,
        out_shape=jax.ShapeDtypeStruct((B, tq, MIX_W), F32),
        compiler_params=_cparams("parallel"), name="nsa_slc_paged")(
            pt_flat, z3, sel, z3, near_tab, z3, emat, *([pool_t] * n_pages))


def _win_state_kernel(q_ref, state_ref, new_ref, bias_ref, ng_ref, o_ref, qbd_ref, acc_ref, *, tq):
    _build_qbd(q_ref, qbd_ref, tq)
    qb = qbd_ref[...].astype(BF16)
    kt = state_ref[0].reshape(NSA_KV_W, WINDOW).astype(BF16)
    vt = state_ref[1].reshape(NSA_KV_W, WINDOW).astype(BF16)
    k_new = _new_tile(new_ref, slice(0, NSA_KV_W))
    v_new = _new_tile(new_ref, slice(NSA_KV_W, 2 * NSA_KV_W))
    bias = bias_ref[...]
    s = jnp.concatenate([_dot(qb, kt), _dot_nt(qb, k_new)], axis=1) + bias
    pr = _masked_softmax(s, bias > 0.1 * NEG).astype(BF16)
    acc_ref[...] = _dot_nt(pr[:, 0:WINDOW], vt) + _dot(pr[:, WINDOW:WINDOW + KEY_TILE], v_new)
    _assemble_heads(acc_ref, ng_ref, o_ref, tq, 2)


def _win_state(z3, state_t, layer, win_tab):
    B, tq, _ = z3.shape
    R = NSA_HEADS * tq
    assert WIN_KEYS == WINDOW + KEY_TILE and tq <= KEY_TILE
    kern = functools.partial(_win_state_kernel, tq=tq)
    return pl.pallas_call(
        kern, grid=(B,),
        in_specs=[pl.BlockSpec((None, tq, MIX_W), lambda b: (b, 0, _cb("nq"))),
                  pl.BlockSpec((None, None, 2, NSA_KV_HEADS, HEAD_DIM, WINDOW), lambda b: (layer, b, 0, 0, 0, 0)),
                  pl.BlockSpec((None, tq, 2 * NSA_KV_W), lambda b: (b, 0, _cb("kvw"))),
                  pl.BlockSpec((R, WIN_KEYS), lambda b: (0, 0)),
                  pl.BlockSpec((None, tq, LANES), lambda b: (b, 0, _cb("ng")))],
        out_specs=pl.BlockSpec((None, tq, MIX_W), lambda b: (b, 0, 0)),
        out_shape=jax.ShapeDtypeStruct((B, tq, MIX_W), F32),
        scratch_shapes=_nsa_scratch(tq),
        compiler_params=_cparams("parallel"), name="nsa_win_state")(z3, state_t, z3, win_tab, z3)


def _prep_in_proj(w):
    offs = np.concatenate([[0], np.cumsum(IN_WIDTHS)])
    lead = w.shape[:-1]

    def seg(i):
        return w[..., int(offs[i]):int(offs[i + 1])]

    def pad(x, width):
        return jnp.pad(x, [(0, 0)] * (x.ndim - 1) + [(0, width - x.shape[-1])])

    nq = seg(5).reshape(*lead, 2, 4, HEAD_DIM).swapaxes(-3, -2).reshape(*lead, MIX_W)
    ng = seg(12).reshape(*lead, 3, 2, 4).swapaxes(-2, -1).reshape(*lead, 3 * NSA_HEADS)
    out = jnp.concatenate([seg(16), seg(14), seg(15), seg(2), seg(3), nq, seg(13), seg(0), seg(1),
                           seg(6), seg(7), seg(8), seg(9), seg(10), seg(11), pad(seg(4), LANES), pad(ng, LANES)], -1)
    assert out.shape[-1] == ZW
    return out


def _t5_bucket(dist):
    n = jnp.maximum(dist, 0)
    exact = REL_BUCKETS // 2
    scaled = jnp.log(jnp.maximum(n, 1).astype(F32) / exact) / math.log(REL_MAX_DIST / exact)
    large = jnp.minimum(exact + (scaled * (REL_BUCKETS - exact)).astype(jnp.int32), REL_BUCKETS - 1)
    return jnp.where(n < exact, n, large)


def _bias_rows(rel_slots, dist, valid, tq):
    T, S = dist.shape
    bucket = _t5_bucket(dist).reshape(T // tq, 1, tq, S)
    tab = jnp.zeros((T // tq, NSA_HEADS, tq, S), F32)
    for k in range(REL_BUCKETS):
        tab = jnp.where(bucket == k, rel_slots[k].reshape(1, NSA_HEADS, 1, 1), tab)
    tab = jnp.where(valid.reshape(T // tq, 1, tq, S), tab, NEG)
    return tab.reshape(T // tq, NSA_HEADS * tq, S)


def _group_tables(rel_slots, T, tq, pos0, n_cmp_pad):
    pos_q = pos0 + np.arange(T)
    end = np.arange(n_cmp_pad) * CMP_STRIDE + CMP_BLOCK - 1
    d_cmp = jnp.asarray(pos_q[:, None] - end[None, :], jnp.int32)
    cmp_tab = _bias_rows(rel_slots, d_cmp, d_cmp >= 0, tq)
    t = np.arange(tq)
    d_near = jnp.asarray(KEY_TILE + t[:, None] - np.arange(2 * KEY_TILE)[None, :], jnp.int32)
    far = jnp.repeat(rel_slots[REL_BUCKETS - 1], tq)[:, None]
    near_tab = _bias_rows(rel_slots, d_near, d_near >= 0, tq)[0] - far
    d_win = jnp.asarray(WINDOW + t[:, None] - np.arange(WIN_KEYS)[None, :], jnp.int32)
    win_tab = _bias_rows(rel_slots, d_win, (d_win >= 0) & (d_win <= WINDOW), tq)[0]
    return cmp_tab, near_tab, win_tab


def _cover(n_cmp_pad):
    start = np.arange(n_cmp_pad) * CMP_STRIDE
    blk = np.arange(SEL_LANES) * SEL_BLOCK
    c = ((start[:, None] < blk[None, :] + SEL_BLOCK) & (start[:, None] + CMP_BLOCK > blk[None, :])).astype(np.float32)
    out = np.zeros((2, LANES, n_cmp_pad), np.float32)
    out[0, :SEL_LANES, :] = c.T
    out[1, SEL_LANES:, :] = c.T
    return jnp.asarray(out)


def kernel(x_prompt, x_sample, cache_cmp_kv, cache_slc_kv, cache_sb_kv, state_win_kv, state_gla, page_table,
           rel_bias, w_in, b_in, w_gla_a2, b_gla_a2, gla_norm_g, w_cmp1, b_cmp1, w_cmp2, cmp_pe,
           w_branch, w_o, ln1_g, ln1_b, w_ff_gate, w_ff_up, w_ff_down, ln2_g, ln2_b):
    depth = w_in.shape[0]
    B, T, _ = x_prompt.shape
    DB, DT, _ = x_sample.shape
    n_pages = page_table.shape[1]
    past = n_pages * PAGE_SIZE
    n_win_state = state_win_kv.shape[2]
    assert T % Q_BLK == 0 and DT % SUBLANES == 0 and DT <= Q_BLK and n_win_state == WINDOW and T >= WINDOW

    w_all = _prep_in_proj(w_in).astype(BF16)
    b_all = _prep_in_proj(b_in)[:, None, :]
    wa = jnp.pad(w_gla_a2, ((0, 0), (0, LANES - GLA_GATE_RANK), (0, 0)))
    eye2 = jnp.eye(2, dtype=F32)
    w1bd = jnp.einsum("gG,ljcdh->lcjgdGh", eye2, w_cmp1).reshape(
        depth, 2, 2, CMP_STRIDE * NSA_KV_W, 2 * CMP_HIDDEN).astype(BF16)
    w2bd = jnp.einsum("gG,lchd->lcghGd", eye2, w_cmp2).reshape(depth, 2, 2 * CMP_HIDDEN, NSA_KV_W).astype(BF16)
    pe2 = jnp.broadcast_to(cmp_pe.swapaxes(1, 2)[:, :, :, None, :], (depth, 2, CMP_BLOCK, 2, HEAD_DIM)).reshape(
        depth, 2, CMP_BLOCK, NSA_KV_W)
    b1t = jnp.broadcast_to(b_cmp1[:, :, None, None, :], (depth, 2, 1, 2, CMP_HIDDEN)).reshape(
        depth, 2, 1, 2 * CMP_HIDDEN)
    wb_nsa = w_branch[:, 1].reshape(depth, 2, 4, HEAD_DIM, D_MODEL).swapaxes(1, 2).reshape(depth, MIX_W, D_MODEL)
    wb = jnp.stack([w_branch[:, 0], wb_nsa, w_branch[:, 2]], 1).astype(BF16)
    wo = w_o.astype(BF16)
    wg, wu, wd = w_ff_gate.astype(BF16), w_ff_up.astype(BF16), w_ff_down.astype(BF16)
    rel_slots = rel_bias.reshape(REL_BUCKETS, 2, 4).swapaxes(1, 2).reshape(REL_BUCKETS, NSA_HEADS)

    n_seg_p = T // CMP_STRIDE
    n_seg_s = past // CMP_STRIDE
    tabs_p = _group_tables(rel_slots, T, Q_BLK, 0, n_seg_p)
    tabs_s = _group_tables(rel_slots, DT, DT, past, n_seg_s)
    cover_p, cover_s = _cover(n_seg_p), _cover(n_seg_s)
    pt_flat = page_table.reshape(-1).astype(jnp.int32)
    cmp_pool, slc_pool, sb_pool, win_state_t = (a.transpose(0, 1, 3, 4, 5, 2) for a in
                                                (cache_cmp_kv, cache_slc_kv, cache_sb_kv, state_win_kv))

    xp = x_prompt.reshape(B * T, D_MODEL)
    xs = x_sample.reshape(DB * DT, D_MODEL)
    zero_state = jnp.zeros((B, GLA_HEADS, GLA_DV, GLA_DK), F32)
    outs_p, outs_s = [], []

    def dense_tail(o, z, x, l):
        n = x.shape[0]
        o_gla, o_cmp, o_slc, o_win, o_sb = (a.reshape(n, MIX_W) for a in o)
        x1 = _merge(o_gla, o_cmp, o_slc, o_win, o_sb, z, x, wb[l], wo[l], ln1_g[l][None], ln1_b[l][None])
        return _ffn(x1, wg[l], wu[l], wd[l], ln2_g[l][None], ln2_b[l][None])

    def col(z3, name):
        off, w = COL[name]
        return z3[..., off:off + w]

    for l in range(depth):
        cw = (w1bd[l], pe2[l], b1t[l], w2bd[l])

        z = _linear(xp, w_all[l], b_all[l])
        z3 = z.reshape(B, T, ZW)
        kvc = _compress(z3, _cb("kvc"), T, *cw)
        cmp_tab, near_tab, win_tab = tabs_p
        o_gla, st = _gla(z3, zero_state, wa[l], b_gla_a2[l][None], gla_norm_g[l][None])
        o_cmp, sel = _cmp_attend(z3, kvc, cmp_tab, cover_p, Q_BLK, 0)
        o_slc = _sel_attend(z3, sel, z3, _cb("kvs"), T, near_tab, Q_BLK, 0)
        o_win = _win_attend(z3, win_tab, Q_BLK)
        o_sb = _sb_attend(z3, z3, _cb("sbkv", SB_PROMPT_HEADS * HEAD_DIM), T, Q_BLK, SB_PROMPT_HEADS, 0)
        xp = dense_tail((o_gla, o_cmp, o_slc, o_win, o_sb), z, xp, l)
        outs_p.append((col(z3, "kvc"), col(z3, "kvs"), col(z3, "sbkv"), col(z3, "kvw")[:, T - WINDOW:],
                       st.swapaxes(-1, -2)))

        z = _linear(xs, w_all[l], b_all[l])
        z3 = z.reshape(DB, DT, ZW)
        cmp_tab, near_tab, win_tab = tabs_s
        o_gla, st = _gla(z3, state_gla[l].swapaxes(-1, -2), wa[l], b_gla_a2[l][None], gla_norm_g[l][None])
        kvc = _compress_paged(cmp_pool, l, pt_flat, n_pages, *cw)
        o_cmp, sel = _cmp_attend(z3, kvc, cmp_tab, cover_s, DT, past)
        o_slc = _sel_paged(z3, sel, slc_pool, l, pt_flat, n_pages, near_tab)
        o_win = _win_state(z3, win_state_t, l, win_tab)
        o_sb = _sb_paged(z3, sb_pool, l, pt_flat, n_pages)
        xs = dense_tail((o_gla, o_cmp, o_slc, o_win, o_sb), z, xs, l)
        win_all = jnp.concatenate([state_win_kv[l].reshape(DB, n_win_state, 2 * NSA_KV_W), col(z3, "kvw")], 1)
        outs_s.append((col(z3, "kvc"), col(z3, "kvs"), col(z3, "sbkv"), win_all[:, win_all.shape[1] - WINDOW:],
                       st.swapaxes(-1, -2)))

    def stacked(outs, i, tail):
        a = jnp.stack([o[i] for o in outs])
        return a.reshape(*a.shape[:3], *tail)

    kv_tail = (2, NSA_KV_HEADS, HEAD_DIM)
    sb_tail = (2, SB_HEADS, HEAD_DIM)
    return (xp.reshape(B, T, D_MODEL), xs.reshape(DB, DT, D_MODEL),
            stacked(outs_p, 0, kv_tail), stacked(outs_s, 0, kv_tail),
            stacked(outs_p, 1, kv_tail), stacked(outs_s, 1, kv_tail),
            stacked(outs_p, 2, sb_tail), stacked(outs_s, 2, sb_tail),
            stacked(outs_p, 3, kv_tail), stacked(outs_s, 3, kv_tail),
            jnp.stack([o[4] for o in outs_p]), jnp.stack([o[4] for o in outs_s]))
```

```python
import functools
import math

import jax
import jax.numpy as jnp
import numpy as np
from jax import lax
from jax.experimental import pallas as pl
from jax.experimental.pallas import tpu as pltpu

F32, BF16 = jnp.float32, jnp.bfloat16
HI = lax.Precision.HIGHEST

D_MODEL = 1024
HEAD_DIM = 64
MIX_W = D_MODEL // 2
GLA_HEADS = 4
GLA_DV = MIX_W // GLA_HEADS
GLA_DK = GLA_DV // 2
GLA_GATE_RANK = 16
GLA_TAU = 16.0
GLA_CHUNK = 64
GLA_SUB = 16
NSA_HEADS = MIX_W // HEAD_DIM
NSA_KV_HEADS = 2
NSA_KV_W = NSA_KV_HEADS * HEAD_DIM
CMP_STRIDE = 16
CMP_BLOCK = 32
CMP_HIDDEN = 128
SEL_BLOCK = 64
SEL_TOPN = 16
WINDOW = 512
FORCE_BONUS = 1e4
SB_HEADS = MIX_W // HEAD_DIM
REL_BUCKETS = 32
REL_MAX_DIST = 128
D_FF = -(-8 * D_MODEL // (3 * 256)) * 256
Q_BLK = 128
LN_EPS = 1e-5
TRUNK_DEPTH = 4
DN_ALPHA = (2 * TRUNK_DEPTH) ** 0.25
PAGE_SIZE = 128
GLA_QK_W = GLA_HEADS * GLA_DK
IN_WIDTHS = (GLA_QK_W, GLA_QK_W, MIX_W, MIX_W, GLA_GATE_RANK,
             MIX_W, NSA_KV_W, NSA_KV_W, NSA_KV_W, NSA_KV_W, NSA_KV_W, NSA_KV_W, 3 * NSA_HEADS,
             MIX_W, MIX_W, MIX_W,
             3 * D_MODEL)

LANES = 128
SUBLANES = 8
KEY_TILE = 128
VMEM_LIMIT = 56 * 1024 * 1024
NEG = -1e30
SEL_LANES = 64
SB_UNDERFLOW = -104.0
SB_PROMPT_HEADS = 4
GLA_SAFE_DECAY = 60.0

COL = dict(mg=(0, 3072), sbkv=(3072, 1024), gv=(4096, 512), gr=(4608, 512), nq=(5120, 512),
           sbq=(5632, 512), gqk=(6144, 512), kvc=(6656, 256), kvs=(6912, 256), kvw=(7168, 256),
           ga=(7424, 128), ng=(7552, 128))
ZW = 7680


def _cb(name, width=None):
    off, w = COL[name]
    width = width or w
    assert off % width == 0
    return off // width


def _cparams(*sem):
    return pltpu.CompilerParams(dimension_semantics=sem, vmem_limit_bytes=VMEM_LIMIT)


def _dot(a, b):
    return jnp.dot(a, b, preferred_element_type=F32)


def _dot_hi(a, b):
    return jnp.dot(a, b, precision=HI, preferred_element_type=F32)


def _dot_nt(a, b):
    return lax.dot_general(a, b, (((1,), (1,)), ((), ())), preferred_element_type=F32)


def _dot_tn(a, b):
    return lax.dot_general(a, b, (((0,), (0,)), ((), ())), preferred_element_type=F32)


def _log_sigmoid(x):
    return jnp.minimum(x, 0.0) - jnp.log(1.0 + jnp.exp(-jnp.abs(x)))


def _iota(shape, dim):
    return lax.broadcasted_iota(jnp.int32, shape, dim)


def _pick_tile(n, cands=(1024, 512, 256, 128, 64, 32, 16, 8)):
    for c in cands:
        if n % c == 0:
            return c
    raise ValueError(f"no row tile divides {n}")


def _linear_kernel(x_ref, w_ref, b_ref, o_ref, xb_ref):
    @pl.when(pl.program_id(1) == 0)
    def _():
        xb_ref[...] = x_ref[...].astype(BF16)

    o_ref[...] = _dot(xb_ref[...], w_ref[...]) + b_ref[...]


def _linear(x, w, b):
    n, k = x.shape
    dout = w.shape[1]
    tm, tn = _pick_tile(n), 512
    return pl.pallas_call(
        _linear_kernel, grid=(n // tm, dout // tn),
        in_specs=[pl.BlockSpec((tm, k), lambda i, j: (i, 0)),
                  pl.BlockSpec((k, tn), lambda i, j: (0, j)),
                  pl.BlockSpec((1, tn), lambda i, j: (0, j))],
        out_specs=pl.BlockSpec((tm, tn), lambda i, j: (i, j)),
        out_shape=jax.ShapeDtypeStruct((n, dout), F32),
        scratch_shapes=[pltpu.VMEM((tm, k), BF16)],
        compiler_params=_cparams("parallel", "arbitrary"), name="in_proj")(x, w, b)


def _layer_norm(h, g, b):
    mu = jnp.mean(h, -1, keepdims=True)
    d = h - mu
    var = jnp.mean(d * d, -1, keepdims=True)
    return d * lax.rsqrt(var + LN_EPS) * g + b


def _merge_kernel(og_ref, oc_ref, os_ref, ow_ref, ob_ref, mg_ref, x_ref, wb_ref, wo_ref, g_ref, b_ref, o_ref):
    o_nsa = oc_ref[...] + os_ref[...] + ow_ref[...]
    mixed = None
    for br, o in enumerate((og_ref[...], o_nsa, ob_ref[...])):
        proj = _dot(o.astype(BF16), wb_ref[br])
        term = jax.nn.sigmoid(mg_ref[:, br * D_MODEL:(br + 1) * D_MODEL]) * proj
        mixed = term if mixed is None else mixed + term
    y = _dot(mixed.astype(BF16), wo_ref[...])
    o_ref[...] = _layer_norm(DN_ALPHA * x_ref[...] + y, g_ref[...], b_ref[...])


def _merge(o_gla, o_cmp, o_slc, o_win, o_sb, z, x, wb, wo, g, b):
    n = x.shape[0]
    tm = _pick_tile(n, (512, 256, 128, 64, 32, 16, 8))
    mix = pl.BlockSpec((tm, MIX_W), lambda i: (i, 0))
    return pl.pallas_call(
        _merge_kernel, grid=(n // tm,),
        in_specs=[mix, mix, mix, mix, mix,
                  pl.BlockSpec((tm, 3 * D_MODEL), lambda i: (i, _cb("mg"))),
                  pl.BlockSpec((tm, D_MODEL), lambda i: (i, 0)),
                  pl.BlockSpec((3, MIX_W, D_MODEL), lambda i: (0, 0, 0)),
                  pl.BlockSpec((D_MODEL, D_MODEL), lambda i: (0, 0)),
                  pl.BlockSpec((1, D_MODEL), lambda i: (0, 0)),
                  pl.BlockSpec((1, D_MODEL), lambda i: (0, 0))],
        out_specs=pl.BlockSpec((tm, D_MODEL), lambda i: (i, 0)),
        out_shape=jax.ShapeDtypeStruct((n, D_MODEL), F32),
        compiler_params=_cparams("parallel"), name="merge")(o_gla, o_cmp, o_slc, o_win, o_sb, z, x, wb, wo, g, b)


FF_TILE = 256


def _ffn_kernel(x_ref, wg_ref, wu_ref, wd_ref, g_ref, b_ref, o_ref, xb_ref, acc_ref):
    f = pl.program_id(1)

    @pl.when(f == 0)
    def _():
        xb_ref[...] = x_ref[...].astype(BF16)
        acc_ref[...] = jnp.zeros_like(acc_ref)

    xb = xb_ref[...]
    gate = _dot(xb, wg_ref[...])
    up = _dot(xb, wu_ref[...])
    h = gate * jax.nn.sigmoid(gate) * up
    acc_ref[...] += _dot(h.astype(BF16), wd_ref[...])

    @pl.when(f == pl.num_programs(1) - 1)
    def _():
        o_ref[...] = _layer_norm(DN_ALPHA * x_ref[...] + acc_ref[...], g_ref[...], b_ref[...])


def _ffn(x, wg, wu, wd, g, b):
    n = x.shape[0]
    tm = _pick_tile(n)
    return pl.pallas_call(
        _ffn_kernel, grid=(n // tm, D_FF // FF_TILE),
        in_specs=[pl.BlockSpec((tm, D_MODEL), lambda i, f: (i, 0)),
                  pl.BlockSpec((D_MODEL, FF_TILE), lambda i, f: (0, f)),
                  pl.BlockSpec((D_MODEL, FF_TILE), lambda i, f: (0, f)),
                  pl.BlockSpec((FF_TILE, D_MODEL), lambda i, f: (f, 0)),
                  pl.BlockSpec((1, D_MODEL), lambda i, f: (0, 0)),
                  pl.BlockSpec((1, D_MODEL), lambda i, f: (0, 0))],
        out_specs=pl.BlockSpec((tm, D_MODEL), lambda i, f: (i, 0)),
        out_shape=jax.ShapeDtypeStruct((n, D_MODEL), F32),
        scratch_shapes=[pltpu.VMEM((tm, D_MODEL), BF16), pltpu.VMEM((tm, D_MODEL), F32)],
        compiler_params=_cparams("parallel", "arbitrary"), name="ffn")(x, wg, wu, wd, g, b)


def _gla_kernel(qk_ref, v_ref, r_ref, ga_ref, s0_ref, wa_ref, ba_ref, ng_ref, o_ref, st_ref, *, chunk, n_chunks):
    C = chunk

    @pl.when(pl.program_id(1) == 0)
    def _():
        st_ref[...] = s0_ref[...]

    sub = min(C, GLA_SUB)
    n_sub = C // sub
    causal = _iota((C, C), 1) <= _iota((C, C), 0)
    tril = causal.astype(F32)
    row_sub = _iota((sub, GLA_DK), 0)

    def intra_factored(qh, kh, vh, ch):
        qd = qh * jnp.exp(ch)
        kd = kh * jnp.exp(-ch)
        att = jnp.where(causal, _dot_nt(qd.astype(BF16), kd.astype(BF16)), 0.0)
        return _dot(att.astype(BF16), vh.astype(BF16))

    def intra_blocked(qh, kh, vh, ch):
        outs = []
        for sb in range(n_sub):
            a0 = sb * sub
            qs, ks, vs, cs = qh[a0:a0 + sub], kh[a0:a0 + sub], vh[a0:a0 + sub], ch[a0:a0 + sub]
            oi = jnp.zeros((sub, GLA_DV), F32)
            if sb > 0:
                cref = ch[a0 - 1:a0]
                qd = qs * jnp.exp(cs - cref)
                kd = kh[:a0] * jnp.exp(cref - ch[:a0])
                att = _dot_nt(qd.astype(BF16), kd.astype(BF16))
                oi = oi + _dot(att.astype(BF16), vh[:a0].astype(BF16))
            for s in range(sub):
                e = jnp.exp(jnp.minimum(cs - cs[s:s + 1], 0.0))
                w = jnp.sum(jnp.where(row_sub >= s, qs * ks[s:s + 1] * e, 0.0), axis=-1, keepdims=True)
                oi = oi + w * vs[s:s + 1]
            outs.append(oi)
        return outs[0] if n_sub == 1 else jnp.concatenate(outs, axis=0)

    def chunk_body(c, carry):
        r0 = pl.multiple_of(c * C, C)
        rows = pl.ds(r0, C)
        qk = qk_ref[rows, :]
        v = v_ref[rows, :]
        log_a = _log_sigmoid(_dot_hi(ga_ref[rows, :], wa_ref[...]) + ba_ref[...]) * (1.0 / GLA_TAU)
        cum = _dot_hi(tril, log_a)
        parts = []
        for h in range(GLA_HEADS):
            parts.append((qk[:, h * GLA_DK:(h + 1) * GLA_DK] * (GLA_DK ** -0.5),
                          qk[:, GLA_QK_W + h * GLA_DK:GLA_QK_W + (h + 1) * GLA_DK],
                          v[:, h * GLA_DV:(h + 1) * GLA_DV],
                          cum[:, h * GLA_DK:(h + 1) * GLA_DK]))
        small_decay = jnp.max(-cum[C - 1:C, :]) <= GLA_SAFE_DECAY
        intra = lax.cond(small_decay,
                         lambda: tuple(intra_factored(*p) for p in parts),
                         lambda: tuple(intra_blocked(*p) for p in parts))
        for h in range(GLA_HEADS):
            qh, kh, vh, ch = parts[h]
            st = st_ref[h]
            o = intra[h] + _dot_nt((qh * jnp.exp(ch)).astype(BF16), st.astype(BF16))
            mu = jnp.mean(o, -1, keepdims=True)
            d = o - mu
            var = jnp.mean(d * d, -1, keepdims=True)
            lanes = slice(h * GLA_DV, (h + 1) * GLA_DV)
            rr = r_ref[rows, lanes]
            o_ref[rows, lanes] = d * lax.rsqrt(var + LN_EPS) * ng_ref[:, lanes] * (rr * jax.nn.sigmoid(rr))
            last = ch[C - 1:C]
            kdl = kh * jnp.exp(last - ch)
            st_ref[h] = st * jnp.exp(last) + _dot_tn(vh.astype(BF16), kdl.astype(BF16))
        return carry

    lax.fori_loop(0, n_chunks, chunk_body, 0, unroll=2 if n_chunks % 2 == 0 else 1)


def _gla(z3, s0t, wa, ba, norm_g):
    B, T, _ = z3.shape
    C = math.gcd(T, GLA_CHUNK)
    tt = min(T, 512)
    assert T % tt == 0 and tt % C == 0
    kern = functools.partial(_gla_kernel, chunk=C, n_chunks=tt // C)
    st_spec = pl.BlockSpec((None, GLA_HEADS, GLA_DV, GLA_DK), lambda b, i: (b, 0, 0, 0))
    return pl.pallas_call(
        kern, grid=(B, T // tt),
        in_specs=[pl.BlockSpec((None, tt, 512), lambda b, i: (b, i, _cb("gqk"))),
                  pl.BlockSpec((None, tt, 512), lambda b, i: (b, i, _cb("gv"))),
                  pl.BlockSpec((None, tt, 512), lambda b, i: (b, i, _cb("gr"))),
                  pl.BlockSpec((None, tt, LANES), lambda b, i: (b, i, _cb("ga"))),
                  st_spec,
                  pl.BlockSpec((LANES, GLA_QK_W), lambda b, i: (0, 0)),
                  pl.BlockSpec((1, GLA_QK_W), lambda b, i: (0, 0)),
                  pl.BlockSpec((1, MIX_W), lambda b, i: (0, 0))],
        out_specs=[pl.BlockSpec((None, tt, MIX_W), lambda b, i: (b, i, 0)), st_spec],
        out_shape=[jax.ShapeDtypeStruct((B, T, MIX_W), F32),
                   jax.ShapeDtypeStruct((B, GLA_HEADS, GLA_DV, GLA_DK), F32)],
        compiler_params=_cparams("parallel", "arbitrary"), name="gla")(z3, z3, z3, z3, s0t, wa, ba, norm_g)


def _compress_kernel(rows_ref, w1_ref, pe_ref, b1_ref, w2_ref, o_ref, h1_ref, *, n_seg):
    M = n_seg
    xs = [rows_ref[pl.ds(j, M, stride=CMP_STRIDE), :] for j in range(CMP_STRIDE)]
    halves = []
    for r in range(CMP_BLOCK // CMP_STRIDE):
        xr = jnp.concatenate([(xs[j] + pe_ref[r * CMP_STRIDE + j:r * CMP_STRIDE + j + 1, :]).astype(BF16)
                              for j in range(CMP_STRIDE)], axis=1)
        halves.append(_dot(xr, w1_ref[r]))
    h0, h1 = halves
    h1_ref[0:M, :] = h1
    h1_ref[M:M + SUBLANES, :] = jnp.zeros((SUBLANES, 2 * CMP_HIDDEN), F32)
    h = jax.nn.gelu(b1_ref[...] + h0 + h1_ref[pl.ds(1, M), :])
    o_ref[...] = _dot(h.astype(BF16), w2_ref[...])


def _compress(rows3, colblk, n_rows, w1bd, pe2, b1t, w2bd):
    B = rows3.shape[0]
    n_seg = n_rows // CMP_STRIDE
    kern = functools.partial(_compress_kernel, n_seg=n_seg)
    return pl.pallas_call(
        kern, grid=(B, 2),
        in_specs=[pl.BlockSpec((None, n_rows, NSA_KV_W), lambda b, c: (b, 0, 2 * colblk + c)),
                  pl.BlockSpec((None, 2, CMP_STRIDE * NSA_KV_W, 2 * CMP_HIDDEN), lambda b, c: (c, 0, 0, 0)),
                  pl.BlockSpec((None, CMP_BLOCK, NSA_KV_W), lambda b, c: (c, 0, 0)),
                  pl.BlockSpec((None, 1, 2 * CMP_HIDDEN), lambda b, c: (c, 0, 0)),
                  pl.BlockSpec((None, 2 * CMP_HIDDEN, NSA_KV_W), lambda b, c: (c, 0, 0))],
        out_specs=pl.BlockSpec((None, n_seg, NSA_KV_W), lambda b, c: (b, 0, c)),
        out_shape=jax.ShapeDtypeStruct((B, n_seg, 2 * NSA_KV_W), F32),
        scratch_shapes=[pltpu.VMEM((n_seg + SUBLANES, 2 * CMP_HIDDEN), F32)],
        compiler_params=_cparams("parallel", "parallel"), name="compress")(rows3, w1bd, pe2, b1t, w2bd)


def _compress_paged_kernel(pt_ref, *refs, n_pages):
    pages = refs[:n_pages]
    w1_ref, pe_ref, b1_ref, w2_ref, o_ref, h1_ref, rows_ref = refs[n_pages:]
    for p, page in enumerate(pages):
        rows_ref[p * PAGE_SIZE:(p + 1) * PAGE_SIZE, :] = page[...].reshape(NSA_KV_W, PAGE_SIZE).T
    _compress_kernel(rows_ref, w1_ref, pe_ref, b1_ref, w2_ref, o_ref, h1_ref, n_seg=n_pages * PAGE_SIZE // CMP_STRIDE)


def _page_spec(block, layer, n_pages, page_of, *rest):
    def index_map(b, *idx_and_pt):
        *idx, pt = idx_and_pt
        return (layer, pt[b * n_pages + page_of(*idx)]) + tuple(r(*idx) if callable(r) else r for r in rest)
    return pl.BlockSpec(block, index_map)


def _compress_paged(pool_t, layer, pt_flat, n_pages, w1bd, pe2, b1t, w2bd):
    B = pt_flat.shape[0] // n_pages
    n_seg = n_pages * PAGE_SIZE // CMP_STRIDE
    kern = functools.partial(_compress_paged_kernel, n_pages=n_pages)
    page_specs = [_page_spec((None, None, None, NSA_KV_HEADS, HEAD_DIM, PAGE_SIZE), layer, n_pages,
                             (lambda c, p=p: p), (lambda c: c), 0, 0, 0) for p in range(n_pages)]
    grid_spec = pltpu.PrefetchScalarGridSpec(
        num_scalar_prefetch=1, grid=(B, 2),
        in_specs=page_specs + [
            pl.BlockSpec((None, 2, CMP_STRIDE * NSA_KV_W, 2 * CMP_HIDDEN), lambda b, c, pt: (c, 0, 0, 0)),
            pl.BlockSpec((None, CMP_BLOCK, NSA_KV_W), lambda b, c, pt: (c, 0, 0)),
            pl.BlockSpec((None, 1, 2 * CMP_HIDDEN), lambda b, c, pt: (c, 0, 0)),
            pl.BlockSpec((None, 2 * CMP_HIDDEN, NSA_KV_W), lambda b, c, pt: (c, 0, 0))],
        out_specs=pl.BlockSpec((None, n_seg, NSA_KV_W), lambda b, c, pt: (b, 0, c)),
        scratch_shapes=[pltpu.VMEM((n_seg + SUBLANES, 2 * CMP_HIDDEN), F32),
                        pltpu.VMEM((n_pages * PAGE_SIZE, NSA_KV_W), F32)])
    return pl.pallas_call(
        kern, grid_spec=grid_spec,
        out_shape=jax.ShapeDtypeStruct((B, n_seg, 2 * NSA_KV_W), F32),
        compiler_params=_cparams("parallel", "parallel"), name="compress_paged")(
            pt_flat, *([pool_t] * n_pages), w1bd, pe2, b1t, w2bd)


def _build_qbd(q_ref, qbd_ref, tq):
    lane = _iota((tq, LANES), 1)
    for p in range(NSA_HEADS):
        slab = q_ref[:, (p // 2) * LANES:(p // 2 + 1) * LANES] * (HEAD_DIM ** -0.5)
        keep = (lane >= HEAD_DIM) if p % 2 else (lane < HEAD_DIM)
        qbd_ref[p * tq:(p + 1) * tq, :] = jnp.where(keep, slab, 0.0)


def _assemble_heads(acc_ref, ng_ref, o_ref, tq, branch):
    lo = _iota((tq, LANES), 1) < HEAD_DIM
    for k in range(NSA_HEADS // 2):
        a0 = acc_ref[(2 * k) * tq:(2 * k + 1) * tq, :]
        a1 = acc_ref[(2 * k + 1) * tq:(2 * k + 2) * tq, :]
        c0 = branch * NSA_HEADS + 2 * k
        g0 = jax.nn.sigmoid(ng_ref[:, c0:c0 + 1])
        g1 = jax.nn.sigmoid(ng_ref[:, c0 + 1:c0 + 2])
        o_ref[:, k * LANES:(k + 1) * LANES] = jnp.where(lo, a0 * g0, a1 * g1)


def _masked_softmax(s, valid):
    s = jnp.where(valid, s, NEG)
    m = jnp.max(s, -1, keepdims=True)
    e = jnp.where(valid, jnp.exp(s - m), 0.0)
    return e / jnp.maximum(jnp.sum(e, -1, keepdims=True), 1e-30)


def _dot_nt_hi(a, b):
    return lax.dot_general(a, b, (((1,), (1,)), ((), ())), precision=HI, preferred_element_type=F32)


def _select_blocks(score, blk, cur, slot_axis):
    forced = (blk == 0) | (blk == cur) | (blk == cur - 1)
    score = jnp.where(forced, score + FORCE_BONUS, score)
    visible = blk <= cur
    score = jnp.where(visible, score, -jnp.inf)
    first_group = _iota(score.shape, slot_axis) < SEL_LANES
    rank = jnp.zeros(score.shape, F32)
    for mp in range(SEL_LANES):
        if slot_axis == 0:
            other = jnp.where(first_group, score[mp:mp + 1, :], score[SEL_LANES + mp:SEL_LANES + mp + 1, :])
        else:
            other = jnp.where(first_group, score[:, mp:mp + 1], score[:, SEL_LANES + mp:SEL_LANES + mp + 1])
        tie = jnp.where(blk > mp, 1.0, 0.0)
        rank = rank + jnp.where(other > score, 1.0, jnp.where(other == score, tie, 0.0))
    return jnp.where(visible, jnp.where(rank < SEL_TOPN, 1.0, 0.0), 0.0)


def _cmp_kernel(q_ref, kvc_ref, bias_ref, covt_ref, ng_ref, o_ref, sel_ref, qbd_ref, acc_ref, *, tq, rc, pos0):
    qi = pl.program_id(1)
    _build_qbd(q_ref, qbd_ref, tq)
    n = kvc_ref.shape[0]
    kc = kvc_ref[:, 0:NSA_KV_W].astype(BF16)
    vc = kvc_ref[:, NSA_KV_W:2 * NSA_KV_W].astype(BF16)
    ones = jnp.ones((n, LANES), BF16)
    heads_per_chunk = rc // tq
    p_sum = [jnp.zeros((tq, n), F32), jnp.zeros((tq, n), F32)]
    for c in range(NSA_HEADS * tq // rc):
        rows = slice(c * rc, (c + 1) * rc)
        s = _dot_nt(qbd_ref[rows, :].astype(BF16), kc) + bias_ref[rows, :]
        lane_max = s[:, 0:LANES]
        for t in range(1, n // LANES):
            lane_max = jnp.maximum(lane_max, s[:, t * LANES:(t + 1) * LANES])
        row_max = jnp.max(lane_max, -1, keepdims=True)
        e = jnp.exp(s - row_max)
        e_hi = e.astype(BF16)
        e_lo = (e - e_hi.astype(F32)).astype(BF16)
        row_sum = _dot(e_hi, ones) + _dot(e_lo, ones)
        scale = jnp.where(row_max > 0.5 * NEG, 1.0 / row_sum, 0.0)
        pr = jnp.concatenate([e[:, t * LANES:(t + 1) * LANES] * scale for t in range(n // LANES)], axis=1)
        acc_ref[rows, :] = _dot(pr.astype(BF16), vc)
        for pp in range(heads_per_chunk):
            g = (c * heads_per_chunk + pp) % 2
            p_sum[g] = p_sum[g] + pr[pp * tq:(pp + 1) * tq]
    _assemble_heads(acc_ref, ng_ref, o_ref, tq, 0)

    shift = int(math.log2(SEL_BLOCK))
    if tq == LANES:
        score_t = _dot_nt_hi(covt_ref[0], p_sum[0]) + _dot_nt_hi(covt_ref[1], p_sum[1])
        blk = jnp.bitwise_and(_iota((LANES, tq), 0), SEL_LANES - 1)
        cur = lax.shift_right_logical(pos0 + qi * tq + _iota((LANES, tq), 1), shift)
        sel_t = _select_blocks(score_t, blk, cur, 0)
        eye = (_iota((LANES, LANES), 0) == _iota((LANES, LANES), 1)).astype(BF16)
        sel_ref[...] = _dot_tn(sel_t.astype(BF16), eye)
    else:
        score = _dot_nt_hi(p_sum[0], covt_ref[0]) + _dot_nt_hi(p_sum[1], covt_ref[1])
        blk = jnp.bitwise_and(_iota((tq, LANES), 1), SEL_LANES - 1)
        cur = lax.shift_right_logical(pos0 + qi * tq + _iota((tq, LANES), 0), shift)
        sel_ref[...] = _select_blocks(score, blk, cur, 1)


def _nsa_common_specs(tq):
    q_spec = pl.BlockSpec((None, tq, MIX_W), lambda b, i: (b, i, _cb("nq")))
    ng_spec = pl.BlockSpec((None, tq, LANES), lambda b, i: (b, i, _cb("ng")))
    o_spec = pl.BlockSpec((None, tq, MIX_W), lambda b, i: (b, i, 0))
    return q_spec, ng_spec, o_spec


def _nsa_scratch(tq):
    return [pltpu.VMEM((NSA_HEADS * tq, LANES), F32), pltpu.VMEM((NSA_HEADS * tq, LANES), F32)]


def _cmp_attend(z3, kvc, bias_tab, cover2, tq, pos0):
    B, T, _ = z3.shape
    n = kvc.shape[1]
    R = NSA_HEADS * tq
    rc = min(R, 128)
    assert n % LANES == 0
    q_spec, ng_spec, o_spec = _nsa_common_specs(tq)
    kern = functools.partial(_cmp_kernel, tq=tq, rc=rc, pos0=pos0)
    return pl.pallas_call(
        kern, grid=(B, T // tq),
        in_specs=[q_spec,
                  pl.BlockSpec((None, n, 2 * NSA_KV_W), lambda b, i: (b, 0, 0)),
                  pl.BlockSpec((None, R, n), lambda b, i: (i, 0, 0)),
                  pl.BlockSpec((2, LANES, n), lambda b, i: (0, 0, 0)),
                  ng_spec],
        out_specs=[o_spec, pl.BlockSpec((None, tq, LANES), lambda b, i: (b, i, 0))],
        out_shape=[jax.ShapeDtypeStruct((B, T, MIX_W), F32), jax.ShapeDtypeStruct((B, T, LANES), F32)],
        scratch_shapes=_nsa_scratch(tq),
        compiler_params=_cparams("parallel", "parallel"), name="nsa_cmp")(z3, kvc, bias_tab, cover2, z3)


SLC_PAIR = 2 * KEY_TILE


def _sel_kernel(q_ref, sel_ref, kv_ref, near_ref, ng_ref, emat_ref, o_ref,
                qbd_ref, acc_ref, mx_ref, selx_ref, selp_ref, lg_ref, *, tq, tile0, n_tiles):
    qi = pl.program_id(1)
    qt = tile0 + (qi * tq) // KEY_TILE
    R = NSA_HEADS * tq
    _build_qbd(q_ref, qbd_ref, tq)
    sel_b = sel_ref[...].astype(BF16)
    for t in range(n_tiles):
        @pl.when(t <= qt)
        def _(t=t):
            for g in range(NSA_KV_HEADS):
                hit = _dot(sel_b, emat_ref[g, :, t * KEY_TILE:(t + 1) * KEY_TILE])
                selx_ref[g, t] = (hit - 1.0) * (-NEG)
                if t // 2 < n_tiles // 2:
                    selp_ref[g, t // 2, :, (t % 2) * KEY_TILE:(t % 2 + 1) * KEY_TILE] = (hit - 1.0) * (-NEG)
    n_far = jnp.maximum(qt - 1, 0)
    n_pair = lax.shift_right_logical(n_far, 1)
    odd_far = jnp.bitwise_and(n_far, 1) == 1
    drop_prev = ((qt >= 1).astype(F32) - 1.0) * (-NEG)
    odd_slot, near_slot = n_tiles // 2, n_tiles // 2 + 1
    left, right = slice(0, KEY_TILE), slice(KEY_TILE, SLC_PAIR)
    near_tiles = ((n_far, near_slot, left, lambda rows: near_ref[rows, left] + drop_prev),
                  (qt, near_slot, right, lambda rows: near_ref[rows, right]))

    def head_rows(p):
        return slice(p * tq, (p + 1) * tq)

    def key_rows(start, size):
        return pl.ds(pl.multiple_of(start, KEY_TILE), size)

    def sweep_max(kk, sel_add, slot, lanes, bias_of):
        for p in range(NSA_HEADS):
            s = _dot_nt(qbd_ref[head_rows(p), :].astype(BF16), kk) + sel_add[p % 2]
            if bias_of is not None:
                s = s + bias_of(head_rows(p))
            lg_ref[p, slot, :, lanes] = s
            m = mx_ref[head_rows(p), :]
            for t in range(s.shape[1] // KEY_TILE):
                m = jnp.maximum(m, s[:, t * KEY_TILE:(t + 1) * KEY_TILE])
            mx_ref[head_rows(p), :] = m

    def sweep_acc(v, slot, lanes, row_max):
        lo = _iota(v.shape, 1) < HEAD_DIM
        vv = (jnp.where(lo, v, 1.0).astype(BF16), jnp.where(lo, 1.0, v).astype(BF16))
        for p in range(NSA_HEADS):
            pe = jnp.exp(lg_ref[p, slot, :, lanes] - row_max[p]).astype(BF16)
            acc_ref[head_rows(p), :] += _dot(pe, vv[p % 2])

    def sweep(kt, slot, lanes, bias_of, row_max):
        pair = lanes is None
        size = SLC_PAIR if pair else KEY_TILE
        rows = key_rows(kt * size, size)
        lanes = slice(0, SLC_PAIR) if pair else lanes
        if row_max is None:
            sel_add = [(selp_ref if pair else selx_ref)[g, kt] for g in range(NSA_KV_HEADS)]
            sweep_max(kv_ref[rows, 0:NSA_KV_W].astype(BF16), sel_add, slot, lanes, bias_of)
        else:
            sweep_acc(kv_ref[rows, NSA_KV_W:2 * NSA_KV_W], slot, lanes, row_max)

    def all_tiles(row_max):
        lax.fori_loop(0, n_pair, lambda kt, cr: (sweep(kt, kt, None, None, row_max), cr)[1], 0)

        @pl.when(odd_far)
        def _():
            sweep(n_far - 1, odd_slot, left, None, row_max)

        for kt, slot, lanes, bias_of in near_tiles:
            sweep(kt, slot, lanes, bias_of, row_max)

    mx_ref[...] = jnp.full((R, LANES), NEG, F32)
    all_tiles(None)
    row_max = [jnp.max(mx_ref[head_rows(p), :], -1, keepdims=True) for p in range(NSA_HEADS)]
    acc_ref[...] = jnp.zeros((R, LANES), F32)
    all_tiles(row_max)
    acc = acc_ref[...]
    acc_ref[...] = acc / pltpu.roll(acc, HEAD_DIM, 1)
    _assemble_heads(acc_ref, ng_ref, o_ref, tq, 1)


def _sel_attend(z3, sel, keys3, key_colblk, n_keys, near_tab, tq, pos0):
    B, T, _ = z3.shape
    R = NSA_HEADS * tq
    n_tiles = n_keys // KEY_TILE
    assert pos0 % KEY_TILE == 0 and tq == KEY_TILE and 2 * n_tiles <= SEL_LANES
    slot = jnp.arange(LANES)[:, None]
    blk = jnp.arange(n_keys)[None, :] // SEL_BLOCK
    emat = jnp.stack([slot == blk + g * SEL_LANES for g in range(NSA_KV_HEADS)]).astype(BF16)
    q_spec, ng_spec, o_spec = _nsa_common_specs(tq)
    kern = functools.partial(_sel_kernel, tq=tq, tile0=pos0 // KEY_TILE, n_tiles=n_tiles)
    return pl.pallas_call(
        kern, grid=(B, T // tq),
        in_specs=[q_spec,
                  pl.BlockSpec((None, tq, LANES), lambda b, i: (b, i, 0)),
                  pl.BlockSpec((None, n_keys, 2 * NSA_KV_W), lambda b, i: (b, 0, key_colblk)),
                  pl.BlockSpec((R, 2 * KEY_TILE), lambda b, i: (0, 0)),
                  ng_spec,
                  pl.BlockSpec((NSA_KV_HEADS, LANES, n_keys), lambda b, i: (0, 0, 0))],
        out_specs=o_spec,
        out_shape=jax.ShapeDtypeStruct((B, T, MIX_W), F32),
        scratch_shapes=_nsa_scratch(tq) + [pltpu.VMEM((R, LANES), F32),
                                           pltpu.VMEM((NSA_KV_HEADS, n_tiles, tq, KEY_TILE), F32),
                                           pltpu.VMEM((NSA_KV_HEADS, max(n_tiles // 2, 1), tq, SLC_PAIR), F32),
                                           pltpu.VMEM((NSA_HEADS, n_tiles // 2 + 2, tq, SLC_PAIR), F32)],
        compiler_params=_cparams("parallel", "parallel"), name="nsa_slc")(
            z3, sel, keys3, near_tab, z3, emat)


WIN_KEYS = WINDOW + Q_BLK


def _win_kernel(q_ref, kv_ref, bias_ref, ng_ref, o_ref, qbd_ref, acc_ref, *, tq):
    qi = pl.program_id(1)
    _build_qbd(q_ref, qbd_ref, tq)
    n_t = WIN_KEYS // KEY_TILE
    ks, vs, drop = [], [], []
    for j in range(n_t):
        kt = qi - (n_t - 1) + j
        rows = pl.ds(pl.multiple_of(jnp.maximum(kt, 0) * KEY_TILE, KEY_TILE), KEY_TILE)
        ks.append(kv_ref[rows, 0:NSA_KV_W].astype(BF16))
        vs.append(kv_ref[rows, NSA_KV_W:2 * NSA_KV_W].astype(BF16))
        drop.append(((kt >= 0).astype(F32) - 1.0) * (-NEG))
    chunks = [tuple(range(j, min(j + 2, n_t))) for j in range(0, n_t, 2)]
    kc = [ks[c[0]] if len(c) == 1 else jnp.concatenate([ks[j] for j in c], axis=0) for c in chunks]
    vc = [vs[c[0]] if len(c) == 1 else jnp.concatenate([vs[j] for j in c], axis=0) for c in chunks]
    for p in range(NSA_HEADS):
        rows = slice(p * tq, (p + 1) * tq)
        qb = qbd_ref[rows, :].astype(BF16)
        s = []
        for c, kk in zip(chunks, kc):
            bias = [bias_ref[rows, j * KEY_TILE:(j + 1) * KEY_TILE] + drop[j] for j in c]
            s.append(_dot_nt(qb, kk) + (bias[0] if len(c) == 1 else jnp.concatenate(bias, axis=1)))
        lane_max = None
        for sc in s:
            for t in range(sc.shape[1] // KEY_TILE):
                piece = sc[:, t * KEY_TILE:(t + 1) * KEY_TILE]
                lane_max = piece if lane_max is None else jnp.maximum(lane_max, piece)
        row_max = jnp.max(lane_max, -1, keepdims=True)
        lane_sum, acc = None, None
        for sc, vv in zip(s, vc):
            e = jnp.exp(sc - row_max)
            for t in range(e.shape[1] // KEY_TILE):
                piece = e[:, t * KEY_TILE:(t + 1) * KEY_TILE]
                lane_sum = piece if lane_sum is None else lane_sum + piece
            part = _dot(e.astype(BF16), vv)
            acc = part if acc is None else acc + part
        acc_ref[rows, :] = acc / jnp.sum(lane_sum, -1, keepdims=True)
    _assemble_heads(acc_ref, ng_ref, o_ref, tq, 2)


def _win_attend(z3, win_tab, tq):
    B, T, _ = z3.shape
    R = NSA_HEADS * tq
    assert tq == KEY_TILE
    q_spec, ng_spec, o_spec = _nsa_common_specs(tq)
    kern = functools.partial(_win_kernel, tq=tq)
    return pl.pallas_call(
        kern, grid=(B, T // tq),
        in_specs=[q_spec,
                  pl.BlockSpec((None, T, 2 * NSA_KV_W), lambda b, i: (b, 0, _cb("kvw"))),
                  pl.BlockSpec((R, WIN_KEYS), lambda b, i: (0, 0)),
                  ng_spec],
        out_specs=o_spec,
        out_shape=jax.ShapeDtypeStruct((B, T, MIX_W), F32),
        scratch_shapes=_nsa_scratch(tq),
        compiler_params=_cparams("parallel", "parallel"), name="nsa_win")(z3, z3, win_tab, z3)


def _sb_kernel(q_ref, k_ref, v_ref, o_ref, *, tq, heads, tile0):
    qi = pl.program_id(2)
    qt = tile0 + (qi * tq) // KEY_TILE
    sw = heads * HEAD_DIM
    R = heads * tq
    lane_head = lax.shift_right_logical(_iota((tq, sw), 1), int(math.log2(HEAD_DIM)))
    q = q_ref[...] * (HEAD_DIM ** -0.5)
    qbd = jnp.concatenate([jnp.where(lane_head == j, q, 0.0) for j in range(heads)], axis=0).astype(BF16)
    causal = _iota((R, KEY_TILE), 1) < jnp.bitwise_and(_iota((R, KEY_TILE), 0), tq - 1)

    def tile(start, size, run, acc, masked):
        rows = pl.ds(pl.multiple_of(start, KEY_TILE), size)
        after = (_iota((size, size), 0) > _iota((size, size), 1)).astype(BF16)
        z = _dot_nt(qbd, k_ref[rows, :].astype(BF16))
        u = _log_sigmoid(-z)
        if masked:
            u = jnp.where(causal, u, 0.0)
        u_hi = u.astype(BF16)
        u_lo = (u - u_hi.astype(F32)).astype(BF16)
        rest = _dot(u_hi, after) + _dot(u_lo, after)
        a = jnp.exp(u + z + rest + run)
        if masked:
            a = jnp.where(causal, a, 0.0)
        acc = acc + _dot(a.astype(BF16), v_ref[rows, :].astype(BF16))
        return run + jnp.sum(u, -1, keepdims=True), acc

    run, acc = tile(qt * KEY_TILE, KEY_TILE, jnp.zeros((R, 1), F32), jnp.zeros((R, sw), F32), True)

    def live(run):
        return jnp.max(run) > SB_UNDERFLOW

    odd = jnp.bitwise_and(qt, 1)

    def more(st):
        return (st[0] >= 0) & live(st[1])

    def step(st):
        run, acc = tile((odd + 2 * st[0]) * KEY_TILE, 2 * KEY_TILE, st[1], st[2], False)
        return st[0] - 1, run, acc

    _, run, acc = lax.while_loop(more, step, (lax.shift_right_logical(qt, 1) - 1, run, acc))
    run, acc = lax.cond((odd == 1) & live(run),
                        lambda: tile(0, KEY_TILE, run, acc, False),
                        lambda: (run, acc))
    out = None
    for j in range(heads):
        part = jnp.where(lane_head == j, acc[j * tq:(j + 1) * tq], 0.0)
        out = part if out is None else out + part
    o_ref[...] = out


def _sb_attend(z3, keys3, k_colblk0, n_keys, tq, heads, pos0):
    B, T, _ = z3.shape
    sw = heads * HEAD_DIM
    n_slabs = SB_HEADS // heads
    assert pos0 % KEY_TILE == 0 and (tq == KEY_TILE or T == tq) and tq & (tq - 1) == 0
    kern = functools.partial(_sb_kernel, tq=tq, heads=heads, tile0=pos0 // KEY_TILE)
    qcb = _cb("sbq", sw)
    return pl.pallas_call(
        kern, grid=(B, n_slabs, T // tq),
        in_specs=[pl.BlockSpec((None, tq, sw), lambda b, s, i: (b, i, qcb + s)),
                  pl.BlockSpec((None, n_keys, sw), lambda b, s, i: (b, 0, k_colblk0 + s)),
                  pl.BlockSpec((None, n_keys, sw), lambda b, s, i: (b, 0, k_colblk0 + n_slabs + s))],
        out_specs=pl.BlockSpec((None, tq, sw), lambda b, s, i: (b, i, s)),
        out_shape=jax.ShapeDtypeStruct((B, T, MIX_W), F32),
        compiler_params=_cparams("parallel", "parallel", "parallel"), name="sb")(z3, keys3, keys3)


def _new_tile(new_ref, lanes):
    x = new_ref[:, lanes]
    return jnp.concatenate([x, jnp.zeros((KEY_TILE - x.shape[0], x.shape[1]), F32)], axis=0).astype(BF16)


def _sb_tile(z, weigh_values, mask, run_ref, acc_ref):
    after = (_iota((KEY_TILE, KEY_TILE), 0) > _iota((KEY_TILE, KEY_TILE), 1)).astype(BF16)
    u = _log_sigmoid(-z)
    if mask is not None:
        u = jnp.where(mask, u, 0.0)
    u_hi = u.astype(BF16)
    u_lo = (u - u_hi.astype(F32)).astype(BF16)
    a = jnp.exp(u + z + _dot(u_hi, after) + _dot(u_lo, after) + run_ref[...])
    if mask is not None:
        a = jnp.where(mask, a, 0.0)
    acc_ref[...] += weigh_values(a.astype(BF16))
    run_ref[...] += jnp.sum(u, -1, keepdims=True)


def _sb_build_qbd(q_ref, qbd_ref, tq):
    sw = SB_HEADS * HEAD_DIM
    lane_head = lax.shift_right_logical(_iota((tq, sw), 1), int(math.log2(HEAD_DIM)))
    q = q_ref[...] * (HEAD_DIM ** -0.5)
    for h in range(SB_HEADS):
        qbd_ref[h * tq:(h + 1) * tq, :] = jnp.where(lane_head == h, q, 0.0)


def _sb_pages(pages, qbd_ref, run_ref, acc_ref):
    sw = SB_HEADS * HEAD_DIM
    for page in reversed(pages):
        @pl.when(jnp.max(run_ref[...]) > SB_UNDERFLOW)
        def _(page=page):
            kt = page[0].reshape(sw, PAGE_SIZE).astype(BF16)
            vt = page[1].reshape(sw, PAGE_SIZE).astype(BF16)
            _sb_tile(_dot(qbd_ref[...].astype(BF16), kt), lambda a: _dot_nt(a, vt), None, run_ref, acc_ref)


def _sb_lead_kernel(pt_ref, q_ref, new_ref, *refs, tq, n_lead):
    pages = refs[:n_lead]
    run_ref, acc_ref, qbd_ref = refs[n_lead:]
    sw = SB_HEADS * HEAD_DIM
    R = SB_HEADS * tq
    _sb_build_qbd(q_ref, qbd_ref, tq)
    run_ref[...] = jnp.zeros((R, 1), F32)
    acc_ref[...] = jnp.zeros((R, sw), F32)
    k_new = _new_tile(new_ref, slice(0, sw))
    v_new = _new_tile(new_ref, slice(sw, 2 * sw))
    causal = _iota((R, KEY_TILE), 1) < jnp.bitwise_and(_iota((R, KEY_TILE), 0), tq - 1)
    _sb_tile(_dot_nt(qbd_ref[...].astype(BF16), k_new), lambda a: _dot(a, v_new), causal, run_ref, acc_ref)
    _sb_pages(pages, qbd_ref, run_ref, acc_ref)


def _sb_rest_kernel(pt_ref, need_ref, q_ref, run_in_ref, acc_in_ref, *refs, tq, n_rest):
    pages = refs[:n_rest]
    o_ref, qbd_ref, run_ref, acc_ref = refs[n_rest:]
    sw = SB_HEADS * HEAD_DIM
    run_ref[...] = run_in_ref[...]
    acc_ref[...] = acc_in_ref[...]

    @pl.when(need_ref[pl.program_id(0)] > 0)
    def _():
        _sb_build_qbd(q_ref, qbd_ref, tq)
        _sb_pages(pages, qbd_ref, run_ref, acc_ref)

    lane_head = lax.shift_right_logical(_iota((tq, sw), 1), int(math.log2(HEAD_DIM)))
    out = None
    for h in range(SB_HEADS):
        part = jnp.where(lane_head == h, acc_ref[h * tq:(h + 1) * tq, :], 0.0)
        out = part if out is None else out + part
    o_ref[...] = out


SB_LEAD_PAGES = 2


def _sb_paged(z3, pool_t, layer, pt_flat, n_pages):
    B, tq, _ = z3.shape
    sw = SB_HEADS * HEAD_DIM
    R = SB_HEADS * tq
    n_lead = min(SB_LEAD_PAGES, n_pages - 1)
    n_rest = n_pages - n_lead
    assert tq & (tq - 1) == 0 and tq <= KEY_TILE
    page_block = (None, None, 2, SB_HEADS, HEAD_DIM, PAGE_SIZE)
    q_spec = pl.BlockSpec((None, tq, sw), lambda b, *_: (b, 0, _cb("sbq")))
    run_spec = pl.BlockSpec((None, R, 1), lambda b, *_: (b, 0, 0))
    acc_spec = pl.BlockSpec((None, R, sw), lambda b, *_: (b, 0, 0))

    lead_pages = [_page_spec(page_block, layer, n_pages, (lambda i=i: n_rest + i), 0, 0, 0, 0) for i in range(n_lead)]
    run, acc = pl.pallas_call(
        functools.partial(_sb_lead_kernel, tq=tq, n_lead=n_lead),
        grid_spec=pltpu.PrefetchScalarGridSpec(
            num_scalar_prefetch=1, grid=(B,),
            in_specs=[q_spec, pl.BlockSpec((None, tq, 2 * sw), lambda b, pt: (b, 0, _cb("sbkv")))] + lead_pages,
            out_specs=[run_spec, acc_spec],
            scratch_shapes=[pltpu.VMEM((R, sw), F32)]),
        out_shape=[jax.ShapeDtypeStruct((B, R, 1), F32), jax.ShapeDtypeStruct((B, R, sw), F32)],
        compiler_params=_cparams("parallel"), name="sb_lead")(pt_flat, z3, z3, *([pool_t] * n_lead))

    need = (jnp.max(run, axis=(1, 2)) > SB_UNDERFLOW).astype(jnp.int32)

    def rest_page(i):
        def index_map(b, pt, nd):
            return (layer, pt[jnp.where(nd[b] > 0, b * n_pages + i, i)], 0, 0, 0, 0)
        return pl.BlockSpec(page_block, index_map)

    return pl.pallas_call(
        functools.partial(_sb_rest_kernel, tq=tq, n_rest=n_rest),
        grid_spec=pltpu.PrefetchScalarGridSpec(
            num_scalar_prefetch=2, grid=(B,),
            in_specs=[q_spec, run_spec, acc_spec] + [rest_page(i) for i in range(n_rest)],
            out_specs=pl.BlockSpec((None, tq, sw), lambda b, pt, nd: (b, 0, 0)),
            scratch_shapes=[pltpu.VMEM((R, sw), F32), pltpu.VMEM((R, 1), F32), pltpu.VMEM((R, sw), F32)]),
        out_shape=jax.ShapeDtypeStruct((B, tq, MIX_W), F32),
        compiler_params=_cparams("arbitrary"), name="sb_rest")(
            pt_flat, need, z3, run, acc, *([pool_t] * n_rest))


def _sel_paged_kernel(pt_ref, q_ref, sel_ref, new_ref, near_ref, ng_ref, emat_ref, *refs, tq, n_pages):
    pages = refs[:n_pages]
    o_ref, qbd_ref, acc_ref = refs[n_pages:]
    _build_qbd(q_ref, qbd_ref, tq)
    qb = qbd_ref[...].astype(BF16)
    sel_b = sel_ref[...].astype(BF16)
    k_new = _new_tile(new_ref, slice(0, NSA_KV_W))
    v_new = _new_tile(new_ref, slice(NSA_KV_W, 2 * NSA_KV_W))
    n_tiles = n_pages + 1
    logits = []
    for t in range(n_tiles):
        hit = [_dot(sel_b, emat_ref[g, :, t * KEY_TILE:(t + 1) * KEY_TILE]) for g in range(NSA_KV_HEADS)]
        open_ = jnp.concatenate([hit[p % 2] for p in range(NSA_HEADS)], axis=0)
        if t < n_pages:
            s = _dot(qb, pages[t][0].reshape(NSA_KV_W, PAGE_SIZE).astype(BF16))
        else:
            s = _dot_nt(qb, k_new)
        s = s + (open_ - 1.0) * (-NEG)
        if t >= n_pages - 1:
            s = s + near_ref[:, (t - n_pages + 1) * KEY_TILE:(t - n_pages + 2) * KEY_TILE]
        logits.append(s)
    lane_max = logits[0]
    for s in logits[1:]:
        lane_max = jnp.maximum(lane_max, s)
    row_max = jnp.max(lane_max, -1, keepdims=True)
    lane_sum = None
    acc = None
    for t, s in enumerate(logits):
        pe = jnp.exp(s - row_max)
        lane_sum = pe if lane_sum is None else lane_sum + pe
        pb = pe.astype(BF16)
        part = _dot_nt(pb, pages[t][1].reshape(NSA_KV_W, PAGE_SIZE).astype(BF16)) if t < n_pages else _dot(pb, v_new)
        acc = part if acc is None else acc + part
    acc_ref[...] = acc / jnp.sum(lane_sum, -1, keepdims=True)
    _assemble_heads(acc_ref, ng_ref, o_ref, tq, 1)


def _sel_paged(z3, sel, pool_t, layer, pt_flat, n_pages, near_tab):
    B, tq, _ = z3.shape
    R = NSA_HEADS * tq
    n_keys = (n_pages + 1) * KEY_TILE
    assert 2 * (n_pages + 1) <= SEL_LANES and tq <= KEY_TILE
    slot = jnp.arange(LANES)[:, None]
    blk = jnp.arange(n_keys)[None, :] // SEL_BLOCK
    emat = jnp.stack([slot == blk + g * SEL_LANES for g in range(NSA_KV_HEADS)]).astype(BF16)
    kern = functools.partial(_sel_paged_kernel, tq=tq, n_pages=n_pages)
    page_specs = [_page_spec((None, None, 2, NSA_KV_HEADS, HEAD_DIM, PAGE_SIZE), layer, n_pages,
                             (lambda p=p: p), 0, 0, 0, 0) for p in range(n_pages)]
    grid_spec = pltpu.PrefetchScalarGridSpec(
        num_scalar_prefetch=1, grid=(B,),
        in_specs=[pl.BlockSpec((None, tq, MIX_W), lambda b, pt: (b, 0, _cb("nq"))),
                  pl.BlockSpec((None, tq, LANES), lambda b, pt: (b, 0, 0)),
                  pl.BlockSpec((None, tq, 2 * NSA_KV_W), lambda b, pt: (b, 0, _cb("kvs"))),
                  pl.BlockSpec((R, 2 * KEY_TILE), lambda b, pt: (0, 0)),
                  pl.BlockSpec((None, tq, LANES), lambda b, pt: (b, 0, _cb("ng"))),
                  pl.BlockSpec((NSA_KV_HEADS, LANES, n_keys), lambda b, pt: (0, 0, 0))] + page_specs,
        out_specs=pl.BlockSpec((None, tq, MIX_W), lambda b, pt: (b, 0, 0)),
        scratch_shapes=_nsa_scratch(tq))
    return pl.pallas_call(
        kern, grid_spec=grid_spec,
        out_shape=jax.ShapeDtypeStruct((B, tq, MIX_W), F32),
        compiler_params=_cparams("parallel"), name="nsa_slc_paged")(
            pt_flat, z3, sel, z3, near_tab, z3, emat, *([pool_t] * n_pages))


def _win_state_kernel(q_ref, state_ref, new_ref, bias_ref, ng_ref, o_ref, qbd_ref, acc_ref, *, tq):
    _build_qbd(q_ref, qbd_ref, tq)
    qb = qbd_ref[...].astype(BF16)
    kt = state_ref[0].reshape(NSA_KV_W, WINDOW).astype(BF16)
    vt = state_ref[1].reshape(NSA_KV_W, WINDOW).astype(BF16)
    k_new = _new_tile(new_ref, slice(0, NSA_KV_W))
    v_new = _new_tile(new_ref, slice(NSA_KV_W, 2 * NSA_KV_W))
    bias = bias_ref[...]
    s = jnp.concatenate([_dot(qb, kt), _dot_nt(qb, k_new)], axis=1) + bias
    pr = _masked_softmax(s, bias > 0.1 * NEG).astype(BF16)
    acc_ref[...] = _dot_nt(pr[:, 0:WINDOW], vt) + _dot(pr[:, WINDOW:WINDOW + KEY_TILE], v_new)
    _assemble_heads(acc_ref, ng_ref, o_ref, tq, 2)


def _win_state(z3, state_t, layer, win_tab):
    B, tq, _ = z3.shape
    R = NSA_HEADS * tq
    assert WIN_KEYS == WINDOW + KEY_TILE and tq <= KEY_TILE
    kern = functools.partial(_win_state_kernel, tq=tq)
    return pl.pallas_call(
        kern, grid=(B,),
        in_specs=[pl.BlockSpec((None, tq, MIX_W), lambda b: (b, 0, _cb("nq"))),
                  pl.BlockSpec((None, None, 2, NSA_KV_HEADS, HEAD_DIM, WINDOW), lambda b: (layer, b, 0, 0, 0, 0)),
                  pl.BlockSpec((None, tq, 2 * NSA_KV_W), lambda b: (b, 0, _cb("kvw"))),
                  pl.BlockSpec((R, WIN_KEYS), lambda b: (0, 0)),
                  pl.BlockSpec((None, tq, LANES), lambda b: (b, 0, _cb("ng")))],
        out_specs=pl.BlockSpec((None, tq, MIX_W), lambda b: (b, 0, 0)),
        out_shape=jax.ShapeDtypeStruct((B, tq, MIX_W), F32),
        scratch_shapes=_nsa_scratch(tq),
        compiler_params=_cparams("parallel"), name="nsa_win_state")(z3, state_t, z3, win_tab, z3)


def _prep_in_proj(w):
    offs = np.concatenate([[0], np.cumsum(IN_WIDTHS)])
    lead = w.shape[:-1]

    def seg(i):
        return w[..., int(offs[i]):int(offs[i + 1])]

    def pad(x, width):
        return jnp.pad(x, [(0, 0)] * (x.ndim - 1) + [(0, width - x.shape[-1])])

    nq = seg(5).reshape(*lead, 2, 4, HEAD_DIM).swapaxes(-3, -2).reshape(*lead, MIX_W)
    ng = seg(12).reshape(*lead, 3, 2, 4).swapaxes(-2, -1).reshape(*lead, 3 * NSA_HEADS)
    out = jnp.concatenate([seg(16), seg(14), seg(15), seg(2), seg(3), nq, seg(13), seg(0), seg(1),
                           seg(6), seg(7), seg(8), seg(9), seg(10), seg(11), pad(seg(4), LANES), pad(ng, LANES)], -1)
    assert out.shape[-1] == ZW
    return out


def _t5_bucket(dist):
    n = jnp.maximum(dist, 0)
    exact = REL_BUCKETS // 2
    scaled = jnp.log(jnp.maximum(n, 1).astype(F32) / exact) / math.log(REL_MAX_DIST / exact)
    large = jnp.minimum(exact + (scaled * (REL_BUCKETS - exact)).astype(jnp.int32), REL_BUCKETS - 1)
    return jnp.where(n < exact, n, large)


def _bias_rows(rel_slots, dist, valid, tq):
    T, S = dist.shape
    bucket = _t5_bucket(dist).reshape(T // tq, 1, tq, S)
    tab = jnp.zeros((T // tq, NSA_HEADS, tq, S), F32)
    for k in range(REL_BUCKETS):
        tab = jnp.where(bucket == k, rel_slots[k].reshape(1, NSA_HEADS, 1, 1), tab)
    tab = jnp.where(valid.reshape(T // tq, 1, tq, S), tab, NEG)
    return tab.reshape(T // tq, NSA_HEADS * tq, S)


def _group_tables(rel_slots, T, tq, pos0, n_cmp_pad):
    pos_q = pos0 + np.arange(T)
    end = np.arange(n_cmp_pad) * CMP_STRIDE + CMP_BLOCK - 1
    d_cmp = jnp.asarray(pos_q[:, None] - end[None, :], jnp.int32)
    cmp_tab = _bias_rows(rel_slots, d_cmp, d_cmp >= 0, tq)
    t = np.arange(tq)
    d_near = jnp.asarray(KEY_TILE + t[:, None] - np.arange(2 * KEY_TILE)[None, :], jnp.int32)
    far = jnp.repeat(rel_slots[REL_BUCKETS - 1], tq)[:, None]
    near_tab = _bias_rows(rel_slots, d_near, d_near >= 0, tq)[0] - far
    d_win = jnp.asarray(WINDOW + t[:, None] - np.arange(WIN_KEYS)[None, :], jnp.int32)
    win_tab = _bias_rows(rel_slots, d_win, (d_win >= 0) & (d_win <= WINDOW), tq)[0]
    return cmp_tab, near_tab, win_tab


def _cover(n_cmp_pad):
    start = np.arange(n_cmp_pad) * CMP_STRIDE
    blk = np.arange(SEL_LANES) * SEL_BLOCK
    c = ((start[:, None] < blk[None, :] + SEL_BLOCK) & (start[:, None] + CMP_BLOCK > blk[None, :])).astype(np.float32)
    out = np.zeros((2, LANES, n_cmp_pad), np.float32)
    out[0, :SEL_LANES, :] = c.T
    out[1, SEL_LANES:, :] = c.T
    return jnp.asarray(out)


def kernel(x_prompt, x_sample, cache_cmp_kv, cache_slc_kv, cache_sb_kv, state_win_kv, state_gla, page_table,
           rel_bias, w_in, b_in, w_gla_a2, b_gla_a2, gla_norm_g, w_cmp1, b_cmp1, w_cmp2, cmp_pe,
           w_branch, w_o, ln1_g, ln1_b, w_ff_gate, w_ff_up, w_ff_down, ln2_g, ln2_b):
    depth = w_in.shape[0]
    B, T, _ = x_prompt.shape
    DB, DT, _ = x_sample.shape
    n_pages = page_table.shape[1]
    past = n_pages * PAGE_SIZE
    n_win_state = state_win_kv.shape[2]
    assert T % Q_BLK == 0 and DT % SUBLANES == 0 and DT <= Q_BLK and n_win_state == WINDOW and T >= WINDOW

    w_all = _prep_in_proj(w_in).astype(BF16)
    b_all = _prep_in_proj(b_in)[:, None, :]
    wa = jnp.pad(w_gla_a2, ((0, 0), (0, LANES - GLA_GATE_RANK), (0, 0)))
    eye2 = jnp.eye(2, dtype=F32)
    w1bd = jnp.einsum("gG,ljcdh->lcjgdGh", eye2, w_cmp1).reshape(
        depth, 2, 2, CMP_STRIDE * NSA_KV_W, 2 * CMP_HIDDEN).astype(BF16)
    w2bd = jnp.einsum("gG,lchd->lcghGd", eye2, w_cmp2).reshape(depth, 2, 2 * CMP_HIDDEN, NSA_KV_W).astype(BF16)
    pe2 = jnp.broadcast_to(cmp_pe.swapaxes(1, 2)[:, :, :, None, :], (depth, 2, CMP_BLOCK, 2, HEAD_DIM)).reshape(
        depth, 2, CMP_BLOCK, NSA_KV_W)
    b1t = jnp.broadcast_to(b_cmp1[:, :, None, None, :], (depth, 2, 1, 2, CMP_HIDDEN)).reshape(
        depth, 2, 1, 2 * CMP_HIDDEN)
    wb_nsa = w_branch[:, 1].reshape(depth, 2, 4, HEAD_DIM, D_MODEL).swapaxes(1, 2).reshape(depth, MIX_W, D_MODEL)
    wb = jnp.stack([w_branch[:, 0], wb_nsa, w_branch[:, 2]], 1).astype(BF16)
    wo = w_o.astype(BF16)
    wg, wu, wd = w_ff_gate.astype(BF16), w_ff_up.astype(BF16), w_ff_down.astype(BF16)
    rel_slots = rel_bias.reshape(REL_BUCKETS, 2, 4).swapaxes(1, 2).reshape(REL_BUCKETS, NSA_HEADS)

    n_seg_p = T // CMP_STRIDE
    n_seg_s = past // CMP_STRIDE
    tabs_p = _group_tables(rel_slots, T, Q_BLK, 0, n_seg_p)
    tabs_s = _group_tables(rel_slots, DT, DT, past, n_seg_s)
    cover_p, cover_s = _cover(n_seg_p), _cover(n_seg_s)
    pt_flat = page_table.reshape(-1).astype(jnp.int32)
    cmp_pool, slc_pool, sb_pool, win_state_t = (a.transpose(0, 1, 3, 4, 5, 2) for a in
                                                (cache_cmp_kv, cache_slc_kv, cache_sb_kv, state_win_kv))

    xp = x_prompt.reshape(B * T, D_MODEL)
    xs = x_sample.reshape(DB * DT, D_MODEL)
    zero_state = jnp.zeros((B, GLA_HEADS, GLA_DV, GLA_DK), F32)
    outs_p, outs_s = [], []

    def dense_tail(o, z, x, l):
        n = x.shape[0]
        o_gla, o_cmp, o_slc, o_win, o_sb = (a.reshape(n, MIX_W) for a in o)
        x1 = _merge(o_gla, o_cmp, o_slc, o_win, o_sb, z, x, wb[l], wo[l], ln1_g[l][None], ln1_b[l][None])
        return _ffn(x1, wg[l], wu[l], wd[l], ln2_g[l][None], ln2_b[l][None])

    def col(z3, name):
        off, w = COL[name]
        return z3[..., off:off + w]

    for l in range(depth):
        cw = (w1bd[l], pe2[l], b1t[l], w2bd[l])

        z = _linear(xp, w_all[l], b_all[l])
        z3 = z.reshape(B, T, ZW)
        kvc = _compress(z3, _cb("kvc"), T, *cw)
        cmp_tab, near_tab, win_tab = tabs_p
        o_gla, st = _gla(z3, zero_state, wa[l], b_gla_a2[l][None], gla_norm_g[l][None])
        o_cmp, sel = _cmp_attend(z3, kvc, cmp_tab, cover_p, Q_BLK, 0)
        o_slc = _sel_attend(z3, sel, z3, _cb("kvs"), T, near_tab, Q_BLK, 0)
        o_win = _win_attend(z3, win_tab, Q_BLK)
        o_sb = _sb_attend(z3, z3, _cb("sbkv", SB_PROMPT_HEADS * HEAD_DIM), T, Q_BLK, SB_PROMPT_HEADS, 0)
        xp = dense_tail((o_gla, o_cmp, o_slc, o_win, o_sb), z, xp, l)
        outs_p.append((col(z3, "kvc"), col(z3, "kvs"), col(z3, "sbkv"), col(z3, "kvw")[:, T - WINDOW:],
                       st.swapaxes(-1, -2)))

        z = _linear(xs, w_all[l], b_all[l])
        z3 = z.reshape(DB, DT, ZW)
        cmp_tab, near_tab, win_tab = tabs_s
        o_gla, st = _gla(z3, state_gla[l].swapaxes(-1, -2), wa[l], b_gla_a2[l][None], gla_norm_g[l][None])
        kvc = _compress_paged(cmp_pool, l, pt_flat, n_pages, *cw)
        o_cmp, sel = _cmp_attend(z3, kvc, cmp_tab, cover_s, DT, past)
        o_slc = _sel_paged(z3, sel, slc_pool, l, pt_flat, n_pages, near_tab)
        o_win = _win_state(z3, win_state_t, l, win_tab)
        o_sb = _sb_paged(z3, sb_pool, l, pt_flat, n_pages)
        xs = dense_tail((o_gla, o_cmp, o_slc, o_win, o_sb), z, xs, l)
        win_all = jnp.concatenate([state_win_kv[l].reshape(DB, n_win_state, 2 * NSA_KV_W), col(z3, "kvw")], 1)
        outs_s.append((col(z3, "kvc"), col(z3, "kvs"), col(z3, "sbkv"), win_all[:, win_all.shape[1] - WINDOW:],
                       st.swapaxes(-1, -2)))

    def stacked(outs, i, tail):
        a = jnp.stack([o[i] for o in outs])
        return a.reshape(*a.shape[:3], *tail)

    kv_tail = (2, NSA_KV_HEADS, HEAD_DIM)
    sb_tail = (2, SB_HEADS, HEAD_DIM)
    return (xp.reshape(B, T, D_MODEL), xs.reshape(DB, DT, D_MODEL),
            stacked(outs_p, 0, kv_tail), stacked(outs_s, 0, kv_tail),
            stacked(outs_p, 1, kv_tail), stacked(outs_s, 1, kv_tail),
            stacked(outs_p, 2, sb_tail), stacked(outs_s, 2, sb_tail),
            stacked(outs_p, 3, kv_tail), stacked(outs_s, 3, kv_tail),
            jnp.stack([o[4] for o in outs_p]), jnp.stack([o[4] for o in outs_s]))
```

```python
import functools
import math

import jax
import jax.numpy as jnp
import numpy as np
from jax import lax
from jax.experimental import pallas as pl
from jax.experimental.pallas import tpu as pltpu

F32, BF16 = jnp.float32, jnp.bfloat16
HI = lax.Precision.HIGHEST

D_MODEL = 1024
HEAD_DIM = 64
MIX_W = D_MODEL // 2
GLA_HEADS = 4
GLA_DV = MIX_W // GLA_HEADS
GLA_DK = GLA_DV // 2
GLA_GATE_RANK = 16
GLA_TAU = 16.0
GLA_CHUNK = 64
GLA_SUB = 16
NSA_HEADS = MIX_W // HEAD_DIM
NSA_KV_HEADS = 2
NSA_KV_W = NSA_KV_HEADS * HEAD_DIM
CMP_STRIDE = 16
CMP_BLOCK = 32
CMP_HIDDEN = 128
SEL_BLOCK = 64
SEL_TOPN = 16
WINDOW = 512
FORCE_BONUS = 1e4
SB_HEADS = MIX_W // HEAD_DIM
REL_BUCKETS = 32
REL_MAX_DIST = 128
D_FF = -(-8 * D_MODEL // (3 * 256)) * 256
Q_BLK = 128
LN_EPS = 1e-5
TRUNK_DEPTH = 4
DN_ALPHA = (2 * TRUNK_DEPTH) ** 0.25
PAGE_SIZE = 128
GLA_QK_W = GLA_HEADS * GLA_DK
IN_WIDTHS = (GLA_QK_W, GLA_QK_W, MIX_W, MIX_W, GLA_GATE_RANK,
             MIX_W, NSA_KV_W, NSA_KV_W, NSA_KV_W, NSA_KV_W, NSA_KV_W, NSA_KV_W, 3 * NSA_HEADS,
             MIX_W, MIX_W, MIX_W,
             3 * D_MODEL)

LANES = 128
SUBLANES = 8
KEY_TILE = 128
VMEM_LIMIT = 56 * 1024 * 1024
NEG = -1e30
SEL_LANES = 64
SB_UNDERFLOW = -104.0
SB_PROMPT_HEADS = 4
GLA_SAFE_DECAY = 60.0

COL = dict(mg=(0, 3072), sbkv=(3072, 1024), gv=(4096, 512), gr=(4608, 512), nq=(5120, 512),
           sbq=(5632, 512), gqk=(6144, 512), kvc=(6656, 256), kvs=(6912, 256), kvw=(7168, 256),
           ga=(7424, 128), ng=(7552, 128))
ZW = 7680


def _cb(name, width=None):
    off, w = COL[name]
    width = width or w
    assert off % width == 0
    return off // width


def _cparams(*sem):
    return pltpu.CompilerParams(dimension_semantics=sem, vmem_limit_bytes=VMEM_LIMIT)


def _dot(a, b):
    return jnp.dot(a, b, preferred_element_type=F32)


def _dot_hi(a, b):
    return jnp.dot(a, b, precision=HI, preferred_element_type=F32)


def _dot_nt(a, b):
    return lax.dot_general(a, b, (((1,), (1,)), ((), ())), preferred_element_type=F32)


def _dot_tn(a, b):
    return lax.dot_general(a, b, (((0,), (0,)), ((), ())), preferred_element_type=F32)


def _log_sigmoid(x):
    return jnp.minimum(x, 0.0) - jnp.log(1.0 + jnp.exp(-jnp.abs(x)))


def _iota(shape, dim):
    return lax.broadcasted_iota(jnp.int32, shape, dim)


def _pick_tile(n, cands=(1024, 512, 256, 128, 64, 32, 16, 8)):
    for c in cands:
        if n % c == 0:
            return c
    raise ValueError(f"no row tile divides {n}")


IN_PROJ_TN = 1536


def _linear_kernel(x_ref, w_ref, b_ref, o_ref, xb_ref):
    @pl.when(pl.program_id(1) == 0)
    def _():
        xb_ref[...] = x_ref[...].astype(BF16)

    o_ref[...] = _dot(xb_ref[...], w_ref[...]) + b_ref[...]


def _linear(x, w, b):
    n, k = x.shape
    dout = w.shape[1]
    tm, tn = _pick_tile(n), IN_PROJ_TN
    return pl.pallas_call(
        _linear_kernel, grid=(n // tm, dout // tn),
        in_specs=[pl.BlockSpec((tm, k), lambda i, j: (i, 0)),
                  pl.BlockSpec((k, tn), lambda i, j: (0, j)),
                  pl.BlockSpec((1, tn), lambda i, j: (0, j))],
        out_specs=pl.BlockSpec((tm, tn), lambda i, j: (i, j)),
        out_shape=jax.ShapeDtypeStruct((n, dout), F32),
        scratch_shapes=[pltpu.VMEM((tm, k), BF16)],
        compiler_params=_cparams("parallel", "arbitrary"), name="in_proj")(x, w, b)


def _layer_norm(h, g, b):
    mu = jnp.mean(h, -1, keepdims=True)
    d = h - mu
    var = jnp.mean(d * d, -1, keepdims=True)
    return d * lax.rsqrt(var + LN_EPS) * g + b


def _merge_kernel(og_ref, oc_ref, os_ref, ow_ref, ob_ref, mg_ref, x_ref, wb_ref, wo_ref, g_ref, b_ref, o_ref):
    o_nsa = oc_ref[...] + os_ref[...] + ow_ref[...]
    mixed = None
    for br, o in enumerate((og_ref[...], o_nsa, ob_ref[...])):
        proj = _dot(o.astype(BF16), wb_ref[br])
        term = jax.nn.sigmoid(mg_ref[:, br * D_MODEL:(br + 1) * D_MODEL]) * proj
        mixed = term if mixed is None else mixed + term
    y = _dot(mixed.astype(BF16), wo_ref[...])
    o_ref[...] = _layer_norm(DN_ALPHA * x_ref[...] + y, g_ref[...], b_ref[...])


def _merge(o_gla, o_cmp, o_slc, o_win, o_sb, z, x, wb, wo, g, b):
    n = x.shape[0]
    tm = _pick_tile(n, (512, 256, 128, 64, 32, 16, 8))
    mix = pl.BlockSpec((tm, MIX_W), lambda i: (i, 0))
    return pl.pallas_call(
        _merge_kernel, grid=(n // tm,),
        in_specs=[mix, mix, mix, mix, mix,
                  pl.BlockSpec((tm, 3 * D_MODEL), lambda i: (i, _cb("mg"))),
                  pl.BlockSpec((tm, D_MODEL), lambda i: (i, 0)),
                  pl.BlockSpec((3, MIX_W, D_MODEL), lambda i: (0, 0, 0)),
                  pl.BlockSpec((D_MODEL, D_MODEL), lambda i: (0, 0)),
                  pl.BlockSpec((1, D_MODEL), lambda i: (0, 0)),
                  pl.BlockSpec((1, D_MODEL), lambda i: (0, 0))],
        out_specs=pl.BlockSpec((tm, D_MODEL), lambda i: (i, 0)),
        out_shape=jax.ShapeDtypeStruct((n, D_MODEL), F32),
        compiler_params=_cparams("parallel"), name="merge")(o_gla, o_cmp, o_slc, o_win, o_sb, z, x, wb, wo, g, b)


FF_TILE = 1408


def _ffn_kernel(x_ref, wg_ref, wu_ref, wd_ref, g_ref, b_ref, o_ref, xb_ref, acc_ref):
    f = pl.program_id(1)

    @pl.when(f == 0)
    def _():
        xb_ref[...] = x_ref[...].astype(BF16)
        acc_ref[...] = jnp.zeros_like(acc_ref)

    xb = xb_ref[...]
    gate = _dot(xb, wg_ref[...])
    up = _dot(xb, wu_ref[...])
    h = gate * jax.nn.sigmoid(gate) * up
    acc_ref[...] += _dot(h.astype(BF16), wd_ref[...])

    @pl.when(f == pl.num_programs(1) - 1)
    def _():
        o_ref[...] = _layer_norm(DN_ALPHA * x_ref[...] + acc_ref[...], g_ref[...], b_ref[...])


def _ffn(x, wg, wu, wd, g, b):
    n = x.shape[0]
    tm = _pick_tile(n, (512, 256, 128, 64, 32, 16, 8))
    return pl.pallas_call(
        _ffn_kernel, grid=(n // tm, D_FF // FF_TILE),
        in_specs=[pl.BlockSpec((tm, D_MODEL), lambda i, f: (i, 0)),
                  pl.BlockSpec((D_MODEL, FF_TILE), lambda i, f: (0, f)),
                  pl.BlockSpec((D_MODEL, FF_TILE), lambda i, f: (0, f)),
                  pl.BlockSpec((FF_TILE, D_MODEL), lambda i, f: (f, 0)),
                  pl.BlockSpec((1, D_MODEL), lambda i, f: (0, 0)),
                  pl.BlockSpec((1, D_MODEL), lambda i, f: (0, 0))],
        out_specs=pl.BlockSpec((tm, D_MODEL), lambda i, f: (i, 0)),
        out_shape=jax.ShapeDtypeStruct((n, D_MODEL), F32),
        scratch_shapes=[pltpu.VMEM((tm, D_MODEL), BF16), pltpu.VMEM((tm, D_MODEL), F32)],
        compiler_params=_cparams("parallel", "arbitrary"), name="ffn")(x, wg, wu, wd, g, b)


def _gla_kernel(qk_ref, v_ref, r_ref, ga_ref, s0_ref, wa_ref, ba_ref, ng_ref, o_ref, st_ref, *, chunk, n_chunks):
    C = chunk
    rt = C * n_chunks

    @pl.when(pl.program_id(1) == 0)
    def _():
        st_ref[...] = s0_ref[...]

    sub = min(C, GLA_SUB)
    n_sub = C // sub
    causal = _iota((C, C), 1) <= _iota((C, C), 0)
    row_sub = _iota((sub, GLA_DK), 0)
    same_chunk = (lax.shift_right_logical(_iota((rt, rt), 0), int(math.log2(C)))
                  == lax.shift_right_logical(_iota((rt, rt), 1), int(math.log2(C))))
    tril = (same_chunk & (_iota((rt, rt), 1) <= _iota((rt, rt), 0))).astype(F32)
    log_a = _log_sigmoid(_dot_hi(ga_ref[...], wa_ref[...]) + ba_ref[...]) * (1.0 / GLA_TAU)
    cum_all = _dot_hi(tril, log_a)

    def intra_factored(qh, kh, vh, ch):
        qd = qh * jnp.exp(ch)
        kd = kh * jnp.exp(-ch)
        att = jnp.where(causal, _dot_nt(qd.astype(BF16), kd.astype(BF16)), 0.0)
        return _dot(att.astype(BF16), vh.astype(BF16))

    def intra_blocked(qh, kh, vh, ch):
        outs = []
        for sb in range(n_sub):
            a0 = sb * sub
            qs, ks, vs, cs = qh[a0:a0 + sub], kh[a0:a0 + sub], vh[a0:a0 + sub], ch[a0:a0 + sub]
            oi = jnp.zeros((sub, GLA_DV), F32)
            if sb > 0:
                cref = ch[a0 - 1:a0]
                qd = qs * jnp.exp(cs - cref)
                kd = kh[:a0] * jnp.exp(cref - ch[:a0])
                att = _dot_nt(qd.astype(BF16), kd.astype(BF16))
                oi = oi + _dot(att.astype(BF16), vh[:a0].astype(BF16))
            for s in range(sub):
                e = jnp.exp(jnp.minimum(cs - cs[s:s + 1], 0.0))
                w = jnp.sum(jnp.where(row_sub >= s, qs * ks[s:s + 1] * e, 0.0), axis=-1, keepdims=True)
                oi = oi + w * vs[s:s + 1]
            outs.append(oi)
        return outs[0] if n_sub == 1 else jnp.concatenate(outs, axis=0)

    def run_chunk(rows, cum, intra_fn):
        qk = qk_ref[rows, :]
        v = v_ref[rows, :]
        for h in range(GLA_HEADS):
            qh = qk[:, h * GLA_DK:(h + 1) * GLA_DK] * (GLA_DK ** -0.5)
            kh = qk[:, GLA_QK_W + h * GLA_DK:GLA_QK_W + (h + 1) * GLA_DK]
            vh = v[:, h * GLA_DV:(h + 1) * GLA_DV]
            ch = cum[:, h * GLA_DK:(h + 1) * GLA_DK]
            st = st_ref[h]
            o = intra_fn(qh, kh, vh, ch) + _dot_nt((qh * jnp.exp(ch)).astype(BF16), st.astype(BF16))
            mu = jnp.mean(o, -1, keepdims=True)
            d = o - mu
            var = jnp.mean(d * d, -1, keepdims=True)
            lanes = slice(h * GLA_DV, (h + 1) * GLA_DV)
            rr = r_ref[rows, lanes]
            o_ref[rows, lanes] = d * lax.rsqrt(var + LN_EPS) * ng_ref[:, lanes] * (rr * jax.nn.sigmoid(rr))
            last = ch[C - 1:C]
            kdl = kh * jnp.exp(last - ch)
            st_ref[h] = st * jnp.exp(last) + _dot_tn(vh.astype(BF16), kdl.astype(BF16))

    def all_factored():
        for c in range(n_chunks):
            run_chunk(slice(c * C, (c + 1) * C), cum_all[c * C:(c + 1) * C], intra_factored)

    def all_blocked():
        def body(c, carry):
            rows = pl.ds(pl.multiple_of(c * C, C), C)
            chunk_log_a = _log_sigmoid(_dot_hi(ga_ref[rows, :], wa_ref[...]) + ba_ref[...]) * (1.0 / GLA_TAU)
            run_chunk(rows, _dot_hi(causal.astype(F32), chunk_log_a), intra_blocked)
            return carry
        lax.fori_loop(0, n_chunks, body, 0)

    small_decay = jnp.max(-cum_all) <= GLA_SAFE_DECAY
    lax.cond(small_decay, all_factored, all_blocked)


def _gla(z3, s0t, wa, ba, norm_g):
    B, T, _ = z3.shape
    C = math.gcd(T, GLA_CHUNK)
    tt = min(T, 512)
    assert T % tt == 0 and tt % C == 0
    kern = functools.partial(_gla_kernel, chunk=C, n_chunks=tt // C)
    st_spec = pl.BlockSpec((None, GLA_HEADS, GLA_DV, GLA_DK), lambda b, i: (b, 0, 0, 0))
    return pl.pallas_call(
        kern, grid=(B, T // tt),
        in_specs=[pl.BlockSpec((None, tt, 512), lambda b, i: (b, i, _cb("gqk"))),
                  pl.BlockSpec((None, tt, 512), lambda b, i: (b, i, _cb("gv"))),
                  pl.BlockSpec((None, tt, 512), lambda b, i: (b, i, _cb("gr"))),
                  pl.BlockSpec((None, tt, LANES), lambda b, i: (b, i, _cb("ga"))),
                  st_spec,
                  pl.BlockSpec((LANES, GLA_QK_W), lambda b, i: (0, 0)),
                  pl.BlockSpec((1, GLA_QK_W), lambda b, i: (0, 0)),
                  pl.BlockSpec((1, MIX_W), lambda b, i: (0, 0))],
        out_specs=[pl.BlockSpec((None, tt, MIX_W), lambda b, i: (b, i, 0)), st_spec],
        out_shape=[jax.ShapeDtypeStruct((B, T, MIX_W), F32),
                   jax.ShapeDtypeStruct((B, GLA_HEADS, GLA_DV, GLA_DK), F32)],
        compiler_params=_cparams("parallel", "arbitrary"), name="gla")(z3, z3, z3, z3, s0t, wa, ba, norm_g)


def _compress_kernel(rows_ref, w1_ref, pe_ref, b1_ref, w2_ref, o_ref, h1_ref, *, n_seg):
    M = n_seg
    xs = [rows_ref[pl.ds(j, M, stride=CMP_STRIDE), :] for j in range(CMP_STRIDE)]
    halves = []
    for r in range(CMP_BLOCK // CMP_STRIDE):
        xr = jnp.concatenate([(xs[j] + pe_ref[r * CMP_STRIDE + j:r * CMP_STRIDE + j + 1, :]).astype(BF16)
                              for j in range(CMP_STRIDE)], axis=1)
        halves.append(_dot(xr, w1_ref[r]))
    h0, h1 = halves
    h1_ref[0:M, :] = h1
    h1_ref[M:M + SUBLANES, :] = jnp.zeros((SUBLANES, 2 * CMP_HIDDEN), F32)
    h = jax.nn.gelu(b1_ref[...] + h0 + h1_ref[pl.ds(1, M), :])
    o_ref[...] = _dot(h.astype(BF16), w2_ref[...])


def _compress(rows3, colblk, n_rows, w1bd, pe2, b1t, w2bd):
    B = rows3.shape[0]
    n_seg = n_rows // CMP_STRIDE
    kern = functools.partial(_compress_kernel, n_seg=n_seg)
    return pl.pallas_call(
        kern, grid=(B, 2),
        in_specs=[pl.BlockSpec((None, n_rows, NSA_KV_W), lambda b, c: (b, 0, 2 * colblk + c)),
                  pl.BlockSpec((None, 2, CMP_STRIDE * NSA_KV_W, 2 * CMP_HIDDEN), lambda b, c: (c, 0, 0, 0)),
                  pl.BlockSpec((None, CMP_BLOCK, NSA_KV_W), lambda b, c: (c, 0, 0)),
                  pl.BlockSpec((None, 1, 2 * CMP_HIDDEN), lambda b, c: (c, 0, 0)),
                  pl.BlockSpec((None, 2 * CMP_HIDDEN, NSA_KV_W), lambda b, c: (c, 0, 0))],
        out_specs=pl.BlockSpec((None, n_seg, NSA_KV_W), lambda b, c: (b, 0, c)),
        out_shape=jax.ShapeDtypeStruct((B, n_seg, 2 * NSA_KV_W), F32),
        scratch_shapes=[pltpu.VMEM((n_seg + SUBLANES, 2 * CMP_HIDDEN), F32)],
        compiler_params=_cparams("parallel", "parallel"), name="compress")(rows3, w1bd, pe2, b1t, w2bd)


def _compress_paged_kernel(pt_ref, *refs, n_pages):
    pages = refs[:n_pages]
    w1_ref, pe_ref, b1_ref, w2_ref, o_ref, h1_ref, rows_ref = refs[n_pages:]
    for p, page in enumerate(pages):
        rows_ref[p * PAGE_SIZE:(p + 1) * PAGE_SIZE, :] = page[...].reshape(NSA_KV_W, PAGE_SIZE).T
    _compress_kernel(rows_ref, w1_ref, pe_ref, b1_ref, w2_ref, o_ref, h1_ref, n_seg=n_pages * PAGE_SIZE // CMP_STRIDE)


def _page_spec(block, layer, n_pages, page_of, *rest):
    def index_map(b, *idx_and_pt):
        *idx, pt = idx_and_pt
        return (layer, pt[b * n_pages + page_of(*idx)]) + tuple(r(*idx) if callable(r) else r for r in rest)
    return pl.BlockSpec(block, index_map)


def _compress_paged(pool_t, layer, pt_flat, n_pages, w1bd, pe2, b1t, w2bd):
    B = pt_flat.shape[0] // n_pages
    n_seg = n_pages * PAGE_SIZE // CMP_STRIDE
    kern = functools.partial(_compress_paged_kernel, n_pages=n_pages)
    page_specs = [_page_spec((None, None, None, NSA_KV_HEADS, HEAD_DIM, PAGE_SIZE), layer, n_pages,
                             (lambda c, p=p: p), (lambda c: c), 0, 0, 0) for p in range(n_pages)]
    grid_spec = pltpu.PrefetchScalarGridSpec(
        num_scalar_prefetch=1, grid=(B, 2),
        in_specs=page_specs + [
            pl.BlockSpec((None, 2, CMP_STRIDE * NSA_KV_W, 2 * CMP_HIDDEN), lambda b, c, pt: (c, 0, 0, 0)),
            pl.BlockSpec((None, CMP_BLOCK, NSA_KV_W), lambda b, c, pt: (c, 0, 0)),
            pl.BlockSpec((None, 1, 2 * CMP_HIDDEN), lambda b, c, pt: (c, 0, 0)),
            pl.BlockSpec((None, 2 * CMP_HIDDEN, NSA_KV_W), lambda b, c, pt: (c, 0, 0))],
        out_specs=pl.BlockSpec((None, n_seg, NSA_KV_W), lambda b, c, pt: (b, 0, c)),
        scratch_shapes=[pltpu.VMEM((n_seg + SUBLANES, 2 * CMP_HIDDEN), F32),
                        pltpu.VMEM((n_pages * PAGE_SIZE, NSA_KV_W), F32)])
    return pl.pallas_call(
        kern, grid_spec=grid_spec,
        out_shape=jax.ShapeDtypeStruct((B, n_seg, 2 * NSA_KV_W), F32),
        compiler_params=_cparams("parallel", "parallel"), name="compress_paged")(
            pt_flat, *([pool_t] * n_pages), w1bd, pe2, b1t, w2bd)


def _build_qbd(q_ref, qbd_ref, tq):
    lane = _iota((tq, LANES), 1)
    for p in range(NSA_HEADS):
        slab = q_ref[:, (p // 2) * LANES:(p // 2 + 1) * LANES] * (HEAD_DIM ** -0.5)
        keep = (lane >= HEAD_DIM) if p % 2 else (lane < HEAD_DIM)
        qbd_ref[p * tq:(p + 1) * tq, :] = jnp.where(keep, slab, 0.0)


def _assemble_heads(acc_ref, ng_ref, o_ref, tq, branch):
    lo = _iota((tq, LANES), 1) < HEAD_DIM
    for k in range(NSA_HEADS // 2):
        a0 = acc_ref[(2 * k) * tq:(2 * k + 1) * tq, :]
        a1 = acc_ref[(2 * k + 1) * tq:(2 * k + 2) * tq, :]
        c0 = branch * NSA_HEADS + 2 * k
        g0 = jax.nn.sigmoid(ng_ref[:, c0:c0 + 1])
        g1 = jax.nn.sigmoid(ng_ref[:, c0 + 1:c0 + 2])
        o_ref[:, k * LANES:(k + 1) * LANES] = jnp.where(lo, a0 * g0, a1 * g1)


def _masked_softmax(s, valid):
    s = jnp.where(valid, s, NEG)
    m = jnp.max(s, -1, keepdims=True)
    e = jnp.where(valid, jnp.exp(s - m), 0.0)
    return e / jnp.maximum(jnp.sum(e, -1, keepdims=True), 1e-30)


def _dot_nt_hi(a, b):
    return lax.dot_general(a, b, (((1,), (1,)), ((), ())), precision=HI, preferred_element_type=F32)


def _select_blocks(score, blk, cur, slot_axis):
    forced = (blk == 0) | (blk == cur) | (blk == cur - 1)
    score = jnp.where(forced, score + FORCE_BONUS, score)
    visible = blk <= cur
    score = jnp.where(visible, score, -jnp.inf)
    first_group = _iota(score.shape, slot_axis) < SEL_LANES
    rank = jnp.zeros(score.shape, F32)
    for mp in range(SEL_LANES):
        if slot_axis == 0:
            other = jnp.where(first_group, score[mp:mp + 1, :], score[SEL_LANES + mp:SEL_LANES + mp + 1, :])
        else:
            other = jnp.where(first_group, score[:, mp:mp + 1], score[:, SEL_LANES + mp:SEL_LANES + mp + 1])
        tie = jnp.where(blk > mp, 1.0, 0.0)
        rank = rank + jnp.where(other > score, 1.0, jnp.where(other == score, tie, 0.0))
    return jnp.where(visible, jnp.where(rank < SEL_TOPN, 1.0, 0.0), 0.0)


def _cmp_kernel(q_ref, kvc_ref, bias_ref, covt_ref, ng_ref, o_ref, sel_ref, qbd_ref, acc_ref, *, tq, rc, pos0):
    qi = pl.program_id(1)
    _build_qbd(q_ref, qbd_ref, tq)
    n = kvc_ref.shape[0]
    kc = kvc_ref[:, 0:NSA_KV_W].astype(BF16)
    vc = kvc_ref[:, NSA_KV_W:2 * NSA_KV_W].astype(BF16)
    ones = jnp.ones((n, LANES), BF16)
    heads_per_chunk = rc // tq
    p_sum = [jnp.zeros((tq, n), F32), jnp.zeros((tq, n), F32)]
    for c in range(NSA_HEADS * tq // rc):
        rows = slice(c * rc, (c + 1) * rc)
        s = _dot_nt(qbd_ref[rows, :].astype(BF16), kc) + bias_ref[rows, :]
        lane_max = s[:, 0:LANES]
        for t in range(1, n // LANES):
            lane_max = jnp.maximum(lane_max, s[:, t * LANES:(t + 1) * LANES])
        row_max = jnp.max(lane_max, -1, keepdims=True)
        e = jnp.exp(s - row_max)
        e_hi = e.astype(BF16)
        e_lo = (e - e_hi.astype(F32)).astype(BF16)
        row_sum = _dot(e_hi, ones) + _dot(e_lo, ones)
        scale = jnp.where(row_max > 0.5 * NEG, 1.0 / row_sum, 0.0)
        pr = jnp.concatenate([e[:, t * LANES:(t + 1) * LANES] * scale for t in range(n // LANES)], axis=1)
        acc_ref[rows, :] = _dot(pr.astype(BF16), vc)
        for pp in range(heads_per_chunk):
            g = (c * heads_per_chunk + pp) % 2
            p_sum[g] = p_sum[g] + pr[pp * tq:(pp + 1) * tq]
    _assemble_heads(acc_ref, ng_ref, o_ref, tq, 0)

    shift = int(math.log2(SEL_BLOCK))
    if tq == LANES:
        score_t = _dot_nt_hi(covt_ref[0], p_sum[0]) + _dot_nt_hi(covt_ref[1], p_sum[1])
        blk = jnp.bitwise_and(_iota((LANES, tq), 0), SEL_LANES - 1)
        cur = lax.shift_right_logical(pos0 + qi * tq + _iota((LANES, tq), 1), shift)
        sel_t = _select_blocks(score_t, blk, cur, 0)
        eye = (_iota((LANES, LANES), 0) == _iota((LANES, LANES), 1)).astype(BF16)
        sel_ref[...] = _dot_tn(sel_t.astype(BF16), eye)
    else:
        score = _dot_nt_hi(p_sum[0], covt_ref[0]) + _dot_nt_hi(p_sum[1], covt_ref[1])
        blk = jnp.bitwise_and(_iota((tq, LANES), 1), SEL_LANES - 1)
        cur = lax.shift_right_logical(pos0 + qi * tq + _iota((tq, LANES), 0), shift)
        sel_ref[...] = _select_blocks(score, blk, cur, 1)


def _nsa_common_specs(tq):
    q_spec = pl.BlockSpec((None, tq, MIX_W), lambda b, i: (b, i, _cb("nq")))
    ng_spec = pl.BlockSpec((None, tq, LANES), lambda b, i: (b, i, _cb("ng")))
    o_spec = pl.BlockSpec((None, tq, MIX_W), lambda b, i: (b, i, 0))
    return q_spec, ng_spec, o_spec


def _nsa_scratch(tq):
    return [pltpu.VMEM((NSA_HEADS * tq, LANES), F32), pltpu.VMEM((NSA_HEADS * tq, LANES), F32)]


def _cmp_attend(z3, kvc, bias_tab, cover2, tq, pos0):
    B, T, _ = z3.shape
    n = kvc.shape[1]
    R = NSA_HEADS * tq
    rc = min(R, 128)
    assert n % LANES == 0
    q_spec, ng_spec, o_spec = _nsa_common_specs(tq)
    kern = functools.partial(_cmp_kernel, tq=tq, rc=rc, pos0=pos0)
    return pl.pallas_call(
        kern, grid=(B, T // tq),
        in_specs=[q_spec,
                  pl.BlockSpec((None, n, 2 * NSA_KV_W), lambda b, i: (b, 0, 0)),
                  pl.BlockSpec((None, R, n), lambda b, i: (i, 0, 0)),
                  pl.BlockSpec((2, LANES, n), lambda b, i: (0, 0, 0)),
                  ng_spec],
        out_specs=[o_spec, pl.BlockSpec((None, tq, LANES), lambda b, i: (b, i, 0))],
        out_shape=[jax.ShapeDtypeStruct((B, T, MIX_W), F32), jax.ShapeDtypeStruct((B, T, LANES), F32)],
        scratch_shapes=_nsa_scratch(tq),
        compiler_params=_cparams("parallel", "parallel"), name="nsa_cmp")(z3, kvc, bias_tab, cover2, z3)


SLC_PAIR = 2 * KEY_TILE


def _sel_kernel(q_ref, sel_ref, kv_ref, near_ref, ng_ref, emat_ref, o_ref,
                qbd_ref, acc_ref, mx_ref, selx_ref, selp_ref, lg_ref, *, tq, tile0, n_tiles):
    qi = pl.program_id(1)
    qt = tile0 + (qi * tq) // KEY_TILE
    R = NSA_HEADS * tq
    _build_qbd(q_ref, qbd_ref, tq)
    sel_b = sel_ref[...].astype(BF16)
    for t in range(n_tiles):
        @pl.when(t <= qt)
        def _(t=t):
            for g in range(NSA_KV_HEADS):
                hit = _dot(sel_b, emat_ref[g, :, t * KEY_TILE:(t + 1) * KEY_TILE])
                selx_ref[g, t] = (hit - 1.0) * (-NEG)
                if t // 2 < n_tiles // 2:
                    selp_ref[g, t // 2, :, (t % 2) * KEY_TILE:(t % 2 + 1) * KEY_TILE] = (hit - 1.0) * (-NEG)
    n_far = jnp.maximum(qt - 1, 0)
    n_pair = lax.shift_right_logical(n_far, 1)
    odd_far = jnp.bitwise_and(n_far, 1) == 1
    drop_prev = ((qt >= 1).astype(F32) - 1.0) * (-NEG)
    odd_slot, near_slot = n_tiles // 2, n_tiles // 2 + 1
    left, right = slice(0, KEY_TILE), slice(KEY_TILE, SLC_PAIR)
    near_tiles = ((n_far, near_slot, left, lambda rows: near_ref[rows, left] + drop_prev),
                  (qt, near_slot, right, lambda rows: near_ref[rows, right]))

    def head_rows(p):
        return slice(p * tq, (p + 1) * tq)

    def key_rows(start, size):
        return pl.ds(pl.multiple_of(start, KEY_TILE), size)

    def sweep_max(kk, sel_add, slot, lanes, bias_of):
        for p in range(NSA_HEADS):
            s = _dot_nt(qbd_ref[head_rows(p), :].astype(BF16), kk) + sel_add[p % 2]
            if bias_of is not None:
                s = s + bias_of(head_rows(p))
            lg_ref[p, slot, :, lanes] = s
            m = mx_ref[head_rows(p), :]
            for t in range(s.shape[1] // KEY_TILE):
                m = jnp.maximum(m, s[:, t * KEY_TILE:(t + 1) * KEY_TILE])
            mx_ref[head_rows(p), :] = m

    def sweep_acc(v, slot, lanes, row_max):
        lo = _iota(v.shape, 1) < HEAD_DIM
        vv = (jnp.where(lo, v, 1.0).astype(BF16), jnp.where(lo, 1.0, v).astype(BF16))
        for p in range(NSA_HEADS):
            pe = jnp.exp(lg_ref[p, slot, :, lanes] - row_max[p]).astype(BF16)
            acc_ref[head_rows(p), :] += _dot(pe, vv[p % 2])

    def sweep(kt, slot, lanes, bias_of, row_max):
        pair = lanes is None
        size = SLC_PAIR if pair else KEY_TILE
        rows = key_rows(kt * size, size)
        lanes = slice(0, SLC_PAIR) if pair else lanes
        if row_max is None:
            sel_add = [(selp_ref if pair else selx_ref)[g, kt] for g in range(NSA_KV_HEADS)]
            sweep_max(kv_ref[rows, 0:NSA_KV_W].astype(BF16), sel_add, slot, lanes, bias_of)
        else:
            sweep_acc(kv_ref[rows, NSA_KV_W:2 * NSA_KV_W], slot, lanes, row_max)

    def all_tiles(row_max):
        lax.fori_loop(0, n_pair, lambda kt, cr: (sweep(kt, kt, None, None, row_max), cr)[1], 0)

        @pl.when(odd_far)
        def _():
            sweep(n_far - 1, odd_slot, left, None, row_max)

        for kt, slot, lanes, bias_of in near_tiles:
            sweep(kt, slot, lanes, bias_of, row_max)

    mx_ref[...] = jnp.full((R, LANES), NEG, F32)
    all_tiles(None)
    row_max = [jnp.max(mx_ref[head_rows(p), :], -1, keepdims=True) for p in range(NSA_HEADS)]
    acc_ref[...] = jnp.zeros((R, LANES), F32)
    all_tiles(row_max)
    acc = acc_ref[...]
    acc_ref[...] = acc / pltpu.roll(acc, HEAD_DIM, 1)
    _assemble_heads(acc_ref, ng_ref, o_ref, tq, 1)


def _sel_attend(z3, sel, keys3, key_colblk, n_keys, near_tab, tq, pos0):
    B, T, _ = z3.shape
    R = NSA_HEADS * tq
    n_tiles = n_keys // KEY_TILE
    assert pos0 % KEY_TILE == 0 and tq == KEY_TILE and 2 * n_tiles <= SEL_LANES
    slot = jnp.arange(LANES)[:, None]
    blk = jnp.arange(n_keys)[None, :] // SEL_BLOCK
    emat = jnp.stack([slot == blk + g * SEL_LANES for g in range(NSA_KV_HEADS)]).astype(BF16)
    q_spec, ng_spec, o_spec = _nsa_common_specs(tq)
    kern = functools.partial(_sel_kernel, tq=tq, tile0=pos0 // KEY_TILE, n_tiles=n_tiles)
    return pl.pallas_call(
        kern, grid=(B, T // tq),
        in_specs=[q_spec,
                  pl.BlockSpec((None, tq, LANES), lambda b, i: (b, i, 0)),
                  pl.BlockSpec((None, n_keys, 2 * NSA_KV_W), lambda b, i: (b, 0, key_colblk)),
                  pl.BlockSpec((R, 2 * KEY_TILE), lambda b, i: (0, 0)),
                  ng_spec,
                  pl.BlockSpec((NSA_KV_HEADS, LANES, n_keys), lambda b, i: (0, 0, 0))],
        out_specs=o_spec,
        out_shape=jax.ShapeDtypeStruct((B, T, MIX_W), F32),
        scratch_shapes=_nsa_scratch(tq) + [pltpu.VMEM((R, LANES), F32),
                                           pltpu.VMEM((NSA_KV_HEADS, n_tiles, tq, KEY_TILE), F32),
                                           pltpu.VMEM((NSA_KV_HEADS, max(n_tiles // 2, 1), tq, SLC_PAIR), F32),
                                           pltpu.VMEM((NSA_HEADS, n_tiles // 2 + 2, tq, SLC_PAIR), F32)],
        compiler_params=_cparams("parallel", "parallel"), name="nsa_slc")(
            z3, sel, keys3, near_tab, z3, emat)


WIN_KEYS = WINDOW + Q_BLK


def _win_kernel(q_ref, kv_ref, bias_ref, ng_ref, o_ref, qbd_ref, acc_ref, *, tq):
    qi = pl.program_id(1)
    _build_qbd(q_ref, qbd_ref, tq)
    n_t = WIN_KEYS // KEY_TILE
    ks, vs, drop = [], [], []
    for j in range(n_t):
        kt = qi - (n_t - 1) + j
        rows = pl.ds(pl.multiple_of(jnp.maximum(kt, 0) * KEY_TILE, KEY_TILE), KEY_TILE)
        ks.append(kv_ref[rows, 0:NSA_KV_W].astype(BF16))
        vs.append(kv_ref[rows, NSA_KV_W:2 * NSA_KV_W].astype(BF16))
        drop.append(((kt >= 0).astype(F32) - 1.0) * (-NEG))
    chunks = [tuple(range(j, min(j + 2, n_t))) for j in range(0, n_t, 2)]
    kc = [ks[c[0]] if len(c) == 1 else jnp.concatenate([ks[j] for j in c], axis=0) for c in chunks]
    vc = [vs[c[0]] if len(c) == 1 else jnp.concatenate([vs[j] for j in c], axis=0) for c in chunks]
    for p in range(NSA_HEADS):
        rows = slice(p * tq, (p + 1) * tq)
        qb = qbd_ref[rows, :].astype(BF16)
        s = []
        for c, kk in zip(chunks, kc):
            bias = [bias_ref[rows, j * KEY_TILE:(j + 1) * KEY_TILE] + drop[j] for j in c]
            s.append(_dot_nt(qb, kk) + (bias[0] if len(c) == 1 else jnp.concatenate(bias, axis=1)))
        lane_max = None
        for sc in s:
            for t in range(sc.shape[1] // KEY_TILE):
                piece = sc[:, t * KEY_TILE:(t + 1) * KEY_TILE]
                lane_max = piece if lane_max is None else jnp.maximum(lane_max, piece)
        row_max = jnp.max(lane_max, -1, keepdims=True)
        lane_sum, acc = None, None
        for sc, vv in zip(s, vc):
            e = jnp.exp(sc - row_max)
            for t in range(e.shape[1] // KEY_TILE):
                piece = e[:, t * KEY_TILE:(t + 1) * KEY_TILE]
                lane_sum = piece if lane_sum is None else lane_sum + piece
            part = _dot(e.astype(BF16), vv)
            acc = part if acc is None else acc + part
        acc_ref[rows, :] = acc / jnp.sum(lane_sum, -1, keepdims=True)
    _assemble_heads(acc_ref, ng_ref, o_ref, tq, 2)


def _win_attend(z3, win_tab, tq):
    B, T, _ = z3.shape
    R = NSA_HEADS * tq
    assert tq == KEY_TILE
    q_spec, ng_spec, o_spec = _nsa_common_specs(tq)
    kern = functools.partial(_win_kernel, tq=tq)
    return pl.pallas_call(
        kern, grid=(B, T // tq),
        in_specs=[q_spec,
                  pl.BlockSpec((None, T, 2 * NSA_KV_W), lambda b, i: (b, 0, _cb("kvw"))),
                  pl.BlockSpec((R, WIN_KEYS), lambda b, i: (0, 0)),
                  ng_spec],
        out_specs=o_spec,
        out_shape=jax.ShapeDtypeStruct((B, T, MIX_W), F32),
        scratch_shapes=_nsa_scratch(tq),
        compiler_params=_cparams("parallel", "parallel"), name="nsa_win")(z3, z3, win_tab, z3)


def _sb_kernel(q_ref, k_ref, v_ref, o_ref, *, tq, heads, tile0):
    qi = pl.program_id(2)
    qt = tile0 + (qi * tq) // KEY_TILE
    sw = heads * HEAD_DIM
    R = heads * tq
    lane_head = lax.shift_right_logical(_iota((tq, sw), 1), int(math.log2(HEAD_DIM)))
    q = q_ref[...] * (HEAD_DIM ** -0.5)
    qbd = jnp.concatenate([jnp.where(lane_head == j, q, 0.0) for j in range(heads)], axis=0).astype(BF16)
    causal = _iota((R, KEY_TILE), 1) < jnp.bitwise_and(_iota((R, KEY_TILE), 0), tq - 1)

    def tile(start, size, run, acc, masked):
        rows = pl.ds(pl.multiple_of(start, KEY_TILE), size)
        after = (_iota((size, size), 0) > _iota((size, size), 1)).astype(BF16)
        z = _dot_nt(qbd, k_ref[rows, :].astype(BF16))
        u = _log_sigmoid(-z)
        if masked:
            u = jnp.where(causal, u, 0.0)
        u_hi = u.astype(BF16)
        u_lo = (u - u_hi.astype(F32)).astype(BF16)
        rest = _dot(u_hi, after) + _dot(u_lo, after)
        a = jnp.exp(u + z + rest + run)
        if masked:
            a = jnp.where(causal, a, 0.0)
        acc = acc + _dot(a.astype(BF16), v_ref[rows, :].astype(BF16))
        return run + jnp.sum(u, -1, keepdims=True), acc

    run, acc = tile(qt * KEY_TILE, KEY_TILE, jnp.zeros((R, 1), F32), jnp.zeros((R, sw), F32), True)

    def live(run):
        return jnp.max(run) > SB_UNDERFLOW

    odd = jnp.bitwise_and(qt, 1)

    def more(st):
        return (st[0] >= 0) & live(st[1])

    def step(st):
        run, acc = tile((odd + 2 * st[0]) * KEY_TILE, 2 * KEY_TILE, st[1], st[2], False)
        return st[0] - 1, run, acc

    _, run, acc = lax.while_loop(more, step, (lax.shift_right_logical(qt, 1) - 1, run, acc))
    run, acc = lax.cond((odd == 1) & live(run),
                        lambda: tile(0, KEY_TILE, run, acc, False),
                        lambda: (run, acc))
    out = None
    for j in range(heads):
        part = jnp.where(lane_head == j, acc[j * tq:(j + 1) * tq], 0.0)
        out = part if out is None else out + part
    o_ref[...] = out


def _sb_attend(z3, keys3, k_colblk0, n_keys, tq, heads, pos0):
    B, T, _ = z3.shape
    sw = heads * HEAD_DIM
    n_slabs = SB_HEADS // heads
    assert pos0 % KEY_TILE == 0 and (tq == KEY_TILE or T == tq) and tq & (tq - 1) == 0
    kern = functools.partial(_sb_kernel, tq=tq, heads=heads, tile0=pos0 // KEY_TILE)
    qcb = _cb("sbq", sw)
    return pl.pallas_call(
        kern, grid=(B, n_slabs, T // tq),
        in_specs=[pl.BlockSpec((None, tq, sw), lambda b, s, i: (b, i, qcb + s)),
                  pl.BlockSpec((None, n_keys, sw), lambda b, s, i: (b, 0, k_colblk0 + s)),
                  pl.BlockSpec((None, n_keys, sw), lambda b, s, i: (b, 0, k_colblk0 + n_slabs + s))],
        out_specs=pl.BlockSpec((None, tq, sw), lambda b, s, i: (b, i, s)),
        out_shape=jax.ShapeDtypeStruct((B, T, MIX_W), F32),
        compiler_params=_cparams("parallel", "parallel", "parallel"), name="sb")(z3, keys3, keys3)


def _new_tile(new_ref, lanes):
    x = new_ref[:, lanes]
    return jnp.concatenate([x, jnp.zeros((KEY_TILE - x.shape[0], x.shape[1]), F32)], axis=0).astype(BF16)


def _sb_tile(z, weigh_values, mask, run_ref, acc_ref):
    after = (_iota((KEY_TILE, KEY_TILE), 0) > _iota((KEY_TILE, KEY_TILE), 1)).astype(BF16)
    u = _log_sigmoid(-z)
    if mask is not None:
        u = jnp.where(mask, u, 0.0)
    u_hi = u.astype(BF16)
    u_lo = (u - u_hi.astype(F32)).astype(BF16)
    a = jnp.exp(u + z + _dot(u_hi, after) + _dot(u_lo, after) + run_ref[...])
    if mask is not None:
        a = jnp.where(mask, a, 0.0)
    acc_ref[...] += weigh_values(a.astype(BF16))
    run_ref[...] += jnp.sum(u, -1, keepdims=True)


def _sb_build_qbd(q_ref, qbd_ref, tq):
    sw = SB_HEADS * HEAD_DIM
    lane_head = lax.shift_right_logical(_iota((tq, sw), 1), int(math.log2(HEAD_DIM)))
    q = q_ref[...] * (HEAD_DIM ** -0.5)
    for h in range(SB_HEADS):
        qbd_ref[h * tq:(h + 1) * tq, :] = jnp.where(lane_head == h, q, 0.0)


def _sb_pages(pages, qbd_ref, run_ref, acc_ref):
    sw = SB_HEADS * HEAD_DIM
    for page in reversed(pages):
        @pl.when(jnp.max(run_ref[...]) > SB_UNDERFLOW)
        def _(page=page):
            kt = page[0].reshape(sw, PAGE_SIZE).astype(BF16)
            vt = page[1].reshape(sw, PAGE_SIZE).astype(BF16)
            _sb_tile(_dot(qbd_ref[...].astype(BF16), kt), lambda a: _dot_nt(a, vt), None, run_ref, acc_ref)


def _sb_lead_kernel(pt_ref, q_ref, new_ref, *refs, tq, n_lead):
    pages = refs[:n_lead]
    run_ref, acc_ref, qbd_ref = refs[n_lead:]
    sw = SB_HEADS * HEAD_DIM
    R = SB_HEADS * tq
    _sb_build_qbd(q_ref, qbd_ref, tq)
    run_ref[...] = jnp.zeros((R, 1), F32)
    acc_ref[...] = jnp.zeros((R, sw), F32)
    k_new = _new_tile(new_ref, slice(0, sw))
    v_new = _new_tile(new_ref, slice(sw, 2 * sw))
    causal = _iota((R, KEY_TILE), 1) < jnp.bitwise_and(_iota((R, KEY_TILE), 0), tq - 1)
    _sb_tile(_dot_nt(qbd_ref[...].astype(BF16), k_new), lambda a: _dot(a, v_new), causal, run_ref, acc_ref)
    _sb_pages(pages, qbd_ref, run_ref, acc_ref)


def _sb_rest_kernel(pt_ref, need_ref, q_ref, run_in_ref, acc_in_ref, *refs, tq, n_rest):
    pages = refs[:n_rest]
    o_ref, qbd_ref, run_ref, acc_ref = refs[n_rest:]
    sw = SB_HEADS * HEAD_DIM
    run_ref[...] = run_in_ref[...]
    acc_ref[...] = acc_in_ref[...]

    @pl.when(need_ref[pl.program_id(0)] > 0)
    def _():
        _sb_build_qbd(q_ref, qbd_ref, tq)
        _sb_pages(pages, qbd_ref, run_ref, acc_ref)

    lane_head = lax.shift_right_logical(_iota((tq, sw), 1), int(math.log2(HEAD_DIM)))
    out = None
    for h in range(SB_HEADS):
        part = jnp.where(lane_head == h, acc_ref[h * tq:(h + 1) * tq, :], 0.0)
        out = part if out is None else out + part
    o_ref[...] = out


SB_LEAD_PAGES = 2


def _sb_paged(z3, pool_t, layer, pt_flat, n_pages):
    B, tq, _ = z3.shape
    sw = SB_HEADS * HEAD_DIM
    R = SB_HEADS * tq
    n_lead = min(SB_LEAD_PAGES, n_pages - 1)
    n_rest = n_pages - n_lead
    assert tq & (tq - 1) == 0 and tq <= KEY_TILE
    page_block = (None, None, 2, SB_HEADS, HEAD_DIM, PAGE_SIZE)
    q_spec = pl.BlockSpec((None, tq, sw), lambda b, *_: (b, 0, _cb("sbq")))
    run_spec = pl.BlockSpec((None, R, 1), lambda b, *_: (b, 0, 0))
    acc_spec = pl.BlockSpec((None, R, sw), lambda b, *_: (b, 0, 0))

    lead_pages = [_page_spec(page_block, layer, n_pages, (lambda i=i: n_rest + i), 0, 0, 0, 0) for i in range(n_lead)]
    run, acc = pl.pallas_call(
        functools.partial(_sb_lead_kernel, tq=tq, n_lead=n_lead),
        grid_spec=pltpu.PrefetchScalarGridSpec(
            num_scalar_prefetch=1, grid=(B,),
            in_specs=[q_spec, pl.BlockSpec((None, tq, 2 * sw), lambda b, pt: (b, 0, _cb("sbkv")))] + lead_pages,
            out_specs=[run_spec, acc_spec],
            scratch_shapes=[pltpu.VMEM((R, sw), F32)]),
        out_shape=[jax.ShapeDtypeStruct((B, R, 1), F32), jax.ShapeDtypeStruct((B, R, sw), F32)],
        compiler_params=_cparams("parallel"), name="sb_lead")(pt_flat, z3, z3, *([pool_t] * n_lead))

    need = (jnp.max(run, axis=(1, 2)) > SB_UNDERFLOW).astype(jnp.int32)

    def rest_page(i):
        def index_map(b, pt, nd):
            return (layer, pt[jnp.where(nd[b] > 0, b * n_pages + i, i)], 0, 0, 0, 0)
        return pl.BlockSpec(page_block, index_map)

    return pl.pallas_call(
        functools.partial(_sb_rest_kernel, tq=tq, n_rest=n_rest),
        grid_spec=pltpu.PrefetchScalarGridSpec(
            num_scalar_prefetch=2, grid=(B,),
            in_specs=[q_spec, run_spec, acc_spec] + [rest_page(i) for i in range(n_rest)],
            out_specs=pl.BlockSpec((None, tq, sw), lambda b, pt, nd: (b, 0, 0)),
            scratch_shapes=[pltpu.VMEM((R, sw), F32), pltpu.VMEM((R, 1), F32), pltpu.VMEM((R, sw), F32)]),
        out_shape=jax.ShapeDtypeStruct((B, tq, MIX_W), F32),
        compiler_params=_cparams("arbitrary"), name="sb_rest")(
            pt_flat, need, z3, run, acc, *([pool_t] * n_rest))


def _sel_paged_kernel(pt_ref, q_ref, sel_ref, new_ref, near_ref, ng_ref, emat_ref, *refs, tq, n_pages):
    pages = refs[:n_pages]
    o_ref, qbd_ref, acc_ref = refs[n_pages:]
    _build_qbd(q_ref, qbd_ref, tq)
    qb = qbd_ref[...].astype(BF16)
    sel_b = sel_ref[...].astype(BF16)
    k_new = _new_tile(new_ref, slice(0, NSA_KV_W))
    v_new = _new_tile(new_ref, slice(NSA_KV_W, 2 * NSA_KV_W))
    n_tiles = n_pages + 1
    logits = []
    for t in range(n_tiles):
        hit = [_dot(sel_b, emat_ref[g, :, t * KEY_TILE:(t + 1) * KEY_TILE]) for g in range(NSA_KV_HEADS)]
        open_ = jnp.concatenate([hit[p % 2] for p in range(NSA_HEADS)], axis=0)
        if t < n_pages:
            s = _dot(qb, pages[t][0].reshape(NSA_KV_W, PAGE_SIZE).astype(BF16))
        else:
            s = _dot_nt(qb, k_new)
        s = s + (open_ - 1.0) * (-NEG)
        if t >= n_pages - 1:
            s = s + near_ref[:, (t - n_pages + 1) * KEY_TILE:(t - n_pages + 2) * KEY_TILE]
        logits.append(s)
    lane_max = logits[0]
    for s in logits[1:]:
        lane_max = jnp.maximum(lane_max, s)
    row_max = jnp.max(lane_max, -1, keepdims=True)
    lane_sum = None
    acc = None
    for t, s in enumerate(logits):
        pe = jnp.exp(s - row_max)
        lane_sum = pe if lane_sum is None else lane_sum + pe
        pb = pe.astype(BF16)
        part = _dot_nt(pb, pages[t][1].reshape(NSA_KV_W, PAGE_SIZE).astype(BF16)) if t < n_pages else _dot(pb, v_new)
        acc = part if acc is None else acc + part
    acc_ref[...] = acc / jnp.sum(lane_sum, -1, keepdims=True)
    _assemble_heads(acc_ref, ng_ref, o_ref, tq, 1)


def _sel_paged(z3, sel, pool_t, layer, pt_flat, n_pages, near_tab):
    B, tq, _ = z3.shape
    R = NSA_HEADS * tq
    n_keys = (n_pages + 1) * KEY_TILE
    assert 2 * (n_pages + 1) <= SEL_LANES and tq <= KEY_TILE
    slot = jnp.arange(LANES)[:, None]
    blk = jnp.arange(n_keys)[None, :] // SEL_BLOCK
    emat = jnp.stack([slot == blk + g * SEL_LANES for g in range(NSA_KV_HEADS)]).astype(BF16)
    kern = functools.partial(_sel_paged_kernel, tq=tq, n_pages=n_pages)
    page_specs = [_page_spec((None, None, 2, NSA_KV_HEADS, HEAD_DIM, PAGE_SIZE), layer, n_pages,
                             (lambda p=p: p), 0, 0, 0, 0) for p in range(n_pages)]
    grid_spec = pltpu.PrefetchScalarGridSpec(
        num_scalar_prefetch=1, grid=(B,),
        in_specs=[pl.BlockSpec((None, tq, MIX_W), lambda b, pt: (b, 0, _cb("nq"))),
                  pl.BlockSpec((None, tq, LANES), lambda b, pt: (b, 0, 0)),
                  pl.BlockSpec((None, tq, 2 * NSA_KV_W), lambda b, pt: (b, 0, _cb("kvs"))),
                  pl.BlockSpec((R, 2 * KEY_TILE), lambda b, pt: (0, 0)),
                  pl.BlockSpec((None, tq, LANES), lambda b, pt: (b, 0, _cb("ng"))),
                  pl.BlockSpec((NSA_KV_HEADS, LANES, n_keys), lambda b, pt: (0, 0, 0))] + page_specs,
        out_specs=pl.BlockSpec((None, tq, MIX_W), lambda b, pt: (b, 0, 0)),
        scratch_shapes=_nsa_scratch(tq))
    return pl.pallas_call(
        kern, grid_spec=grid_spec,
        out_shape=jax.ShapeDtypeStruct((B, tq, MIX_W), F32),
        compiler_params=_cparams("parallel"), name="nsa_slc_paged")(
            pt_flat, z3, sel, z3, near_tab, z3, emat, *([pool_t] * n_pages))


def _win_state_kernel(q_ref, state_ref, new_ref, bias_ref, ng_ref, o_ref, qbd_ref, acc_ref, *, tq):
    _build_qbd(q_ref, qbd_ref, tq)
    qb = qbd_ref[...].astype(BF16)
    kt = state_ref[0].reshape(NSA_KV_W, WINDOW).astype(BF16)
    vt = state_ref[1].reshape(NSA_KV_W, WINDOW).astype(BF16)
    k_new = _new_tile(new_ref, slice(0, NSA_KV_W))
    v_new = _new_tile(new_ref, slice(NSA_KV_W, 2 * NSA_KV_W))
    bias = bias_ref[...]
    s = jnp.concatenate([_dot(qb, kt), _dot_nt(qb, k_new)], axis=1) + bias
    pr = _masked_softmax(s, bias > 0.1 * NEG).astype(BF16)
    acc_ref[...] = _dot_nt(pr[:, 0:WINDOW], vt) + _dot(pr[:, WINDOW:WINDOW + KEY_TILE], v_new)
    _assemble_heads(acc_ref, ng_ref, o_ref, tq, 2)


def _win_state(z3, state_t, layer, win_tab):
    B, tq, _ = z3.shape
    R = NSA_HEADS * tq
    assert WIN_KEYS == WINDOW + KEY_TILE and tq <= KEY_TILE
    kern = functools.partial(_win_state_kernel, tq=tq)
    return pl.pallas_call(
        kern, grid=(B,),
        in_specs=[pl.BlockSpec((None, tq, MIX_W), lambda b: (b, 0, _cb("nq"))),
                  pl.BlockSpec((None, None, 2, NSA_KV_HEADS, HEAD_DIM, WINDOW), lambda b: (layer, b, 0, 0, 0, 0)),
                  pl.BlockSpec((None, tq, 2 * NSA_KV_W), lambda b: (b, 0, _cb("kvw"))),
                  pl.BlockSpec((R, WIN_KEYS), lambda b: (0, 0)),
                  pl.BlockSpec((None, tq, LANES), lambda b: (b, 0, _cb("ng")))],
        out_specs=pl.BlockSpec((None, tq, MIX_W), lambda b: (b, 0, 0)),
        out_shape=jax.ShapeDtypeStruct((B, tq, MIX_W), F32),
        scratch_shapes=_nsa_scratch(tq),
        compiler_params=_cparams("parallel"), name="nsa_win_state")(z3, state_t, z3, win_tab, z3)


def _prep_in_proj(w):
    offs = np.concatenate([[0], np.cumsum(IN_WIDTHS)])
    lead = w.shape[:-1]

    def seg(i):
        return w[..., int(offs[i]):int(offs[i + 1])]

    def pad(x, width):
        return jnp.pad(x, [(0, 0)] * (x.ndim - 1) + [(0, width - x.shape[-1])])

    nq = seg(5).reshape(*lead, 2, 4, HEAD_DIM).swapaxes(-3, -2).reshape(*lead, MIX_W)
    ng = seg(12).reshape(*lead, 3, 2, 4).swapaxes(-2, -1).reshape(*lead, 3 * NSA_HEADS)
    out = jnp.concatenate([seg(16), seg(14), seg(15), seg(2), seg(3), nq, seg(13), seg(0), seg(1),
                           seg(6), seg(7), seg(8), seg(9), seg(10), seg(11), pad(seg(4), LANES), pad(ng, LANES)], -1)
    assert out.shape[-1] == ZW
    return out


def _t5_bucket(dist):
    n = jnp.maximum(dist, 0)
    exact = REL_BUCKETS // 2
    scaled = jnp.log(jnp.maximum(n, 1).astype(F32) / exact) / math.log(REL_MAX_DIST / exact)
    large = jnp.minimum(exact + (scaled * (REL_BUCKETS - exact)).astype(jnp.int32), REL_BUCKETS - 1)
    return jnp.where(n < exact, n, large)


def _bias_rows(rel_slots, dist, valid, tq):
    T, S = dist.shape
    bucket = _t5_bucket(dist).reshape(T // tq, 1, tq, S)
    tab = jnp.zeros((T // tq, NSA_HEADS, tq, S), F32)
    for k in range(REL_BUCKETS):
        tab = jnp.where(bucket == k, rel_slots[k].reshape(1, NSA_HEADS, 1, 1), tab)
    tab = jnp.where(valid.reshape(T // tq, 1, tq, S), tab, NEG)
    return tab.reshape(T // tq, NSA_HEADS * tq, S)


def _group_tables(rel_slots, T, tq, pos0, n_cmp_pad):
    pos_q = pos0 + np.arange(T)
    end = np.arange(n_cmp_pad) * CMP_STRIDE + CMP_BLOCK - 1
    d_cmp = jnp.asarray(pos_q[:, None] - end[None, :], jnp.int32)
    cmp_tab = _bias_rows(rel_slots, d_cmp, d_cmp >= 0, tq)
    t = np.arange(tq)
    d_near = jnp.asarray(KEY_TILE + t[:, None] - np.arange(2 * KEY_TILE)[None, :], jnp.int32)
    far = jnp.repeat(rel_slots[REL_BUCKETS - 1], tq)[:, None]
    near_tab = _bias_rows(rel_slots, d_near, d_near >= 0, tq)[0] - far
    d_win = jnp.asarray(WINDOW + t[:, None] - np.arange(WIN_KEYS)[None, :], jnp.int32)
    win_tab = _bias_rows(rel_slots, d_win, (d_win >= 0) & (d_win <= WINDOW), tq)[0]
    return cmp_tab, near_tab, win_tab


def _cover(n_cmp_pad):
    start = np.arange(n_cmp_pad) * CMP_STRIDE
    blk = np.arange(SEL_LANES) * SEL_BLOCK
    c = ((start[:, None] < blk[None, :] + SEL_BLOCK) & (start[:, None] + CMP_BLOCK > blk[None, :])).astype(np.float32)
    out = np.zeros((2, LANES, n_cmp_pad), np.float32)
    out[0, :SEL_LANES, :] = c.T
    out[1, SEL_LANES:, :] = c.T
    return jnp.asarray(out)


def kernel(x_prompt, x_sample, cache_cmp_kv, cache_slc_kv, cache_sb_kv, state_win_kv, state_gla, page_table,
           rel_bias, w_in, b_in, w_gla_a2, b_gla_a2, gla_norm_g, w_cmp1, b_cmp1, w_cmp2, cmp_pe,
           w_branch, w_o, ln1_g, ln1_b, w_ff_gate, w_ff_up, w_ff_down, ln2_g, ln2_b):
    depth = w_in.shape[0]
    B, T, _ = x_prompt.shape
    DB, DT, _ = x_sample.shape
    n_pages = page_table.shape[1]
    past = n_pages * PAGE_SIZE
    n_win_state = state_win_kv.shape[2]
    assert T % Q_BLK == 0 and DT % SUBLANES == 0 and DT <= Q_BLK and n_win_state == WINDOW and T >= WINDOW

    w_all = _prep_in_proj(w_in).astype(BF16)
    b_all = _prep_in_proj(b_in)[:, None, :]
    wa = jnp.pad(w_gla_a2, ((0, 0), (0, LANES - GLA_GATE_RANK), (0, 0)))
    eye2 = jnp.eye(2, dtype=F32)
    w1bd = jnp.einsum("gG,ljcdh->lcjgdGh", eye2, w_cmp1).reshape(
        depth, 2, 2, CMP_STRIDE * NSA_KV_W, 2 * CMP_HIDDEN).astype(BF16)
    w2bd = jnp.einsum("gG,lchd->lcghGd", eye2, w_cmp2).reshape(depth, 2, 2 * CMP_HIDDEN, NSA_KV_W).astype(BF16)
    pe2 = jnp.broadcast_to(cmp_pe.swapaxes(1, 2)[:, :, :, None, :], (depth, 2, CMP_BLOCK, 2, HEAD_DIM)).reshape(
        depth, 2, CMP_BLOCK, NSA_KV_W)
    b1t = jnp.broadcast_to(b_cmp1[:, :, None, None, :], (depth, 2, 1, 2, CMP_HIDDEN)).reshape(
        depth, 2, 1, 2 * CMP_HIDDEN)
    wb_nsa = w_branch[:, 1].reshape(depth, 2, 4, HEAD_DIM, D_MODEL).swapaxes(1, 2).reshape(depth, MIX_W, D_MODEL)
    wb = jnp.stack([w_branch[:, 0], wb_nsa, w_branch[:, 2]], 1).astype(BF16)
    wo = w_o.astype(BF16)
    wg, wu, wd = w_ff_gate.astype(BF16), w_ff_up.astype(BF16), w_ff_down.astype(BF16)
    rel_slots = rel_bias.reshape(REL_BUCKETS, 2, 4).swapaxes(1, 2).reshape(REL_BUCKETS, NSA_HEADS)

    n_seg_p = T // CMP_STRIDE
    n_seg_s = past // CMP_STRIDE
    tabs_p = _group_tables(rel_slots, T, Q_BLK, 0, n_seg_p)
    tabs_s = _group_tables(rel_slots, DT, DT, past, n_seg_s)
    cover_p, cover_s = _cover(n_seg_p), _cover(n_seg_s)
    pt_flat = page_table.reshape(-1).astype(jnp.int32)
    cmp_pool, slc_pool, sb_pool, win_state_t = (a.transpose(0, 1, 3, 4, 5, 2) for a in
                                                (cache_cmp_kv, cache_slc_kv, cache_sb_kv, state_win_kv))

    xp = x_prompt.reshape(B * T, D_MODEL)
    xs = x_sample.reshape(DB * DT, D_MODEL)
    zero_state = jnp.zeros((B, GLA_HEADS, GLA_DV, GLA_DK), F32)
    outs_p, outs_s = [], []

    def dense_tail(o, z, x, l):
        n = x.shape[0]
        o_gla, o_cmp, o_slc, o_win, o_sb = (a.reshape(n, MIX_W) for a in o)
        x1 = _merge(o_gla, o_cmp, o_slc, o_win, o_sb, z, x, wb[l], wo[l], ln1_g[l][None], ln1_b[l][None])
        return _ffn(x1, wg[l], wu[l], wd[l], ln2_g[l][None], ln2_b[l][None])

    def col(z3, name):
        off, w = COL[name]
        return z3[..., off:off + w]

    for l in range(depth):
        cw = (w1bd[l], pe2[l], b1t[l], w2bd[l])

        z = _linear(xp, w_all[l], b_all[l])
        z3 = z.reshape(B, T, ZW)
        kvc = _compress(z3, _cb("kvc"), T, *cw)
        cmp_tab, near_tab, win_tab = tabs_p
        o_gla, st = _gla(z3, zero_state, wa[l], b_gla_a2[l][None], gla_norm_g[l][None])
        o_cmp, sel = _cmp_attend(z3, kvc, cmp_tab, cover_p, Q_BLK, 0)
        o_slc = _sel_attend(z3, sel, z3, _cb("kvs"), T, near_tab, Q_BLK, 0)
        o_win = _win_attend(z3, win_tab, Q_BLK)
        o_sb = _sb_attend(z3, z3, _cb("sbkv", SB_PROMPT_HEADS * HEAD_DIM), T, Q_BLK, SB_PROMPT_HEADS, 0)
        xp = dense_tail((o_gla, o_cmp, o_slc, o_win, o_sb), z, xp, l)
        outs_p.append((col(z3, "kvc"), col(z3, "kvs"), col(z3, "sbkv"), col(z3, "kvw")[:, T - WINDOW:],
                       st.swapaxes(-1, -2)))

        z = _linear(xs, w_all[l], b_all[l])
        z3 = z.reshape(DB, DT, ZW)
        cmp_tab, near_tab, win_tab = tabs_s
        o_gla, st = _gla(z3, state_gla[l].swapaxes(-1, -2), wa[l], b_gla_a2[l][None], gla_norm_g[l][None])
        kvc = _compress_paged(cmp_pool, l, pt_flat, n_pages, *cw)
        o_cmp, sel = _cmp_attend(z3, kvc, cmp_tab, cover_s, DT, past)
        o_slc = _sel_paged(z3, sel, slc_pool, l, pt_flat, n_pages, near_tab)
        o_win = _win_state(z3, win_state_t, l, win_tab)
        o_sb = _sb_paged(z3, sb_pool, l, pt_flat, n_pages)
        xs = dense_tail((o_gla, o_cmp, o_slc, o_win, o_sb), z, xs, l)
        win_all = jnp.concatenate([state_win_kv[l].reshape(DB, n_win_state, 2 * NSA_KV_W), col(z3, "kvw")], 1)
        outs_s.append((col(z3, "kvc"), col(z3, "kvs"), col(z3, "sbkv"), win_all[:, win_all.shape[1] - WINDOW:],
                       st.swapaxes(-1, -2)))

    def stacked(outs, i, tail):
        a = jnp.stack([o[i] for o in outs])
        return a.reshape(*a.shape[:3], *tail)

    kv_tail = (2, NSA_KV_HEADS, HEAD_DIM)
    sb_tail = (2, SB_HEADS, HEAD_DIM)
    return (xp.reshape(B, T, D_MODEL), xs.reshape(DB, DT, D_MODEL),
            stacked(outs_p, 0, kv_tail), stacked(outs_s, 0, kv_tail),
            stacked(outs_p, 1, kv_tail), stacked(outs_s, 1, kv_tail),
            stacked(outs_p, 2, sb_tail), stacked(outs_s, 2, sb_tail),
            stacked(outs_p, 3, kv_tail), stacked(outs_s, 3, kv_tail),
            jnp.stack([o[4] for o in outs_p]), jnp.stack([o[4] for o in outs_s]))
```

```python
import functools
import math

import jax
import jax.numpy as jnp
import numpy as np
from jax import lax
from jax.experimental import pallas as pl
from jax.experimental.pallas import tpu as pltpu

F32, BF16 = jnp.float32, jnp.bfloat16
HI = lax.Precision.HIGHEST

D_MODEL = 1024
HEAD_DIM = 64
MIX_W = D_MODEL // 2
GLA_HEADS = 4
GLA_DV = MIX_W // GLA_HEADS
GLA_DK = GLA_DV // 2
GLA_GATE_RANK = 16
GLA_TAU = 16.0
GLA_CHUNK = 64
GLA_SUB = 16
NSA_HEADS = MIX_W // HEAD_DIM
NSA_KV_HEADS = 2
NSA_KV_W = NSA_KV_HEADS * HEAD_DIM
CMP_STRIDE = 16
CMP_BLOCK = 32
CMP_HIDDEN = 128
SEL_BLOCK = 64
SEL_TOPN = 16
WINDOW = 512
FORCE_BONUS = 1e4
SB_HEADS = MIX_W // HEAD_DIM
REL_BUCKETS = 32
REL_MAX_DIST = 128
D_FF = -(-8 * D_MODEL // (3 * 256)) * 256
Q_BLK = 128
LN_EPS = 1e-5
TRUNK_DEPTH = 4
DN_ALPHA = (2 * TRUNK_DEPTH) ** 0.25
PAGE_SIZE = 128
GLA_QK_W = GLA_HEADS * GLA_DK
IN_WIDTHS = (GLA_QK_W, GLA_QK_W, MIX_W, MIX_W, GLA_GATE_RANK,
             MIX_W, NSA_KV_W, NSA_KV_W, NSA_KV_W, NSA_KV_W, NSA_KV_W, NSA_KV_W, 3 * NSA_HEADS,
             MIX_W, MIX_W, MIX_W,
             3 * D_MODEL)

LANES = 128
SUBLANES = 8
KEY_TILE = 128
VMEM_LIMIT = 56 * 1024 * 1024
NEG = -1e30
SEL_LANES = 64
SB_UNDERFLOW = -104.0
SB_PROMPT_HEADS = 4
GLA_SAFE_DECAY = 60.0

COL = dict(mg=(0, 3072), sbkv=(3072, 1024), gv=(4096, 512), gr=(4608, 512), nq=(5120, 512),
           sbq=(5632, 512), gqk=(6144, 512), kvc=(6656, 256), kvs=(6912, 256), kvw=(7168, 256),
           ga=(7424, 128), ng=(7552, 128))
ZW = 7680


def _cb(name, width=None):
    off, w = COL[name]
    width = width or w
    assert off % width == 0
    return off // width


def _cparams(*sem):
    return pltpu.CompilerParams(dimension_semantics=sem, vmem_limit_bytes=VMEM_LIMIT)


def _dot(a, b):
    return jnp.dot(a, b, preferred_element_type=F32)


def _dot_hi(a, b):
    return jnp.dot(a, b, precision=HI, preferred_element_type=F32)


def _dot_nt(a, b):
    return lax.dot_general(a, b, (((1,), (1,)), ((), ())), preferred_element_type=F32)


def _dot_tn(a, b):
    return lax.dot_general(a, b, (((0,), (0,)), ((), ())), preferred_element_type=F32)


def _log_sigmoid(x):
    return jnp.minimum(x, 0.0) - jnp.log(1.0 + jnp.exp(-jnp.abs(x)))


def _iota(shape, dim):
    return lax.broadcasted_iota(jnp.int32, shape, dim)


def _pick_tile(n, cands=(1024, 512, 256, 128, 64, 32, 16, 8)):
    for c in cands:
        if n % c == 0:
            return c
    raise ValueError(f"no row tile divides {n}")


IN_PROJ_TN = 1536


def _linear_kernel(x_ref, w_ref, b_ref, o_ref, xb_ref):
    @pl.when(pl.program_id(1) == 0)
    def _():
        xb_ref[...] = x_ref[...].astype(BF16)

    o_ref[...] = _dot(xb_ref[...], w_ref[...]) + b_ref[...]


def _linear(x, w, b):
    n, k = x.shape
    dout = w.shape[1]
    tm, tn = _pick_tile(n), IN_PROJ_TN
    return pl.pallas_call(
        _linear_kernel, grid=(n // tm, dout // tn),
        in_specs=[pl.BlockSpec((tm, k), lambda i, j: (i, 0)),
                  pl.BlockSpec((k, tn), lambda i, j: (0, j)),
                  pl.BlockSpec((1, tn), lambda i, j: (0, j))],
        out_specs=pl.BlockSpec((tm, tn), lambda i, j: (i, j)),
        out_shape=jax.ShapeDtypeStruct((n, dout), F32),
        scratch_shapes=[pltpu.VMEM((tm, k), BF16)],
        compiler_params=_cparams("parallel", "arbitrary"), name="in_proj")(x, w, b)


def _layer_norm(h, g, b):
    mu = jnp.mean(h, -1, keepdims=True)
    d = h - mu
    var = jnp.mean(d * d, -1, keepdims=True)
    return d * lax.rsqrt(var + LN_EPS) * g + b


def _merge_kernel(og_ref, oc_ref, os_ref, ow_ref, ob_ref, mg_ref, x_ref, wb_ref, wo_ref, g_ref, b_ref, o_ref):
    o_nsa = oc_ref[...] + os_ref[...] + ow_ref[...]
    mixed = None
    for br, o in enumerate((og_ref[...], o_nsa, ob_ref[...])):
        proj = _dot(o.astype(BF16), wb_ref[br])
        term = jax.nn.sigmoid(mg_ref[:, br * D_MODEL:(br + 1) * D_MODEL]) * proj
        mixed = term if mixed is None else mixed + term
    y = _dot(mixed.astype(BF16), wo_ref[...])
    o_ref[...] = _layer_norm(DN_ALPHA * x_ref[...] + y, g_ref[...], b_ref[...])


def _merge(o_gla, o_cmp, o_slc, o_win, o_sb, z, x, wb, wo, g, b):
    n = x.shape[0]
    tm = _pick_tile(n, (512, 256, 128, 64, 32, 16, 8))
    mix = pl.BlockSpec((tm, MIX_W), lambda i: (i, 0))
    return pl.pallas_call(
        _merge_kernel, grid=(n // tm,),
        in_specs=[mix, mix, mix, mix, mix,
                  pl.BlockSpec((tm, 3 * D_MODEL), lambda i: (i, _cb("mg"))),
                  pl.BlockSpec((tm, D_MODEL), lambda i: (i, 0)),
                  pl.BlockSpec((3, MIX_W, D_MODEL), lambda i: (0, 0, 0)),
                  pl.BlockSpec((D_MODEL, D_MODEL), lambda i: (0, 0)),
                  pl.BlockSpec((1, D_MODEL), lambda i: (0, 0)),
                  pl.BlockSpec((1, D_MODEL), lambda i: (0, 0))],
        out_specs=pl.BlockSpec((tm, D_MODEL), lambda i: (i, 0)),
        out_shape=jax.ShapeDtypeStruct((n, D_MODEL), F32),
        compiler_params=_cparams("parallel"), name="merge")(o_gla, o_cmp, o_slc, o_win, o_sb, z, x, wb, wo, g, b)


FF_TILE = 1408


def _ffn_kernel(x_ref, wg_ref, wu_ref, wd_ref, g_ref, b_ref, o_ref, xb_ref, acc_ref):
    f = pl.program_id(1)

    @pl.when(f == 0)
    def _():
        xb_ref[...] = x_ref[...].astype(BF16)
        acc_ref[...] = jnp.zeros_like(acc_ref)

    xb = xb_ref[...]
    gate = _dot(xb, wg_ref[...])
    up = _dot(xb, wu_ref[...])
    h = gate * jax.nn.sigmoid(gate) * up
    acc_ref[...] += _dot(h.astype(BF16), wd_ref[...])

    @pl.when(f == pl.num_programs(1) - 1)
    def _():
        o_ref[...] = _layer_norm(DN_ALPHA * x_ref[...] + acc_ref[...], g_ref[...], b_ref[...])


def _ffn(x, wg, wu, wd, g, b):
    n = x.shape[0]
    tm = _pick_tile(n, (512, 256, 128, 64, 32, 16, 8))
    return pl.pallas_call(
        _ffn_kernel, grid=(n // tm, D_FF // FF_TILE),
        in_specs=[pl.BlockSpec((tm, D_MODEL), lambda i, f: (i, 0)),
                  pl.BlockSpec((D_MODEL, FF_TILE), lambda i, f: (0, f)),
                  pl.BlockSpec((D_MODEL, FF_TILE), lambda i, f: (0, f)),
                  pl.BlockSpec((FF_TILE, D_MODEL), lambda i, f: (f, 0)),
                  pl.BlockSpec((1, D_MODEL), lambda i, f: (0, 0)),
                  pl.BlockSpec((1, D_MODEL), lambda i, f: (0, 0))],
        out_specs=pl.BlockSpec((tm, D_MODEL), lambda i, f: (i, 0)),
        out_shape=jax.ShapeDtypeStruct((n, D_MODEL), F32),
        scratch_shapes=[pltpu.VMEM((tm, D_MODEL), BF16), pltpu.VMEM((tm, D_MODEL), F32)],
        compiler_params=_cparams("parallel", "arbitrary"), name="ffn")(x, wg, wu, wd, g, b)


def _gla_kernel(qk_ref, v_ref, r_ref, ga_ref, s0_ref, wa_ref, ba_ref, ng_ref, o_ref, st_ref, *, chunk, n_chunks):
    C = chunk
    rt = C * n_chunks

    @pl.when(pl.program_id(1) == 0)
    def _():
        st_ref[...] = s0_ref[...]

    sub = min(C, GLA_SUB)
    n_sub = C // sub
    causal = _iota((C, C), 1) <= _iota((C, C), 0)
    row_sub = _iota((sub, GLA_DK), 0)
    same_chunk = (lax.shift_right_logical(_iota((rt, rt), 0), int(math.log2(C)))
                  == lax.shift_right_logical(_iota((rt, rt), 1), int(math.log2(C))))
    tril = (same_chunk & (_iota((rt, rt), 1) <= _iota((rt, rt), 0))).astype(F32)
    log_a = _log_sigmoid(_dot_hi(ga_ref[...], wa_ref[...]) + ba_ref[...]) * (1.0 / GLA_TAU)
    cum_all = _dot_hi(tril, log_a)

    def intra_factored(qh, kh, vh, ch):
        qd = qh * jnp.exp(ch)
        kd = kh * jnp.exp(-ch)
        att = jnp.where(causal, _dot_nt(qd.astype(BF16), kd.astype(BF16)), 0.0)
        return _dot(att.astype(BF16), vh.astype(BF16))

    def intra_blocked(qh, kh, vh, ch):
        outs = []
        for sb in range(n_sub):
            a0 = sb * sub
            qs, ks, vs, cs = qh[a0:a0 + sub], kh[a0:a0 + sub], vh[a0:a0 + sub], ch[a0:a0 + sub]
            oi = jnp.zeros((sub, GLA_DV), F32)
            if sb > 0:
                cref = ch[a0 - 1:a0]
                qd = qs * jnp.exp(cs - cref)
                kd = kh[:a0] * jnp.exp(cref - ch[:a0])
                att = _dot_nt(qd.astype(BF16), kd.astype(BF16))
                oi = oi + _dot(att.astype(BF16), vh[:a0].astype(BF16))
            for s in range(sub):
                e = jnp.exp(jnp.minimum(cs - cs[s:s + 1], 0.0))
                w = jnp.sum(jnp.where(row_sub >= s, qs * ks[s:s + 1] * e, 0.0), axis=-1, keepdims=True)
                oi = oi + w * vs[s:s + 1]
            outs.append(oi)
        return outs[0] if n_sub == 1 else jnp.concatenate(outs, axis=0)

    def run_chunk(rows, cum, intra_fn):
        qk = qk_ref[rows, :]
        v = v_ref[rows, :]
        for h in range(GLA_HEADS):
            qh = qk[:, h * GLA_DK:(h + 1) * GLA_DK] * (GLA_DK ** -0.5)
            kh = qk[:, GLA_QK_W + h * GLA_DK:GLA_QK_W + (h + 1) * GLA_DK]
            vh = v[:, h * GLA_DV:(h + 1) * GLA_DV]
            ch = cum[:, h * GLA_DK:(h + 1) * GLA_DK]
            st = st_ref[h]
            o = intra_fn(qh, kh, vh, ch) + _dot_nt((qh * jnp.exp(ch)).astype(BF16), st.astype(BF16))
            mu = jnp.mean(o, -1, keepdims=True)
            d = o - mu
            var = jnp.mean(d * d, -1, keepdims=True)
            lanes = slice(h * GLA_DV, (h + 1) * GLA_DV)
            rr = r_ref[rows, lanes]
            o_ref[rows, lanes] = d * lax.rsqrt(var + LN_EPS) * ng_ref[:, lanes] * (rr * jax.nn.sigmoid(rr))
            last = ch[C - 1:C]
            kdl = kh * jnp.exp(last - ch)
            st_ref[h] = st * jnp.exp(last) + _dot_tn(vh.astype(BF16), kdl.astype(BF16))

    def all_factored():
        for c in range(n_chunks):
            run_chunk(slice(c * C, (c + 1) * C), cum_all[c * C:(c + 1) * C], intra_factored)

    def all_blocked():
        def body(c, carry):
            rows = pl.ds(pl.multiple_of(c * C, C), C)
            chunk_log_a = _log_sigmoid(_dot_hi(ga_ref[rows, :], wa_ref[...]) + ba_ref[...]) * (1.0 / GLA_TAU)
            run_chunk(rows, _dot_hi(causal.astype(F32), chunk_log_a), intra_blocked)
            return carry
        lax.fori_loop(0, n_chunks, body, 0)

    small_decay = jnp.max(-cum_all) <= GLA_SAFE_DECAY
    lax.cond(small_decay, all_factored, all_blocked)


def _gla(z3, s0t, wa, ba, norm_g):
    B, T, _ = z3.shape
    C = math.gcd(T, GLA_CHUNK)
    tt = min(T, 512)
    assert T % tt == 0 and tt % C == 0
    kern = functools.partial(_gla_kernel, chunk=C, n_chunks=tt // C)
    st_spec = pl.BlockSpec((None, GLA_HEADS, GLA_DV, GLA_DK), lambda b, i: (b, 0, 0, 0))
    return pl.pallas_call(
        kern, grid=(B, T // tt),
        in_specs=[pl.BlockSpec((None, tt, 512), lambda b, i: (b, i, _cb("gqk"))),
                  pl.BlockSpec((None, tt, 512), lambda b, i: (b, i, _cb("gv"))),
                  pl.BlockSpec((None, tt, 512), lambda b, i: (b, i, _cb("gr"))),
                  pl.BlockSpec((None, tt, LANES), lambda b, i: (b, i, _cb("ga"))),
                  st_spec,
                  pl.BlockSpec((LANES, GLA_QK_W), lambda b, i: (0, 0)),
                  pl.BlockSpec((1, GLA_QK_W), lambda b, i: (0, 0)),
                  pl.BlockSpec((1, MIX_W), lambda b, i: (0, 0))],
        out_specs=[pl.BlockSpec((None, tt, MIX_W), lambda b, i: (b, i, 0)), st_spec],
        out_shape=[jax.ShapeDtypeStruct((B, T, MIX_W), F32),
                   jax.ShapeDtypeStruct((B, GLA_HEADS, GLA_DV, GLA_DK), F32)],
        compiler_params=_cparams("parallel", "arbitrary"), name="gla")(z3, z3, z3, z3, s0t, wa, ba, norm_g)


def _compress_mlp(rows_ref, w1_ref, pe_ref, b1_ref, w2_ref, h1_ref, n_seg):
    M = n_seg
    xs = [rows_ref[pl.ds(j, M, stride=CMP_STRIDE), :] for j in range(CMP_STRIDE)]
    halves = []
    for r in range(CMP_BLOCK // CMP_STRIDE):
        xr = jnp.concatenate([(xs[j] + pe_ref[r * CMP_STRIDE + j:r * CMP_STRIDE + j + 1, :]).astype(BF16)
                              for j in range(CMP_STRIDE)], axis=1)
        halves.append(_dot(xr, w1_ref[r]))
    h0, h1 = halves
    h1_ref[0:M, :] = h1
    h1_ref[M:M + SUBLANES, :] = jnp.zeros((SUBLANES, 2 * CMP_HIDDEN), F32)
    h = jax.nn.gelu(b1_ref[...] + h0 + h1_ref[pl.ds(1, M), :])
    return _dot(h.astype(BF16), w2_ref[...])


def _compress_kernel(rows_ref, w1_ref, pe_ref, b1_ref, w2_ref, o_ref, h1_ref, *, n_seg):
    o_ref[...] = _compress_mlp(rows_ref, w1_ref, pe_ref, b1_ref, w2_ref, h1_ref, n_seg)


def _compress(rows3, colblk, n_rows, w1bd, pe2, b1t, w2bd):
    B = rows3.shape[0]
    n_seg = n_rows // CMP_STRIDE
    kern = functools.partial(_compress_kernel, n_seg=n_seg)
    return pl.pallas_call(
        kern, grid=(B, 2),
        in_specs=[pl.BlockSpec((None, n_rows, NSA_KV_W), lambda b, c: (b, 0, 2 * colblk + c)),
                  pl.BlockSpec((None, 2, CMP_STRIDE * NSA_KV_W, 2 * CMP_HIDDEN), lambda b, c: (c, 0, 0, 0)),
                  pl.BlockSpec((None, CMP_BLOCK, NSA_KV_W), lambda b, c: (c, 0, 0)),
                  pl.BlockSpec((None, 1, 2 * CMP_HIDDEN), lambda b, c: (c, 0, 0)),
                  pl.BlockSpec((None, 2 * CMP_HIDDEN, NSA_KV_W), lambda b, c: (c, 0, 0))],
        out_specs=pl.BlockSpec((None, n_seg, NSA_KV_W), lambda b, c: (b, 0, c)),
        out_shape=jax.ShapeDtypeStruct((B, n_seg, 2 * NSA_KV_W), F32),
        scratch_shapes=[pltpu.VMEM((n_seg + SUBLANES, 2 * CMP_HIDDEN), F32)],
        compiler_params=_cparams("parallel", "parallel"), name="compress")(rows3, w1bd, pe2, b1t, w2bd)


CMP_SEQS_PER_STEP = 2


def _compress_paged_kernel(pt_ref, *refs, n_pages, n_seq):
    pages = refs[:n_seq * n_pages]
    w1_ref, pe_ref, b1_ref, w2_ref, o_ref, h1_ref, rows_ref = refs[n_seq * n_pages:]
    for p, page in enumerate(pages):
        rows_ref[p * PAGE_SIZE:(p + 1) * PAGE_SIZE, :] = page[...].reshape(NSA_KV_W, PAGE_SIZE).T
    n_seg = n_pages * PAGE_SIZE // CMP_STRIDE
    out = _compress_mlp(rows_ref, w1_ref, pe_ref, b1_ref, w2_ref, h1_ref, n_seq * n_seg)
    for i in range(n_seq):
        o_ref[i] = out[i * n_seg:(i + 1) * n_seg]


def _page_spec(block, layer, n_pages, page_of, *rest):
    def index_map(b, *idx_and_pt):
        *idx, pt = idx_and_pt
        return (layer, pt[b * n_pages + page_of(*idx)]) + tuple(r(*idx) if callable(r) else r for r in rest)
    return pl.BlockSpec(block, index_map)


def _compress_paged(pool_t, layer, pt_flat, n_pages, w1bd, pe2, b1t, w2bd):
    B = pt_flat.shape[0] // n_pages
    n_seq = math.gcd(B, CMP_SEQS_PER_STEP)
    n_seg = n_pages * PAGE_SIZE // CMP_STRIDE
    kern = functools.partial(_compress_paged_kernel, n_pages=n_pages, n_seq=n_seq)
    page_block = (None, None, None, NSA_KV_HEADS, HEAD_DIM, PAGE_SIZE)

    def page_spec(i, p):
        return pl.BlockSpec(page_block, lambda b, c, pt: (layer, pt[(b * n_seq + i) * n_pages + p], c, 0, 0, 0))

    grid_spec = pltpu.PrefetchScalarGridSpec(
        num_scalar_prefetch=1, grid=(B // n_seq, 2),
        in_specs=[page_spec(i, p) for i in range(n_seq) for p in range(n_pages)] + [
            pl.BlockSpec((None, 2, CMP_STRIDE * NSA_KV_W, 2 * CMP_HIDDEN), lambda b, c, pt: (c, 0, 0, 0)),
            pl.BlockSpec((None, CMP_BLOCK, NSA_KV_W), lambda b, c, pt: (c, 0, 0)),
            pl.BlockSpec((None, 1, 2 * CMP_HIDDEN), lambda b, c, pt: (c, 0, 0)),
            pl.BlockSpec((None, 2 * CMP_HIDDEN, NSA_KV_W), lambda b, c, pt: (c, 0, 0))],
        out_specs=pl.BlockSpec((n_seq, n_seg, NSA_KV_W), lambda b, c, pt: (b, 0, c)),
        scratch_shapes=[pltpu.VMEM((n_seq * n_seg + SUBLANES, 2 * CMP_HIDDEN), F32),
                        pltpu.VMEM((n_seq * n_pages * PAGE_SIZE, NSA_KV_W), F32)])
    return pl.pallas_call(
        kern, grid_spec=grid_spec,
        out_shape=jax.ShapeDtypeStruct((B, n_seg, 2 * NSA_KV_W), F32),
        compiler_params=_cparams("parallel", "parallel"), name="compress_paged")(
            pt_flat, *([pool_t] * (n_seq * n_pages)), w1bd, pe2, b1t, w2bd)


def _build_qbd(q_ref, qbd_ref, tq):
    lane = _iota((tq, LANES), 1)
    for p in range(NSA_HEADS):
        slab = q_ref[:, (p // 2) * LANES:(p // 2 + 1) * LANES] * (HEAD_DIM ** -0.5)
        keep = (lane >= HEAD_DIM) if p % 2 else (lane < HEAD_DIM)
        qbd_ref[p * tq:(p + 1) * tq, :] = jnp.where(keep, slab, 0.0)


def _assemble_heads(acc_ref, ng_ref, o_ref, tq, branch):
    lo = _iota((tq, LANES), 1) < HEAD_DIM
    for k in range(NSA_HEADS // 2):
        a0 = acc_ref[(2 * k) * tq:(2 * k + 1) * tq, :]
        a1 = acc_ref[(2 * k + 1) * tq:(2 * k + 2) * tq, :]
        c0 = branch * NSA_HEADS + 2 * k
        g0 = jax.nn.sigmoid(ng_ref[:, c0:c0 + 1])
        g1 = jax.nn.sigmoid(ng_ref[:, c0 + 1:c0 + 2])
        o_ref[:, k * LANES:(k + 1) * LANES] = jnp.where(lo, a0 * g0, a1 * g1)


def _masked_softmax(s, valid):
    s = jnp.where(valid, s, NEG)
    m = jnp.max(s, -1, keepdims=True)
    e = jnp.where(valid, jnp.exp(s - m), 0.0)
    return e / jnp.maximum(jnp.sum(e, -1, keepdims=True), 1e-30)


def _dot_nt_hi(a, b):
    return lax.dot_general(a, b, (((1,), (1,)), ((), ())), precision=HI, preferred_element_type=F32)


def _select_blocks(score, blk, cur, slot_axis):
    forced = (blk == 0) | (blk == cur) | (blk == cur - 1)
    score = jnp.where(forced, score + FORCE_BONUS, score)
    visible = blk <= cur
    score = jnp.where(visible, score, -jnp.inf)
    first_group = _iota(score.shape, slot_axis) < SEL_LANES
    rank = jnp.zeros(score.shape, F32)
    for mp in range(SEL_LANES):
        if slot_axis == 0:
            other = jnp.where(first_group, score[mp:mp + 1, :], score[SEL_LANES + mp:SEL_LANES + mp + 1, :])
        else:
            other = jnp.where(first_group, score[:, mp:mp + 1], score[:, SEL_LANES + mp:SEL_LANES + mp + 1])
        tie = jnp.where(blk > mp, 1.0, 0.0)
        rank = rank + jnp.where(other > score, 1.0, jnp.where(other == score, tie, 0.0))
    return jnp.where(visible, jnp.where(rank < SEL_TOPN, 1.0, 0.0), 0.0)


def _cmp_kernel(q_ref, kvc_ref, bias_ref, covt_ref, ng_ref, o_ref, sel_ref, qbd_ref, acc_ref, *, tq, rc, pos0):
    qi = pl.program_id(1)
    _build_qbd(q_ref, qbd_ref, tq)
    n = kvc_ref.shape[0]
    kc = kvc_ref[:, 0:NSA_KV_W].astype(BF16)
    vc = kvc_ref[:, NSA_KV_W:2 * NSA_KV_W].astype(BF16)
    ones = jnp.ones((n, LANES), BF16)
    heads_per_chunk = rc // tq
    p_sum = [jnp.zeros((tq, n), F32), jnp.zeros((tq, n), F32)]
    for c in range(NSA_HEADS * tq // rc):
        rows = slice(c * rc, (c + 1) * rc)
        s = _dot_nt(qbd_ref[rows, :].astype(BF16), kc) + bias_ref[rows, :]
        lane_max = s[:, 0:LANES]
        for t in range(1, n // LANES):
            lane_max = jnp.maximum(lane_max, s[:, t * LANES:(t + 1) * LANES])
        row_max = jnp.max(lane_max, -1, keepdims=True)
        e = jnp.exp(s - row_max)
        e_hi = e.astype(BF16)
        e_lo = (e - e_hi.astype(F32)).astype(BF16)
        row_sum = _dot(e_hi, ones) + _dot(e_lo, ones)
        scale = jnp.where(row_max > 0.5 * NEG, 1.0 / row_sum, 0.0)
        pr = jnp.concatenate([e[:, t * LANES:(t + 1) * LANES] * scale for t in range(n // LANES)], axis=1)
        acc_ref[rows, :] = _dot(pr.astype(BF16), vc)
        for pp in range(heads_per_chunk):
            g = (c * heads_per_chunk + pp) % 2
            p_sum[g] = p_sum[g] + pr[pp * tq:(pp + 1) * tq]
    _assemble_heads(acc_ref, ng_ref, o_ref, tq, 0)

    shift = int(math.log2(SEL_BLOCK))
    if tq == LANES:
        score_t = _dot_nt_hi(covt_ref[0], p_sum[0]) + _dot_nt_hi(covt_ref[1], p_sum[1])
        blk = jnp.bitwise_and(_iota((LANES, tq), 0), SEL_LANES - 1)
        cur = lax.shift_right_logical(pos0 + qi * tq + _iota((LANES, tq), 1), shift)
        sel_t = _select_blocks(score_t, blk, cur, 0)
        eye = (_iota((LANES, LANES), 0) == _iota((LANES, LANES), 1)).astype(BF16)
        sel_ref[...] = _dot_tn(sel_t.astype(BF16), eye)
    else:
        score = _dot_nt_hi(p_sum[0], covt_ref[0]) + _dot_nt_hi(p_sum[1], covt_ref[1])
        blk = jnp.bitwise_and(_iota((tq, LANES), 1), SEL_LANES - 1)
        cur = lax.shift_right_logical(pos0 + qi * tq + _iota((tq, LANES), 0), shift)
        sel_ref[...] = _select_blocks(score, blk, cur, 1)


def _nsa_common_specs(tq):
    q_spec = pl.BlockSpec((None, tq, MIX_W), lambda b, i: (b, i, _cb("nq")))
    ng_spec = pl.BlockSpec((None, tq, LANES), lambda b, i: (b, i, _cb("ng")))
    o_spec = pl.BlockSpec((None, tq, MIX_W), lambda b, i: (b, i, 0))
    return q_spec, ng_spec, o_spec


def _nsa_scratch(tq):
    return [pltpu.VMEM((NSA_HEADS * tq, LANES), F32), pltpu.VMEM((NSA_HEADS * tq, LANES), F32)]


def _cmp_attend(z3, kvc, bias_tab, cover2, tq, pos0):
    B, T, _ = z3.shape
    n = kvc.shape[1]
    R = NSA_HEADS * tq
    rc = min(R, 128)
    assert n % LANES == 0
    q_spec, ng_spec, o_spec = _nsa_common_specs(tq)
    kern = functools.partial(_cmp_kernel, tq=tq, rc=rc, pos0=pos0)
    return pl.pallas_call(
        kern, grid=(B, T // tq),
        in_specs=[q_spec,
                  pl.BlockSpec((None, n, 2 * NSA_KV_W), lambda b, i: (b, 0, 0)),
                  pl.BlockSpec((None, R, n), lambda b, i: (i, 0, 0)),
                  pl.BlockSpec((2, LANES, n), lambda b, i: (0, 0, 0)),
                  ng_spec],
        out_specs=[o_spec, pl.BlockSpec((None, tq, LANES), lambda b, i: (b, i, 0))],
        out_shape=[jax.ShapeDtypeStruct((B, T, MIX_W), F32), jax.ShapeDtypeStruct((B, T, LANES), F32)],
        scratch_shapes=_nsa_scratch(tq),
        compiler_params=_cparams("parallel", "parallel"), name="nsa_cmp")(z3, kvc, bias_tab, cover2, z3)


SLC_PAIR = 2 * KEY_TILE


def _sel_kernel(q_ref, sel_ref, kv_ref, near_ref, ng_ref, emat_ref, o_ref,
                qbd_ref, acc_ref, mx_ref, selx_ref, selp_ref, lg_ref, *, tq, tile0, n_tiles):
    qi = pl.program_id(1)
    qt = tile0 + (qi * tq) // KEY_TILE
    R = NSA_HEADS * tq
    _build_qbd(q_ref, qbd_ref, tq)
    sel_b = sel_ref[...].astype(BF16)
    for t in range(n_tiles):
        @pl.when(t <= qt)
        def _(t=t):
            for g in range(NSA_KV_HEADS):
                hit = _dot(sel_b, emat_ref[g, :, t * KEY_TILE:(t + 1) * KEY_TILE])
                selx_ref[g, t] = (hit - 1.0) * (-NEG)
                if t // 2 < n_tiles // 2:
                    selp_ref[g, t // 2, :, (t % 2) * KEY_TILE:(t % 2 + 1) * KEY_TILE] = (hit - 1.0) * (-NEG)
    n_far = jnp.maximum(qt - 1, 0)
    n_pair = lax.shift_right_logical(n_far, 1)
    odd_far = jnp.bitwise_and(n_far, 1) == 1
    drop_prev = ((qt >= 1).astype(F32) - 1.0) * (-NEG)
    odd_slot, near_slot = n_tiles // 2, n_tiles // 2 + 1
    left, right = slice(0, KEY_TILE), slice(KEY_TILE, SLC_PAIR)
    near_tiles = ((n_far, near_slot, left, lambda rows: near_ref[rows, left] + drop_prev),
                  (qt, near_slot, right, lambda rows: near_ref[rows, right]))

    def head_rows(p):
        return slice(p * tq, (p + 1) * tq)

    def key_rows(start, size):
        return pl.ds(pl.multiple_of(start, KEY_TILE), size)

    def sweep_max(kk, sel_add, slot, lanes, bias_of):
        for p in range(NSA_HEADS):
            s = _dot_nt(qbd_ref[head_rows(p), :].astype(BF16), kk) + sel_add[p % 2]
            if bias_of is not None:
                s = s + bias_of(head_rows(p))
            lg_ref[p, slot, :, lanes] = s
            m = mx_ref[head_rows(p), :]
            for t in range(s.shape[1] // KEY_TILE):
                m = jnp.maximum(m, s[:, t * KEY_TILE:(t + 1) * KEY_TILE])
            mx_ref[head_rows(p), :] = m

    def sweep_acc(v, slot, lanes, row_max):
        lo = _iota(v.shape, 1) < HEAD_DIM
        vv = (jnp.where(lo, v, 1.0).astype(BF16), jnp.where(lo, 1.0, v).astype(BF16))
        for p in range(NSA_HEADS):
            pe = jnp.exp(lg_ref[p, slot, :, lanes] - row_max[p]).astype(BF16)
            acc_ref[head_rows(p), :] += _dot(pe, vv[p % 2])

    def sweep(kt, slot, lanes, bias_of, row_max):
        pair = lanes is None
        size = SLC_PAIR if pair else KEY_TILE
        rows = key_rows(kt * size, size)
        lanes = slice(0, SLC_PAIR) if pair else lanes
        if row_max is None:
            sel_add = [(selp_ref if pair else selx_ref)[g, kt] for g in range(NSA_KV_HEADS)]
            sweep_max(kv_ref[rows, 0:NSA_KV_W].astype(BF16), sel_add, slot, lanes, bias_of)
        else:
            sweep_acc(kv_ref[rows, NSA_KV_W:2 * NSA_KV_W], slot, lanes, row_max)

    def all_tiles(row_max):
        lax.fori_loop(0, n_pair, lambda kt, cr: (sweep(kt, kt, None, None, row_max), cr)[1], 0)

        @pl.when(odd_far)
        def _():
            sweep(n_far - 1, odd_slot, left, None, row_max)

        for kt, slot, lanes, bias_of in near_tiles:
            sweep(kt, slot, lanes, bias_of, row_max)

    mx_ref[...] = jnp.full((R, LANES), NEG, F32)
    all_tiles(None)
    row_max = [jnp.max(mx_ref[head_rows(p), :], -1, keepdims=True) for p in range(NSA_HEADS)]
    acc_ref[...] = jnp.zeros((R, LANES), F32)
    all_tiles(row_max)
    acc = acc_ref[...]
    acc_ref[...] = acc / pltpu.roll(acc, HEAD_DIM, 1)
    _assemble_heads(acc_ref, ng_ref, o_ref, tq, 1)


def _sel_attend(z3, sel, keys3, key_colblk, n_keys, near_tab, tq, pos0):
    B, T, _ = z3.shape
    R = NSA_HEADS * tq
    n_tiles = n_keys // KEY_TILE
    assert pos0 % KEY_TILE == 0 and tq == KEY_TILE and 2 * n_tiles <= SEL_LANES
    slot = jnp.arange(LANES)[:, None]
    blk = jnp.arange(n_keys)[None, :] // SEL_BLOCK
    emat = jnp.stack([slot == blk + g * SEL_LANES for g in range(NSA_KV_HEADS)]).astype(BF16)
    q_spec, ng_spec, o_spec = _nsa_common_specs(tq)
    kern = functools.partial(_sel_kernel, tq=tq, tile0=pos0 // KEY_TILE, n_tiles=n_tiles)
    return pl.pallas_call(
        kern, grid=(B, T // tq),
        in_specs=[q_spec,
                  pl.BlockSpec((None, tq, LANES), lambda b, i: (b, i, 0)),
                  pl.BlockSpec((None, n_keys, 2 * NSA_KV_W), lambda b, i: (b, 0, key_colblk)),
                  pl.BlockSpec((R, 2 * KEY_TILE), lambda b, i: (0, 0)),
                  ng_spec,
                  pl.BlockSpec((NSA_KV_HEADS, LANES, n_keys), lambda b, i: (0, 0, 0))],
        out_specs=o_spec,
        out_shape=jax.ShapeDtypeStruct((B, T, MIX_W), F32),
        scratch_shapes=_nsa_scratch(tq) + [pltpu.VMEM((R, LANES), F32),
                                           pltpu.VMEM((NSA_KV_HEADS, n_tiles, tq, KEY_TILE), F32),
                                           pltpu.VMEM((NSA_KV_HEADS, max(n_tiles // 2, 1), tq, SLC_PAIR), F32),
                                           pltpu.VMEM((NSA_HEADS, n_tiles // 2 + 2, tq, SLC_PAIR), F32)],
        compiler_params=_cparams("parallel", "parallel"), name="nsa_slc")(
            z3, sel, keys3, near_tab, z3, emat)


WIN_KEYS = WINDOW + Q_BLK


def _win_kernel(q_ref, kv_ref, bias_ref, ng_ref, o_ref, qbd_ref, acc_ref, *, tq):
    qi = pl.program_id(1)
    _build_qbd(q_ref, qbd_ref, tq)
    n_t = WIN_KEYS // KEY_TILE
    ks, vs, drop = [], [], []
    for j in range(n_t):
        kt = qi - (n_t - 1) + j
        rows = pl.ds(pl.multiple_of(jnp.maximum(kt, 0) * KEY_TILE, KEY_TILE), KEY_TILE)
        ks.append(kv_ref[rows, 0:NSA_KV_W].astype(BF16))
        vs.append(kv_ref[rows, NSA_KV_W:2 * NSA_KV_W].astype(BF16))
        drop.append(((kt >= 0).astype(F32) - 1.0) * (-NEG))
    chunks = [tuple(range(j, min(j + 2, n_t))) for j in range(0, n_t, 2)]
    kc = [ks[c[0]] if len(c) == 1 else jnp.concatenate([ks[j] for j in c], axis=0) for c in chunks]
    vc = [vs[c[0]] if len(c) == 1 else jnp.concatenate([vs[j] for j in c], axis=0) for c in chunks]
    for p in range(NSA_HEADS):
        rows = slice(p * tq, (p + 1) * tq)
        qb = qbd_ref[rows, :].astype(BF16)
        s = []
        for c, kk in zip(chunks, kc):
            bias = [bias_ref[rows, j * KEY_TILE:(j + 1) * KEY_TILE] + drop[j] for j in c]
            s.append(_dot_nt(qb, kk) + (bias[0] if len(c) == 1 else jnp.concatenate(bias, axis=1)))
        lane_max = None
        for sc in s:
            for t in range(sc.shape[1] // KEY_TILE):
                piece = sc[:, t * KEY_TILE:(t + 1) * KEY_TILE]
                lane_max = piece if lane_max is None else jnp.maximum(lane_max, piece)
        row_max = jnp.max(lane_max, -1, keepdims=True)
        lane_sum, acc = None, None
        for sc, vv in zip(s, vc):
            e = jnp.exp(sc - row_max)
            for t in range(e.shape[1] // KEY_TILE):
                piece = e[:, t * KEY_TILE:(t + 1) * KEY_TILE]
                lane_sum = piece if lane_sum is None else lane_sum + piece
            part = _dot(e.astype(BF16), vv)
            acc = part if acc is None else acc + part
        acc_ref[rows, :] = acc / jnp.sum(lane_sum, -1, keepdims=True)
    _assemble_heads(acc_ref, ng_ref, o_ref, tq, 2)


def _win_attend(z3, win_tab, tq):
    B, T, _ = z3.shape
    R = NSA_HEADS * tq
    assert tq == KEY_TILE
    q_spec, ng_spec, o_spec = _nsa_common_specs(tq)
    kern = functools.partial(_win_kernel, tq=tq)
    return pl.pallas_call(
        kern, grid=(B, T // tq),
        in_specs=[q_spec,
                  pl.BlockSpec((None, T, 2 * NSA_KV_W), lambda b, i: (b, 0, _cb("kvw"))),
                  pl.BlockSpec((R, WIN_KEYS), lambda b, i: (0, 0)),
                  ng_spec],
        out_specs=o_spec,
        out_shape=jax.ShapeDtypeStruct((B, T, MIX_W), F32),
        scratch_shapes=_nsa_scratch(tq),
        compiler_params=_cparams("parallel", "parallel"), name="nsa_win")(z3, z3, win_tab, z3)


def _sb_kernel(q_ref, k_ref, v_ref, o_ref, *, tq, heads, tile0):
    qi = pl.program_id(2)
    qt = tile0 + (qi * tq) // KEY_TILE
    sw = heads * HEAD_DIM
    R = heads * tq
    lane_head = lax.shift_right_logical(_iota((tq, sw), 1), int(math.log2(HEAD_DIM)))
    q = q_ref[...] * (HEAD_DIM ** -0.5)
    qbd = jnp.concatenate([jnp.where(lane_head == j, q, 0.0) for j in range(heads)], axis=0).astype(BF16)
    causal = _iota((R, KEY_TILE), 1) < jnp.bitwise_and(_iota((R, KEY_TILE), 0), tq - 1)

    def tile(start, size, run, acc, masked):
        rows = pl.ds(pl.multiple_of(start, KEY_TILE), size)
        after = (_iota((size, size), 0) > _iota((size, size), 1)).astype(BF16)
        z = _dot_nt(qbd, k_ref[rows, :].astype(BF16))
        u = _log_sigmoid(-z)
        if masked:
            u = jnp.where(causal, u, 0.0)
        u_hi = u.astype(BF16)
        u_lo = (u - u_hi.astype(F32)).astype(BF16)
        rest = _dot(u_hi, after) + _dot(u_lo, after)
        a = jnp.exp(u + z + rest + run)
        if masked:
            a = jnp.where(causal, a, 0.0)
        acc = acc + _dot(a.astype(BF16), v_ref[rows, :].astype(BF16))
        return run + jnp.sum(u, -1, keepdims=True), acc

    run, acc = tile(qt * KEY_TILE, KEY_TILE, jnp.zeros((R, 1), F32), jnp.zeros((R, sw), F32), True)

    def live(run):
        return jnp.max(run) > SB_UNDERFLOW

    odd = jnp.bitwise_and(qt, 1)

    def more(st):
        return (st[0] >= 0) & live(st[1])

    def step(st):
        run, acc = tile((odd + 2 * st[0]) * KEY_TILE, 2 * KEY_TILE, st[1], st[2], False)
        return st[0] - 1, run, acc

    _, run, acc = lax.while_loop(more, step, (lax.shift_right_logical(qt, 1) - 1, run, acc))
    run, acc = lax.cond((odd == 1) & live(run),
                        lambda: tile(0, KEY_TILE, run, acc, False),
                        lambda: (run, acc))
    out = None
    for j in range(heads):
        part = jnp.where(lane_head == j, acc[j * tq:(j + 1) * tq], 0.0)
        out = part if out is None else out + part
    o_ref[...] = out


def _sb_attend(z3, keys3, k_colblk0, n_keys, tq, heads, pos0):
    B, T, _ = z3.shape
    sw = heads * HEAD_DIM
    n_slabs = SB_HEADS // heads
    assert pos0 % KEY_TILE == 0 and (tq == KEY_TILE or T == tq) and tq & (tq - 1) == 0
    kern = functools.partial(_sb_kernel, tq=tq, heads=heads, tile0=pos0 // KEY_TILE)
    qcb = _cb("sbq", sw)
    return pl.pallas_call(
        kern, grid=(B, n_slabs, T // tq),
        in_specs=[pl.BlockSpec((None, tq, sw), lambda b, s, i: (b, i, qcb + s)),
                  pl.BlockSpec((None, n_keys, sw), lambda b, s, i: (b, 0, k_colblk0 + s)),
                  pl.BlockSpec((None, n_keys, sw), lambda b, s, i: (b, 0, k_colblk0 + n_slabs + s))],
        out_specs=pl.BlockSpec((None, tq, sw), lambda b, s, i: (b, i, s)),
        out_shape=jax.ShapeDtypeStruct((B, T, MIX_W), F32),
        compiler_params=_cparams("parallel", "parallel", "parallel"), name="sb")(z3, keys3, keys3)


def _new_tile(new_ref, lanes):
    x = new_ref[:, lanes]
    return jnp.concatenate([x, jnp.zeros((KEY_TILE - x.shape[0], x.shape[1]), F32)], axis=0).astype(BF16)


def _sb_tile(z, weigh_values, mask, run_ref, acc_ref):
    after = (_iota((KEY_TILE, KEY_TILE), 0) > _iota((KEY_TILE, KEY_TILE), 1)).astype(BF16)
    u = _log_sigmoid(-z)
    if mask is not None:
        u = jnp.where(mask, u, 0.0)
    u_hi = u.astype(BF16)
    u_lo = (u - u_hi.astype(F32)).astype(BF16)
    a = jnp.exp(u + z + _dot(u_hi, after) + _dot(u_lo, after) + run_ref[...])
    if mask is not None:
        a = jnp.where(mask, a, 0.0)
    acc_ref[...] += weigh_values(a.astype(BF16))
    run_ref[...] += jnp.sum(u, -1, keepdims=True)


def _sb_build_qbd(q_ref, qbd_ref, tq):
    sw = SB_HEADS * HEAD_DIM
    lane_head = lax.shift_right_logical(_iota((tq, sw), 1), int(math.log2(HEAD_DIM)))
    q = q_ref[...] * (HEAD_DIM ** -0.5)
    for h in range(SB_HEADS):
        qbd_ref[h * tq:(h + 1) * tq, :] = jnp.where(lane_head == h, q, 0.0)


def _sb_pages(pages, qbd_ref, run_ref, acc_ref):
    sw = SB_HEADS * HEAD_DIM
    for page in reversed(pages):
        @pl.when(jnp.max(run_ref[...]) > SB_UNDERFLOW)
        def _(page=page):
            kt = page[0].reshape(sw, PAGE_SIZE).astype(BF16)
            vt = page[1].reshape(sw, PAGE_SIZE).astype(BF16)
            _sb_tile(_dot(qbd_ref[...].astype(BF16), kt), lambda a: _dot_nt(a, vt), None, run_ref, acc_ref)


def _sb_lead_kernel(pt_ref, q_ref, new_ref, *refs, tq, n_lead):
    pages = refs[:n_lead]
    run_ref, acc_ref, qbd_ref = refs[n_lead:]
    sw = SB_HEADS * HEAD_DIM
    R = SB_HEADS * tq
    _sb_build_qbd(q_ref, qbd_ref, tq)
    run_ref[...] = jnp.zeros((R, 1), F32)
    acc_ref[...] = jnp.zeros((R, sw), F32)
    k_new = _new_tile(new_ref, slice(0, sw))
    v_new = _new_tile(new_ref, slice(sw, 2 * sw))
    causal = _iota((R, KEY_TILE), 1) < jnp.bitwise_and(_iota((R, KEY_TILE), 0), tq - 1)
    _sb_tile(_dot_nt(qbd_ref[...].astype(BF16), k_new), lambda a: _dot(a, v_new), causal, run_ref, acc_ref)
    _sb_pages(pages, qbd_ref, run_ref, acc_ref)


def _sb_rest_kernel(pt_ref, need_ref, q_ref, run_in_ref, acc_in_ref, *refs, tq, n_rest):
    pages = refs[:n_rest]
    o_ref, qbd_ref, run_ref, acc_ref = refs[n_rest:]
    sw = SB_HEADS * HEAD_DIM
    run_ref[...] = run_in_ref[...]
    acc_ref[...] = acc_in_ref[...]

    @pl.when(need_ref[pl.program_id(0)] > 0)
    def _():
        _sb_build_qbd(q_ref, qbd_ref, tq)
        _sb_pages(pages, qbd_ref, run_ref, acc_ref)

    lane_head = lax.shift_right_logical(_iota((tq, sw), 1), int(math.log2(HEAD_DIM)))
    out = None
    for h in range(SB_HEADS):
        part = jnp.where(lane_head == h, acc_ref[h * tq:(h + 1) * tq, :], 0.0)
        out = part if out is None else out + part
    o_ref[...] = out


SB_LEAD_PAGES = 2


def _sb_paged(z3, pool_t, layer, pt_flat, n_pages):
    B, tq, _ = z3.shape
    sw = SB_HEADS * HEAD_DIM
    R = SB_HEADS * tq
    n_lead = min(SB_LEAD_PAGES, n_pages - 1)
    n_rest = n_pages - n_lead
    assert tq & (tq - 1) == 0 and tq <= KEY_TILE
    page_block = (None, None, 2, SB_HEADS, HEAD_DIM, PAGE_SIZE)
    q_spec = pl.BlockSpec((None, tq, sw), lambda b, *_: (b, 0, _cb("sbq")))
    run_spec = pl.BlockSpec((None, R, 1), lambda b, *_: (b, 0, 0))
    acc_spec = pl.BlockSpec((None, R, sw), lambda b, *_: (b, 0, 0))

    lead_pages = [_page_spec(page_block, layer, n_pages, (lambda i=i: n_rest + i), 0, 0, 0, 0) for i in range(n_lead)]
    run, acc = pl.pallas_call(
        functools.partial(_sb_lead_kernel, tq=tq, n_lead=n_lead),
        grid_spec=pltpu.PrefetchScalarGridSpec(
            num_scalar_prefetch=1, grid=(B,),
            in_specs=[q_spec, pl.BlockSpec((None, tq, 2 * sw), lambda b, pt: (b, 0, _cb("sbkv")))] + lead_pages,
            out_specs=[run_spec, acc_spec],
            scratch_shapes=[pltpu.VMEM((R, sw), F32)]),
        out_shape=[jax.ShapeDtypeStruct((B, R, 1), F32), jax.ShapeDtypeStruct((B, R, sw), F32)],
        compiler_params=_cparams("parallel"), name="sb_lead")(pt_flat, z3, z3, *([pool_t] * n_lead))

    need = (jnp.max(run, axis=(1, 2)) > SB_UNDERFLOW).astype(jnp.int32)

    def rest_page(i):
        def index_map(b, pt, nd):
            return (layer, pt[jnp.where(nd[b] > 0, b * n_pages + i, i)], 0, 0, 0, 0)
        return pl.BlockSpec(page_block, index_map)

    return pl.pallas_call(
        functools.partial(_sb_rest_kernel, tq=tq, n_rest=n_rest),
        grid_spec=pltpu.PrefetchScalarGridSpec(
            num_scalar_prefetch=2, grid=(B,),
            in_specs=[q_spec, run_spec, acc_spec] + [rest_page(i) for i in range(n_rest)],
            out_specs=pl.BlockSpec((None, tq, sw), lambda b, pt, nd: (b, 0, 0)),
            scratch_shapes=[pltpu.VMEM((R, sw), F32), pltpu.VMEM((R, 1), F32), pltpu.VMEM((R, sw), F32)]),
        out_shape=jax.ShapeDtypeStruct((B, tq, MIX_W), F32),
        compiler_params=_cparams("arbitrary"), name="sb_rest")(
            pt_flat, need, z3, run, acc, *([pool_t] * n_rest))


def _sel_paged_kernel(pt_ref, q_ref, sel_ref, new_ref, near_ref, ng_ref, emat_ref, *refs, tq, n_pages):
    pages = refs[:n_pages]
    o_ref, qbd_ref, acc_ref = refs[n_pages:]
    _build_qbd(q_ref, qbd_ref, tq)
    qb = qbd_ref[...].astype(BF16)
    sel_b = sel_ref[...].astype(BF16)
    k_new = _new_tile(new_ref, slice(0, NSA_KV_W))
    v_new = _new_tile(new_ref, slice(NSA_KV_W, 2 * NSA_KV_W))
    n_tiles = n_pages + 1
    logits = []
    for t in range(n_tiles):
        hit = [_dot(sel_b, emat_ref[g, :, t * KEY_TILE:(t + 1) * KEY_TILE]) for g in range(NSA_KV_HEADS)]
        open_ = jnp.concatenate([hit[p % 2] for p in range(NSA_HEADS)], axis=0)
        if t < n_pages:
            s = _dot(qb, pages[t][0].reshape(NSA_KV_W, PAGE_SIZE).astype(BF16))
        else:
            s = _dot_nt(qb, k_new)
        s = s + (open_ - 1.0) * (-NEG)
        if t >= n_pages - 1:
            s = s + near_ref[:, (t - n_pages + 1) * KEY_TILE:(t - n_pages + 2) * KEY_TILE]
        logits.append(s)
    lane_max = logits[0]
    for s in logits[1:]:
        lane_max = jnp.maximum(lane_max, s)
    row_max = jnp.max(lane_max, -1, keepdims=True)
    lane_sum = None
    acc = None
    for t, s in enumerate(logits):
        pe = jnp.exp(s - row_max)
        lane_sum = pe if lane_sum is None else lane_sum + pe
        pb = pe.astype(BF16)
        part = _dot_nt(pb, pages[t][1].reshape(NSA_KV_W, PAGE_SIZE).astype(BF16)) if t < n_pages else _dot(pb, v_new)
        acc = part if acc is None else acc + part
    acc_ref[...] = acc / jnp.sum(lane_sum, -1, keepdims=True)
    _assemble_heads(acc_ref, ng_ref, o_ref, tq, 1)


def _sel_paged(z3, sel, pool_t, layer, pt_flat, n_pages, near_tab):
    B, tq, _ = z3.shape
    R = NSA_HEADS * tq
    n_keys = (n_pages + 1) * KEY_TILE
    assert 2 * (n_pages + 1) <= SEL_LANES and tq <= KEY_TILE
    slot = jnp.arange(LANES)[:, None]
    blk = jnp.arange(n_keys)[None, :] // SEL_BLOCK
    emat = jnp.stack([slot == blk + g * SEL_LANES for g in range(NSA_KV_HEADS)]).astype(BF16)
    kern = functools.partial(_sel_paged_kernel, tq=tq, n_pages=n_pages)
    page_specs = [_page_spec((None, None, 2, NSA_KV_HEADS, HEAD_DIM, PAGE_SIZE), layer, n_pages,
                             (lambda p=p: p), 0, 0, 0, 0) for p in range(n_pages)]
    grid_spec = pltpu.PrefetchScalarGridSpec(
        num_scalar_prefetch=1, grid=(B,),
        in_specs=[pl.BlockSpec((None, tq, MIX_W), lambda b, pt: (b, 0, _cb("nq"))),
                  pl.BlockSpec((None, tq, LANES), lambda b, pt: (b, 0, 0)),
                  pl.BlockSpec((None, tq, 2 * NSA_KV_W), lambda b, pt: (b, 0, _cb("kvs"))),
                  pl.BlockSpec((R, 2 * KEY_TILE), lambda b, pt: (0, 0)),
                  pl.BlockSpec((None, tq, LANES), lambda b, pt: (b, 0, _cb("ng"))),
                  pl.BlockSpec((NSA_KV_HEADS, LANES, n_keys), lambda b, pt: (0, 0, 0))] + page_specs,
        out_specs=pl.BlockSpec((None, tq, MIX_W), lambda b, pt: (b, 0, 0)),
        scratch_shapes=_nsa_scratch(tq))
    return pl.pallas_call(
        kern, grid_spec=grid_spec,
        out_shape=jax.ShapeDtypeStruct((B, tq, MIX_W), F32),
        compiler_params=_cparams("parallel"), name="nsa_slc_paged")(
            pt_flat, z3, sel, z3, near_tab, z3, emat, *([pool_t] * n_pages))


def _win_state_kernel(q_ref, state_ref, new_ref, bias_ref, ng_ref, o_ref, qbd_ref, acc_ref, *, tq):
    _build_qbd(q_ref, qbd_ref, tq)
    qb = qbd_ref[...].astype(BF16)
    kt = state_ref[0].reshape(NSA_KV_W, WINDOW).astype(BF16)
    vt = state_ref[1].reshape(NSA_KV_W, WINDOW).astype(BF16)
    k_new = _new_tile(new_ref, slice(0, NSA_KV_W))
    v_new = _new_tile(new_ref, slice(NSA_KV_W, 2 * NSA_KV_W))
    bias = bias_ref[...]
    s = jnp.concatenate([_dot(qb, kt), _dot_nt(qb, k_new)], axis=1) + bias
    pr = _masked_softmax(s, bias > 0.1 * NEG).astype(BF16)
    acc_ref[...] = _dot_nt(pr[:, 0:WINDOW], vt) + _dot(pr[:, WINDOW:WINDOW + KEY_TILE], v_new)
    _assemble_heads(acc_ref, ng_ref, o_ref, tq, 2)


def _win_state(z3, state_t, layer, win_tab):
    B, tq, _ = z3.shape
    R = NSA_HEADS * tq
    assert WIN_KEYS == WINDOW + KEY_TILE and tq <= KEY_TILE
    kern = functools.partial(_win_state_kernel, tq=tq)
    return pl.pallas_call(
        kern, grid=(B,),
        in_specs=[pl.BlockSpec((None, tq, MIX_W), lambda b: (b, 0, _cb("nq"))),
                  pl.BlockSpec((None, None, 2, NSA_KV_HEADS, HEAD_DIM, WINDOW), lambda b: (layer, b, 0, 0, 0, 0)),
                  pl.BlockSpec((None, tq, 2 * NSA_KV_W), lambda b: (b, 0, _cb("kvw"))),
                  pl.BlockSpec((R, WIN_KEYS), lambda b: (0, 0)),
                  pl.BlockSpec((None, tq, LANES), lambda b: (b, 0, _cb("ng")))],
        out_specs=pl.BlockSpec((None, tq, MIX_W), lambda b: (b, 0, 0)),
        out_shape=jax.ShapeDtypeStruct((B, tq, MIX_W), F32),
        scratch_shapes=_nsa_scratch(tq),
        compiler_params=_cparams("parallel"), name="nsa_win_state")(z3, state_t, z3, win_tab, z3)


def _prep_in_proj(w):
    offs = np.concatenate([[0], np.cumsum(IN_WIDTHS)])
    lead = w.shape[:-1]

    def seg(i):
        return w[..., int(offs[i]):int(offs[i + 1])]

    def pad(x, width):
        return jnp.pad(x, [(0, 0)] * (x.ndim - 1) + [(0, width - x.shape[-1])])

    nq = seg(5).reshape(*lead, 2, 4, HEAD_DIM).swapaxes(-3, -2).reshape(*lead, MIX_W)
    ng = seg(12).reshape(*lead, 3, 2, 4).swapaxes(-2, -1).reshape(*lead, 3 * NSA_HEADS)
    out = jnp.concatenate([seg(16), seg(14), seg(15), seg(2), seg(3), nq, seg(13), seg(0), seg(1),
                           seg(6), seg(7), seg(8), seg(9), seg(10), seg(11), pad(seg(4), LANES), pad(ng, LANES)], -1)
    assert out.shape[-1] == ZW
    return out


def _t5_bucket(dist):
    n = jnp.maximum(dist, 0)
    exact = REL_BUCKETS // 2
    scaled = jnp.log(jnp.maximum(n, 1).astype(F32) / exact) / math.log(REL_MAX_DIST / exact)
    large = jnp.minimum(exact + (scaled * (REL_BUCKETS - exact)).astype(jnp.int32), REL_BUCKETS - 1)
    return jnp.where(n < exact, n, large)


def _bias_rows(rel_slots, dist, valid, tq):
    T, S = dist.shape
    bucket = _t5_bucket(dist).reshape(T // tq, 1, tq, S)
    tab = jnp.zeros((T // tq, NSA_HEADS, tq, S), F32)
    for k in range(REL_BUCKETS):
        tab = jnp.where(bucket == k, rel_slots[k].reshape(1, NSA_HEADS, 1, 1), tab)
    tab = jnp.where(valid.reshape(T // tq, 1, tq, S), tab, NEG)
    return tab.reshape(T // tq, NSA_HEADS * tq, S)


def _group_tables(rel_slots, T, tq, pos0, n_cmp_pad):
    pos_q = pos0 + np.arange(T)
    end = np.arange(n_cmp_pad) * CMP_STRIDE + CMP_BLOCK - 1
    d_cmp = jnp.asarray(pos_q[:, None] - end[None, :], jnp.int32)
    cmp_tab = _bias_rows(rel_slots, d_cmp, d_cmp >= 0, tq)
    t = np.arange(tq)
    d_near = jnp.asarray(KEY_TILE + t[:, None] - np.arange(2 * KEY_TILE)[None, :], jnp.int32)
    far = jnp.repeat(rel_slots[REL_BUCKETS - 1], tq)[:, None]
    near_tab = _bias_rows(rel_slots, d_near, d_near >= 0, tq)[0] - far
    d_win = jnp.asarray(WINDOW + t[:, None] - np.arange(WIN_KEYS)[None, :], jnp.int32)
    win_tab = _bias_rows(rel_slots, d_win, (d_win >= 0) & (d_win <= WINDOW), tq)[0]
    return cmp_tab, near_tab, win_tab


def _cover(n_cmp_pad):
    start = np.arange(n_cmp_pad) * CMP_STRIDE
    blk = np.arange(SEL_LANES) * SEL_BLOCK
    c = ((start[:, None] < blk[None, :] + SEL_BLOCK) & (start[:, None] + CMP_BLOCK > blk[None, :])).astype(np.float32)
    out = np.zeros((2, LANES, n_cmp_pad), np.float32)
    out[0, :SEL_LANES, :] = c.T
    out[1, SEL_LANES:, :] = c.T
    return jnp.asarray(out)


def kernel(x_prompt, x_sample, cache_cmp_kv, cache_slc_kv, cache_sb_kv, state_win_kv, state_gla, page_table,
           rel_bias, w_in, b_in, w_gla_a2, b_gla_a2, gla_norm_g, w_cmp1, b_cmp1, w_cmp2, cmp_pe,
           w_branch, w_o, ln1_g, ln1_b, w_ff_gate, w_ff_up, w_ff_down, ln2_g, ln2_b):
    depth = w_in.shape[0]
    B, T, _ = x_prompt.shape
    DB, DT, _ = x_sample.shape
    n_pages = page_table.shape[1]
    past = n_pages * PAGE_SIZE
    n_win_state = state_win_kv.shape[2]
    assert T % Q_BLK == 0 and DT % SUBLANES == 0 and DT <= Q_BLK and n_win_state == WINDOW and T >= WINDOW

    w_all = _prep_in_proj(w_in).astype(BF16)
    b_all = _prep_in_proj(b_in)[:, None, :]
    wa = jnp.pad(w_gla_a2, ((0, 0), (0, LANES - GLA_GATE_RANK), (0, 0)))
    eye2 = jnp.eye(2, dtype=F32)
    w1bd = jnp.einsum("gG,ljcdh->lcjgdGh", eye2, w_cmp1).reshape(
        depth, 2, 2, CMP_STRIDE * NSA_KV_W, 2 * CMP_HIDDEN).astype(BF16)
    w2bd = jnp.einsum("gG,lchd->lcghGd", eye2, w_cmp2).reshape(depth, 2, 2 * CMP_HIDDEN, NSA_KV_W).astype(BF16)
    pe2 = jnp.broadcast_to(cmp_pe.swapaxes(1, 2)[:, :, :, None, :], (depth, 2, CMP_BLOCK, 2, HEAD_DIM)).reshape(
        depth, 2, CMP_BLOCK, NSA_KV_W)
    b1t = jnp.broadcast_to(b_cmp1[:, :, None, None, :], (depth, 2, 1, 2, CMP_HIDDEN)).reshape(
        depth, 2, 1, 2 * CMP_HIDDEN)
    wb_nsa = w_branch[:, 1].reshape(depth, 2, 4, HEAD_DIM, D_MODEL).swapaxes(1, 2).reshape(depth, MIX_W, D_MODEL)
    wb = jnp.stack([w_branch[:, 0], wb_nsa, w_branch[:, 2]], 1).astype(BF16)
    wo = w_o.astype(BF16)
    wg, wu, wd = w_ff_gate.astype(BF16), w_ff_up.astype(BF16), w_ff_down.astype(BF16)
    rel_slots = rel_bias.reshape(REL_BUCKETS, 2, 4).swapaxes(1, 2).reshape(REL_BUCKETS, NSA_HEADS)

    n_seg_p = T // CMP_STRIDE
    n_seg_s = past // CMP_STRIDE
    tabs_p = _group_tables(rel_slots, T, Q_BLK, 0, n_seg_p)
    tabs_s = _group_tables(rel_slots, DT, DT, past, n_seg_s)
    cover_p, cover_s = _cover(n_seg_p), _cover(n_seg_s)
    pt_flat = page_table.reshape(-1).astype(jnp.int32)
    cmp_pool, slc_pool, sb_pool, win_state_t = (a.transpose(0, 1, 3, 4, 5, 2) for a in
                                                (cache_cmp_kv, cache_slc_kv, cache_sb_kv, state_win_kv))

    xp = x_prompt.reshape(B * T, D_MODEL)
    xs = x_sample.reshape(DB * DT, D_MODEL)
    zero_state = jnp.zeros((B, GLA_HEADS, GLA_DV, GLA_DK), F32)
    outs_p, outs_s = [], []

    def dense_tail(o, z, x, l):
        n = x.shape[0]
        o_gla, o_cmp, o_slc, o_win, o_sb = (a.reshape(n, MIX_W) for a in o)
        x1 = _merge(o_gla, o_cmp, o_slc, o_win, o_sb, z, x, wb[l], wo[l], ln1_g[l][None], ln1_b[l][None])
        return _ffn(x1, wg[l], wu[l], wd[l], ln2_g[l][None], ln2_b[l][None])

    def col(z3, name):
        off, w = COL[name]
        return z3[..., off:off + w]

    for l in range(depth):
        cw = (w1bd[l], pe2[l], b1t[l], w2bd[l])

        z = _linear(xp, w_all[l], b_all[l])
        z3 = z.reshape(B, T, ZW)
        kvc = _compress(z3, _cb("kvc"), T, *cw)
        cmp_tab, near_tab, win_tab = tabs_p
        o_gla, st = _gla(z3, zero_state, wa[l], b_gla_a2[l][None], gla_norm_g[l][None])
        o_cmp, sel = _cmp_attend(z3, kvc, cmp_tab, cover_p, Q_BLK, 0)
        o_slc = _sel_attend(z3, sel, z3, _cb("kvs"), T, near_tab, Q_BLK, 0)
        o_win = _win_attend(z3, win_tab, Q_BLK)
        o_sb = _sb_attend(z3, z3, _cb("sbkv", SB_PROMPT_HEADS * HEAD_DIM), T, Q_BLK, SB_PROMPT_HEADS, 0)
        xp = dense_tail((o_gla, o_cmp, o_slc, o_win, o_sb), z, xp, l)
        outs_p.append((col(z3, "kvc"), col(z3, "kvs"), col(z3, "sbkv"), col(z3, "kvw")[:, T - WINDOW:],
                       st.swapaxes(-1, -2)))

        z = _linear(xs, w_all[l], b_all[l])
        z3 = z.reshape(DB, DT, ZW)
        cmp_tab, near_tab, win_tab = tabs_s
        o_gla, st = _gla(z3, state_gla[l].swapaxes(-1, -2), wa[l], b_gla_a2[l][None], gla_norm_g[l][None])
        kvc = _compress_paged(cmp_pool, l, pt_flat, n_pages, *cw)
        o_cmp, sel = _cmp_attend(z3, kvc, cmp_tab, cover_s, DT, past)
        o_slc = _sel_paged(z3, sel, slc_pool, l, pt_flat, n_pages, near_tab)
        o_win = _win_state(z3, win_state_t, l, win_tab)
        o_sb = _sb_paged(z3, sb_pool, l, pt_flat, n_pages)
        xs = dense_tail((o_gla, o_cmp, o_slc, o_win, o_sb), z, xs, l)
        win_all = jnp.concatenate([state_win_kv[l].reshape(DB, n_win_state, 2 * NSA_KV_W), col(z3, "kvw")], 1)
        outs_s.append((col(z3, "kvc"), col(z3, "kvs"), col(z3, "sbkv"), win_all[:, win_all.shape[1] - WINDOW:],
                       st.swapaxes(-1, -2)))

    def stacked(outs, i, tail):
        a = jnp.stack([o[i] for o in outs])
        return a.reshape(*a.shape[:3], *tail)

    kv_tail = (2, NSA_KV_HEADS, HEAD_DIM)
    sb_tail = (2, SB_HEADS, HEAD_DIM)
    return (xp.reshape(B, T, D_MODEL), xs.reshape(DB, DT, D_MODEL),
            stacked(outs_p, 0, kv_tail), stacked(outs_s, 0, kv_tail),
            stacked(outs_p, 1, kv_tail), stacked(outs_s, 1, kv_tail),
            stacked(outs_p, 2, sb_tail), stacked(outs_s, 2, sb_tail),
            stacked(outs_p, 3, kv_tail), stacked(outs_s, 3, kv_tail),
            jnp.stack([o[4] for o in outs_p]), jnp.stack([o[4] for o in outs_s]))
```
